```python
import jax, jax.numpy as jnp
from jax import lax
import numpy as np

D_MODEL = 1024
BATCH = 4
SEQ = 8192
DEPTH = 1
DEC_BATCH = 2
DEC_SEQ = 8192
PAST_LEN = 128

GLA_HEADS = 4
GLA_DK = 128
GLA_DV = 256
GLA_GATE_RANK = 16
GLA_GATE_NORM = 16.0
GLA_CHUNK = 64
MLA_HEADS = 8
MLA_Q_RANK = 256
MLA_KV_RANK = 128
MLA_NOPE = 64
MLA_ROPE = 32
MLA_V = 64
ROPE_BASE = 10000.0
Q_BLOCK = 128
N_GROUPS = 4
EXPERTS_PER_GROUP = 8
N_EXPERTS = N_GROUPS * EXPERTS_PER_GROUP
TOP_K = 2
D_EXPERT = 512
MOE_BLOCK = 128
NORM_EPS = 1e-6
IN_SPLITS = (GLA_HEADS * GLA_DK, GLA_HEADS * GLA_DK, GLA_HEADS * GLA_DV, GLA_HEADS * GLA_DV,
             GLA_GATE_RANK, GLA_GATE_RANK, MLA_Q_RANK, MLA_KV_RANK, MLA_ROPE, D_MODEL, D_MODEL)
IN_DIM = sum(IN_SPLITS)

kernel_name = 'gated_hybrid_gla_mla_hmoe_encoder'


def rmsnorm(x, g):
    xf = x.astype(jnp.float32)
    y = xf * lax.rsqrt(jnp.mean(xf * xf, axis=-1, keepdims=True) + NORM_EPS)
    return (y * g.astype(jnp.float32)).astype(x.dtype)


def split_cols(z, sizes):
    outs = []
    off = 0
    for s in sizes:
        outs.append(z[..., off:off + s])
        off += s
    return outs


def gla_direction(q, k, v, logg, strict):
    B, S, H, dk = q.shape
    dv = v.shape[-1]
    C = GLA_CHUNK
    nc = S // C

    def chunks(t):
        return t.astype(jnp.float32).reshape(B, nc, C, H, t.shape[-1]).transpose(1, 0, 3, 2, 4)

    qc, kc, vc, gc = chunks(q), chunks(k), chunks(v), chunks(logg)
    i = jnp.arange(C)
    mask = (i[:, None] > i[None, :]) if strict else (i[:, None] >= i[None, :])

    def step(state, blk):
        qb, kb, vb, gb = blk
        b = jnp.cumsum(gb, axis=2)
        o_inter = jnp.einsum('bhcd,bhde->bhce', qb * jnp.exp(b), state)
        diff = b[:, :, :, None, :] - b[:, :, None, :, :]
        decay = jnp.exp(jnp.where(mask[:, :, None], diff, -jnp.inf))
        attn = jnp.sum(qb[:, :, :, None, :] * decay * kb[:, :, None, :, :], axis=-1)
        o_intra = jnp.einsum('bhij,bhje->bhie', attn, vb)
        b_last = b[:, :, -1, :]
        state = jnp.exp(b_last)[..., None] * state + jnp.einsum(
            'bhjd,bhje->bhde', kb * jnp.exp(b_last[:, :, None, :] - b), vb)
        return state, o_inter + o_intra

    s0 = jnp.zeros((B, H, dk, dv), jnp.float32)
    _, o = lax.scan(step, s0, (qc, kc, vc, gc))
    return o.transpose(1, 0, 3, 2, 4).reshape(B, S, H, dv).astype(v.dtype)


def rope(x, pos):
    half = MLA_ROPE // 2
    inv = ROPE_BASE ** (-jnp.arange(half, dtype=jnp.float32) / half)
    ang = pos[:, None] * inv[None, :]
    cos = jnp.cos(ang)[None, :, None, :].astype(x.dtype)
    sin = jnp.sin(ang)[None, :, None, :].astype(x.dtype)
    x1, x2 = x[..., :half], x[..., half:]
    return jnp.concatenate([x1 * cos - x2 * sin, x2 * cos + x1 * sin], axis=-1)


def block_attention(q, k, v):
    B, S, H, dq = q.shape
    dv = v.shape[-1]
    nb = S // Q_BLOCK
    scale = dq ** -0.5
    qb = q.reshape(B, nb, Q_BLOCK, H, dq).transpose(1, 0, 3, 2, 4)
    kt = k.transpose(0, 2, 1, 3)
    vt = v.transpose(0, 2, 1, 3)

    def one(qblk):
        s = jnp.einsum('bhqd,bhkd->bhqk', qblk, kt).astype(jnp.float32) * scale
        p = jax.nn.softmax(s, axis=-1)
        return jnp.einsum('bhqk,bhkd->bhqd', p.astype(vt.dtype), vt)

    o = lax.map(one, qb)
    return o.transpose(1, 0, 3, 2, 4).reshape(B, S, H * dv)


def token_mixer(h, w_in, w_gk_fwd, b_gk_fwd, w_gk_bwd, b_gk_bwd, g_gla_out, w_gla_o,
                g_q_a, w_q_b, g_kv_a, w_kv_b, g_q_nope, g_k_nope, g_q_rope, g_k_rope, w_mla_o, w_out):
    B, S, _ = h.shape
    z = h @ w_in
    zq, zk, zv, zg, zgf, zgb, zqa, zkva, zkr, zga, zgm = split_cols(z, IN_SPLITS)

    q = zq.reshape(B, S, GLA_HEADS, GLA_DK) * (GLA_DK ** -0.5)
    k = zk.reshape(B, S, GLA_HEADS, GLA_DK)
    v = zv.reshape(B, S, GLA_HEADS, GLA_DV)
    lg_f = (jax.nn.log_sigmoid((zgf @ w_gk_fwd + b_gk_fwd).astype(jnp.float32)) / GLA_GATE_NORM
            ).reshape(B, S, GLA_HEADS, GLA_DK)
    lg_b = (jax.nn.log_sigmoid((zgb @ w_gk_bwd + b_gk_bwd).astype(jnp.float32)) / GLA_GATE_NORM
            ).reshape(B, S, GLA_HEADS, GLA_DK)
    o_f = gla_direction(q, k, v, lg_f, False)
    o_b = gla_direction(q[:, ::-1], k[:, ::-1], v[:, ::-1], lg_b[:, ::-1], True)[:, ::-1]
    o = rmsnorm(o_f + o_b, g_gla_out) * jax.nn.silu(zg.reshape(B, S, GLA_HEADS, GLA_DV))
    y_gla = o.reshape(B, S, GLA_HEADS * GLA_DV) @ w_gla_o

    pos = jnp.arange(S, dtype=jnp.float32)
    qh = (rmsnorm(zqa, g_q_a) @ w_q_b).reshape(B, S, MLA_HEADS, MLA_NOPE + MLA_ROPE)
    q_nope, q_pe = qh[..., :MLA_NOPE], qh[..., MLA_NOPE:]
    kv = (rmsnorm(zkva, g_kv_a) @ w_kv_b).reshape(B, S, MLA_HEADS, MLA_NOPE + MLA_V)
    k_nope, v_m = kv[..., :MLA_NOPE], kv[..., MLA_NOPE:]
    k_pe = rope(rmsnorm(zkr[:, :, None, :], g_k_rope), pos)
    q_m = jnp.concatenate([rmsnorm(q_nope, g_q_nope), rope(rmsnorm(q_pe, g_q_rope), pos)], axis=-1)
    k_m = jnp.concatenate([rmsnorm(k_nope, g_k_nope),
                           jnp.broadcast_to(k_pe, (B, S, MLA_HEADS, MLA_ROPE))], axis=-1)
    y_mla = block_attention(q_m, k_m, v_m) @ w_mla_o

    merged = jax.nn.sigmoid(zga) * y_gla + jax.nn.sigmoid(zgm) * y_mla
    return merged @ w_out


def hier_moe(h, w_group, b_group, w_router, b_router, w_e_gate, w_e_up, w_e_down):
    B, S, D = h.shape
    T = B * S
    TK = T * TOP_K
    ht = h.reshape(T, D)
    pg = jax.nn.softmax((ht @ w_group + b_group).astype(jnp.float32), axis=-1)
    p_top, g_idx = lax.top_k(pg, 1)
    p_top, g_idx = p_top[:, 0], g_idx[:, 0]
    el = (ht @ w_router + b_router).astype(jnp.float32).reshape(T, N_GROUPS, EXPERTS_PER_GROUP)
    el = jnp.take_along_axis(el, g_idx[:, None, None], axis=1)[:, 0]
    pe = jax.nn.softmax(el, axis=-1)
    pv, pj = lax.top_k(pe, TOP_K)
    wts = p_top[:, None] * pv / jnp.sum(pv, axis=-1, keepdims=True)
    e_flat = (g_idx[:, None] * EXPERTS_PER_GROUP + pj).reshape(TK)
    w_flat = wts.reshape(TK)
    tok_flat = jnp.repeat(jnp.arange(T, dtype=jnp.int32), TOP_K)

    order = jnp.argsort(e_flat)
    e_s, tok_s, w_s = e_flat[order], tok_flat[order], w_flat[order]
    counts = jnp.bincount(e_flat, length=N_EXPERTS)
    padded = (counts + MOE_BLOCK - 1) // MOE_BLOCK * MOE_BLOCK
    pad_end = jnp.cumsum(padded)
    pad_start = pad_end - padded
    seg_start = jnp.cumsum(counts) - counts
    dest = pad_start[e_s] + jnp.arange(TK, dtype=jnp.int32) - seg_start[e_s]
    nb = -(-TK // MOE_BLOCK) + N_EXPERTS
    buf_tok = jnp.zeros((nb * MOE_BLOCK,), jnp.int32).at[dest].set(tok_s)
    buf_w = jnp.zeros((nb * MOE_BLOCK,), h.dtype).at[dest].set(w_s.astype(h.dtype))
    blk_e = jnp.minimum(jnp.searchsorted(pad_end, jnp.arange(nb, dtype=jnp.int32) * MOE_BLOCK, side='right'),
                        N_EXPERTS - 1)

    def expert_block(args):
        tok, wt, e = args
        xb = ht[tok]
        a = jax.nn.silu(xb @ w_e_gate[e]) * (xb @ w_e_up[e])
        return (a @ w_e_down[e]) * wt[:, None]

    ys = lax.map(expert_block, (buf_tok.reshape(nb, MOE_BLOCK), buf_w.reshape(nb, MOE_BLOCK), blk_e))
    out = jnp.zeros((T, D), ys.dtype).at[buf_tok].add(ys.reshape(nb * MOE_BLOCK, D))
    return out.reshape(B, S, D)


def encoder_layer(x, c, w_ada, b_ada, g_norm1, w_in, w_gk_fwd, b_gk_fwd, w_gk_bwd, b_gk_bwd, g_gla_out,
                  w_gla_o, g_q_a, w_q_b, g_kv_a, w_kv_b, g_q_nope, g_k_nope, g_q_rope, g_k_rope, w_mla_o,
                  w_out, g_norm2, w_group, b_group, w_router, b_router, w_e_gate, w_e_up, w_e_down):
    mod = jax.nn.silu(c) @ w_ada + b_ada
    sh1, sc1, ga1, sh2, sc2, ga2 = jnp.split(mod[:, None, :], 6, axis=-1)
    h = rmsnorm(x, g_norm1) * (1.0 + sc1) + sh1
    x = x + ga1 * token_mixer(h, w_in, w_gk_fwd, b_gk_fwd, w_gk_bwd, b_gk_bwd, g_gla_out, w_gla_o,
                              g_q_a, w_q_b, g_kv_a, w_kv_b, g_q_nope, g_k_nope, g_q_rope, g_k_rope,
                              w_mla_o, w_out)
    h = rmsnorm(x, g_norm2) * (1.0 + sc2) + sh2
    x = x + ga2 * hier_moe(h, w_group, b_group, w_router, b_router, w_e_gate, w_e_up, w_e_down)
    return x


def trunk(x, c, params):
    for l in range(DEPTH):
        x = encoder_layer(x, c, *[p[l] for p in params])
    return x


def _w(k, shape, fan_in, gain=1.0):
    return jax.random.normal(k, shape, jnp.float32) * (gain * fan_in ** -0.5)


def _g(k, shape):
    return 1.0 + 0.02 * jax.random.normal(k, shape, jnp.float32)


def _b(k, shape, s):
    return s * jax.random.normal(k, shape, jnp.float32)


def setup_inputs(seed: int = 0) -> dict:
    key = jax.random.key(seed)
    ks = jax.random.split(key, 32)
    L, D = DEPTH, D_MODEL
    return {
        'x_prompt': jax.random.normal(ks[0], (BATCH, SEQ, D), jnp.float32),
        'x_sample': jax.random.normal(ks[1], (DEC_BATCH, DEC_SEQ, D), jnp.float32),
        'c_prompt': jax.random.normal(ks[2], (BATCH, D), jnp.float32),
        'c_sample': jax.random.normal(ks[3], (DEC_BATCH, D), jnp.float32),
        'w_ada': _w(ks[4], (L, D, 6 * D), D, 0.5),
        'b_ada': _b(ks[5], (L, 6 * D), 0.02),
        'g_norm1': _g(ks[6], (L, D)),
        'w_in': _w(ks[7], (L, D, IN_DIM), D),
        'w_gk_fwd': _w(ks[8], (L, GLA_GATE_RANK, GLA_HEADS * GLA_DK), GLA_GATE_RANK),
        'b_gk_fwd': _b(ks[9], (L, GLA_HEADS * GLA_DK), 0.1),
        'w_gk_bwd': _w(ks[10], (L, GLA_GATE_RANK, GLA_HEADS * GLA_DK), GLA_GATE_RANK),
        'b_gk_bwd': _b(ks[11], (L, GLA_HEADS * GLA_DK), 0.1),
        'g_gla_out': _g(ks[12], (L, GLA_DV)),
        'w_gla_o': _w(ks[13], (L, GLA_HEADS * GLA_DV, D), GLA_HEADS * GLA_DV),
        'g_q_a': _g(ks[14], (L, MLA_Q_RANK)),
        'w_q_b': _w(ks[15], (L, MLA_Q_RANK, MLA_HEADS * (MLA_NOPE + MLA_ROPE)), MLA_Q_RANK),
        'g_kv_a': _g(ks[16], (L, MLA_KV_RANK)),
        'w_kv_b': _w(ks[17], (L, MLA_KV_RANK, MLA_HEADS * (MLA_NOPE + MLA_V)), MLA_KV_RANK),
        'g_q_nope': _g(ks[18], (L, MLA_NOPE)),
        'g_k_nope': _g(ks[19], (L, MLA_NOPE)),
        'g_q_rope': _g(ks[20], (L, MLA_ROPE)),
        'g_k_rope': _g(ks[21], (L, MLA_ROPE)),
        'w_mla_o': _w(ks[22], (L, MLA_HEADS * MLA_V, D), MLA_HEADS * MLA_V),
        'w_out': _w(ks[23], (L, D, D), D),
        'g_norm2': _g(ks[24], (L, D)),
        'w_group': _w(ks[25], (L, D, N_GROUPS), D),
        'b_group': _b(ks[26], (L, N_GROUPS), 0.01),
        'w_router': _w(ks[27], (L, D, N_EXPERTS), D),
        'b_router': _b(ks[28], (L, N_EXPERTS), 0.01),
        'w_e_gate': _w(ks[29], (L, N_EXPERTS, D, D_EXPERT), D),
        'w_e_up': _w(ks[30], (L, N_EXPERTS, D, D_EXPERT), D),
        'w_e_down': _w(ks[31], (L, N_EXPERTS, D_EXPERT, D), D_EXPERT),
    }


def reference(x_prompt, x_sample, c_prompt, c_sample, w_ada, b_ada, g_norm1, w_in, w_gk_fwd, b_gk_fwd,
              w_gk_bwd, b_gk_bwd, g_gla_out, w_gla_o, g_q_a, w_q_b, g_kv_a, w_kv_b, g_q_nope, g_k_nope,
              g_q_rope, g_k_rope, w_mla_o, w_out, g_norm2, w_group, b_group, w_router, b_router,
              w_e_gate, w_e_up, w_e_down):
    params = (w_ada, b_ada, g_norm1, w_in, w_gk_fwd, b_gk_fwd, w_gk_bwd, b_gk_bwd, g_gla_out, w_gla_o,
              g_q_a, w_q_b, g_kv_a, w_kv_b, g_q_nope, g_k_nope, g_q_rope, g_k_rope, w_mla_o, w_out,
              g_norm2, w_group, b_group, w_router, b_router, w_e_gate, w_e_up, w_e_down)
    y_prompt = trunk(x_prompt, c_prompt, params)
    y_sample = trunk(x_sample, c_sample, params)
    return (y_prompt, y_sample)
```

```python
import functools
import math

import numpy as np
import jax
import jax.numpy as jnp
from jax import lax
from jax.experimental import pallas as pl
from jax.experimental.pallas import tpu as pltpu

F32 = jnp.float32
BF16 = jnp.bfloat16

D_MODEL = 1024
GLA_HEADS, GLA_DK, GLA_DV, GLA_RANK = 4, 128, 256, 16
GLA_GATE_NORM = 16.0
MLA_HEADS, MLA_Q_RANK, MLA_KV_RANK = 8, 256, 128
MLA_NOPE, MLA_ROPE, MLA_V = 64, 32, 64
ROPE_BASE = 10000.0
N_GROUPS, EXPERTS_PER_GROUP, TOP_K, D_EXPERT = 4, 8, 2, 512
N_EXPERTS = N_GROUPS * EXPERTS_PER_GROUP
NORM_EPS = 1e-6

LANES = 128
SLOT = LANES
VMEM_LIMIT = 56 * 1024 * 1024

ROW_TILE = 256
GLA_CHUNK = 64
GLA_BLOCK = 512
GLA_EXP_CLAMP = 80.0
MLA_TQ = 256
MLA_TK = 512
MOE_TILE = 256
MISC_COLS = MLA_Q_RANK + MLA_KV_RANK + 3 * SLOT


def _cparams(sem):
    return pltpu.CompilerParams(dimension_semantics=sem, vmem_limit_bytes=VMEM_LIMIT)


def _dot(a, b):
    return jnp.dot(a, b, preferred_element_type=F32)


def _dot_nt(a, b):
    return lax.dot_general(a, b, (((1,), (1,)), ((), ())), preferred_element_type=F32)


def _split_bf16(x):
    hi = x.astype(BF16)
    lo = (x - hi.astype(F32)).astype(BF16)
    return hi, lo


def _dot_split_lhs(x, w):
    hi, lo = _split_bf16(x)
    return _dot(hi, w) + _dot(lo, w)


def _sigmoid(x):
    return 1.0 / (1.0 + jnp.exp(-x))


def _silu(x):
    return x * _sigmoid(x)


def _ada_kernel(c_ref, w_ref, b_ref, o_ref):
    a = _silu(c_ref[...])
    ahi, alo = _split_bf16(a)
    w = w_ref[...]
    whi, wlo = _split_bf16(w)
    o_ref[...] = _dot(ahi, whi) + _dot(ahi, wlo) + _dot(alo, whi) + b_ref[...]


def _ada(c, w_ada, b_ada):
    nb = c.shape[0]
    cp = jnp.zeros((8, D_MODEL), F32).at[:nb].set(c)
    tn = 1536
    out = pl.pallas_call(
        _ada_kernel,
        out_shape=jax.ShapeDtypeStruct((8, 6 * D_MODEL), F32),
        grid=(6 * D_MODEL // tn,),
        in_specs=[pl.BlockSpec((8, D_MODEL), lambda j: (0, 0)),
                  pl.BlockSpec((D_MODEL, tn), lambda j: (0, j)),
                  pl.BlockSpec((1, tn), lambda j: (0, j))],
        out_specs=pl.BlockSpec((8, tn), lambda j: (0, j)),
        compiler_params=_cparams(("arbitrary",)),
        name="ada",
    )(cp, w_ada, b_ada.reshape(1, -1))
    mod = out[:nb].reshape(nb, 6, D_MODEL)
    return jnp.concatenate([mod, jnp.zeros((nb, 2, D_MODEL), F32)], axis=1)


_C_Q, _C_K, _C_V, _C_G, _C_GA, _C_GM = 0, 512, 1024, 2048, 3072, 4096
_C_MISC = 5120
IN_EXT = _C_MISC + MISC_COLS


def _segmean(xsq, bd):
    return _dot_split_lhs(xsq, bd)


def _inproj_kernel(x_ref, mod_ref, g1_ref, win_ref, wgk_ref, bgk_ref, gqa_ref, wq_ref, gkva_ref,
                   wk_ref, wv_ref, bd_ref, lv_ref, cos_ref, sin_ref,
                   q_o, k_o, v_o, zg_o, zga_o, zgm_o, lgf_o, lgb_o, qm_o, km_o, vm_o, h_scr):
    x = x_ref[0]
    ms = jnp.mean(x * x, axis=-1, keepdims=True)
    h = (x * lax.rsqrt(ms + NORM_EPS) * g1_ref[...]) * (1.0 + mod_ref[0, 1:2, :]) + mod_ref[0, 0:1, :]
    h_scr[...] = h.astype(BF16)

    def proj(c0, n):
        return _dot(h_scr[...], win_ref[:, c0:c0 + n])

    q_o[0] = (proj(_C_Q, 512) * (GLA_DK ** -0.5)).astype(BF16)
    k_o[0] = proj(_C_K, 512).astype(BF16)
    v_o[0] = proj(_C_V, 1024).astype(BF16)
    zg_o[0] = proj(_C_G, 1024).astype(BF16)
    zga_o[0] = proj(_C_GA, 1024).astype(BF16)
    zgm_o[0] = proj(_C_GM, 1024).astype(BF16)
    misc = proj(_C_MISC, MISC_COLS)

    bd = bd_ref[...]
    cos_t = cos_ref[...]
    sin_t = sin_ref[...]
    g_q, g_qs = lv_ref[0:1, :], lv_ref[1:2, :]
    g_kn, g_kr, g_krs = lv_ref[2:3, :], lv_ref[3:4, :], lv_ref[4:5, :]
    scale = (MLA_NOPE + MLA_ROPE) ** -0.5

    zqa = misc[:, 0:MLA_Q_RANK]
    qa = zqa * lax.rsqrt(jnp.mean(zqa * zqa, axis=-1, keepdims=True) + NORM_EPS) * gqa_ref[...]
    qq = _dot(qa.astype(BF16), wq_ref[...])
    for hh in range(MLA_HEADS):
        xq = qq[:, hh * SLOT:(hh + 1) * SLOT]
        xs = qq[:, (MLA_HEADS + hh) * SLOT:(MLA_HEADS + hh + 1) * SLOT]
        r = lax.rsqrt(_segmean(xq * xq, bd) + NORM_EPS)
        qm = (xq * g_q * cos_t + xs * g_qs * sin_t) * (r * scale)
        qm_o[0, :, hh * SLOT:(hh + 1) * SLOT] = qm.astype(BF16)

    c0 = MLA_Q_RANK
    zkva = misc[:, c0:c0 + MLA_KV_RANK]
    kva = zkva * lax.rsqrt(jnp.mean(zkva * zkva, axis=-1, keepdims=True) + NORM_EPS) * gkva_ref[...]
    kvab = kva.astype(BF16)
    kk = _dot(kvab, wk_ref[...])
    vm_o[0] = _dot(kvab, wv_ref[...]).astype(BF16)
    c0 += MLA_KV_RANK
    kr = misc[:, c0:c0 + SLOT]
    krs = misc[:, c0 + SLOT:c0 + 2 * SLOT]
    r_kr = lax.rsqrt(_segmean(kr * kr, bd) + NORM_EPS)
    kpe = (kr * g_kr * cos_t + krs * g_krs * sin_t) * r_kr
    for hh in range(MLA_HEADS):
        xk = kk[:, hh * SLOT:(hh + 1) * SLOT]
        r = lax.rsqrt(_segmean(xk * xk, bd) + NORM_EPS)
        km_o[0, :, hh * SLOT:(hh + 1) * SLOT] = (xk * r * g_kn + kpe).astype(BF16)

    c0 += 2 * SLOT
    zgate = misc[:, c0:c0 + SLOT].astype(BF16)
    pre = _dot(zgate, wgk_ref[...]) + bgk_ref[...]
    lg = (jnp.minimum(pre, 0.0) - jnp.log(1.0 + jnp.exp(-jnp.abs(pre)))) * (1.0 / GLA_GATE_NORM)
    lgf_o[0] = lg[:, :GLA_HEADS * GLA_DK]
    lgb_o[0] = lg[:, GLA_HEADS * GLA_DK:]


def _inproj(x, mod, wp):
    nb, s, d = x.shape
    tm = min(ROW_TILE, s)
    const = lambda b, i: (0, 0)
    row = lambda b, i: (b, i, 0)

    def full(a):
        return pl.BlockSpec(a.shape, const)

    def out(n, dt):
        return jax.ShapeDtypeStruct((nb, s, n), dt), pl.BlockSpec((1, tm, n), row)

    outs = [out(512, BF16), out(512, BF16), out(1024, BF16), out(1024, BF16), out(1024, BF16),
            out(1024, BF16), out(512, F32), out(512, F32),
            out(MLA_HEADS * SLOT, BF16), out(MLA_HEADS * SLOT, BF16), out(MLA_HEADS * MLA_V, BF16)]
    return pl.pallas_call(
        _inproj_kernel,
        out_shape=[o[0] for o in outs],
        grid=(nb, s // tm),
        in_specs=[pl.BlockSpec((1, tm, d), row),
                  pl.BlockSpec((1, 8, d), lambda b, i: (b, 0, 0)),
                  full(wp["g1"]), full(wp["w_in"]), full(wp["w_gk"]), full(wp["b_gk"]),
                  full(wp["g_qa"]), full(wp["w_q"]), full(wp["g_kva"]), full(wp["w_k"]), full(wp["w_v"]),
                  full(wp["bd"]), full(wp["lanevecs"]),
                  pl.BlockSpec((tm, SLOT), lambda b, i: (i, 0)),
                  pl.BlockSpec((tm, SLOT), lambda b, i: (i, 0))],
        out_specs=[o[1] for o in outs],
        scratch_shapes=[pltpu.VMEM((tm, d), BF16)],
        compiler_params=_cparams(("arbitrary", "arbitrary")),
        name="inproj",
    )(x, mod, wp["g1"], wp["w_in"], wp["w_gk"], wp["b_gk"], wp["g_qa"], wp["w_q"], wp["g_kva"],
      wp["w_k"], wp["w_v"], wp["bd"], wp["lanevecs"], wp["cos"][:s], wp["sin"][:s])


def _gla_unit(q_ref, k_ref, v_ref, g_ref, o_ref, st_ref, r0, hh, cum, mask, first_is_total):
    c = GLA_CHUNK
    ks = slice(hh * GLA_DK, (hh + 1) * GLA_DK)
    vs = slice(hh * GLA_DV, (hh + 1) * GLA_DV)
    g = g_ref[0, pl.ds(r0, c), ks]
    b = _dot_split_lhs_rhs(cum, g)
    mid = b[c // 2:c // 2 + 1, :]
    tot = b[0:1, :] if first_is_total else b[c - 1:c, :]
    q = q_ref[0, pl.ds(r0, c), ks].astype(F32)
    k = k_ref[0, pl.ds(r0, c), ks].astype(F32)
    v = v_ref[0, pl.ds(r0, c), vs]
    q_in = (q * jnp.exp(b)).astype(BF16)
    q_mid = (q * jnp.exp(jnp.minimum(b - mid, GLA_EXP_CLAMP))).astype(BF16)
    k_mid = (k * jnp.exp(jnp.minimum(mid - b, GLA_EXP_CLAMP))).astype(BF16)
    k_out = (k * jnp.exp(tot - b)).astype(BF16)
    st = st_ref[hh]
    attn = jnp.where(mask, _dot_nt(q_mid, k_mid), 0.0).astype(BF16)
    o = _dot_nt(q_in, st.astype(BF16)) + _dot(attn, v)
    o_ref[0, pl.ds(r0, c), vs] = o
    vt = v.astype(F32).T.astype(BF16)
    st_ref[hh] = jnp.exp(tot) * st + _dot(vt, k_out)


def _dot_split_lhs_rhs(cum, g):
    hi, lo = _split_bf16(g)
    return _dot(cum, hi) + _dot(cum, lo)


def _gla_kernel(qf, kf, vf, gf, qb, kb, vb, gb, of_ref, ob_ref, sf_ref, sb_ref):
    @pl.when(pl.program_id(1) == 0)
    def _():
        sf_ref[...] = jnp.zeros_like(sf_ref)
        sb_ref[...] = jnp.zeros_like(sb_ref)

    c = GLA_CHUNK
    nch = qf.shape[1] // c
    ri = lax.broadcasted_iota(jnp.int32, (c, c), 0)
    ci = lax.broadcasted_iota(jnp.int32, (c, c), 1)
    lower = ri >= ci
    upper_strict = ci > ri
    cum_f = jnp.where(lower, 1.0, 0.0).astype(BF16)
    cum_b = jnp.where(ci >= ri, 1.0, 0.0).astype(BF16)

    def body(j, carry):
        rf = pl.multiple_of(j * c, c)
        rb = pl.multiple_of((nch - 1 - j) * c, c)
        for hh in range(GLA_HEADS):
            _gla_unit(qf, kf, vf, gf, of_ref, sf_ref, rf, hh, cum_f, lower, False)
            _gla_unit(qb, kb, vb, gb, ob_ref, sb_ref, rb, hh, cum_b, upper_strict, True)
        return carry

    lax.fori_loop(0, nch, body, 0)


def _gla(q, k, v, lgf, lgb):
    nb, s, _ = q.shape
    cb = min(GLA_BLOCK, s)
    ns = s // cb
    fwd = lambda b, i: (b, i, 0)
    bwd = lambda b, i: (b, ns - 1 - i, 0)
    hk, hv = GLA_HEADS * GLA_DK, GLA_HEADS * GLA_DV

    def specs(im):
        return [pl.BlockSpec((1, cb, hk), im), pl.BlockSpec((1, cb, hk), im),
                pl.BlockSpec((1, cb, hv), im), pl.BlockSpec((1, cb, hk), im)]

    return pl.pallas_call(
        _gla_kernel,
        out_shape=[jax.ShapeDtypeStruct((nb, s, hv), F32)] * 2,
        grid=(nb, ns),
        in_specs=specs(fwd) + specs(bwd),
        out_specs=[pl.BlockSpec((1, cb, hv), fwd), pl.BlockSpec((1, cb, hv), bwd)],
        scratch_shapes=[pltpu.VMEM((GLA_HEADS, GLA_DV, GLA_DK), F32)] * 2,
        compiler_params=_cparams(("arbitrary", "arbitrary")),
        name="gla",
    )(q, k, v, lgf, q, k, v, lgb)


def _mla_kernel(q_ref, k_ref, v_ref, o_ref):
    tq = q_ref.shape[1]
    s = k_ref.shape[1]
    tk = min(MLA_TK, s)
    nk = s // tk
    outs = []
    for hh in range(2):
        q = q_ref[0, :, hh * SLOT:(hh + 1) * SLOT]

        def body(j, carry):
            m, l, acc = carry
            r0 = pl.multiple_of(j * tk, tk)
            kt = k_ref[0, pl.ds(r0, tk), hh * SLOT:(hh + 1) * SLOT]
            vt = v_ref[0, pl.ds(r0, tk), :]
            sc = _dot_nt(q, kt)
            m_new = jnp.maximum(m, jnp.max(sc, axis=-1, keepdims=True))
            p = jnp.exp(sc - m_new)
            alpha = jnp.exp(m - m_new)
            l = alpha * l + jnp.sum(p, axis=-1, keepdims=True)
            acc = alpha * acc + _dot(p.astype(BF16), vt)
            return m_new, l, acc

        init = (jnp.full((tq, 1), -jnp.inf, F32), jnp.zeros((tq, 1), F32), jnp.zeros((tq, 2 * MLA_V), F32))
        m, l, acc = lax.fori_loop(0, nk, body, init)
        outs.append(acc / l)
    lane = lax.broadcasted_iota(jnp.int32, (tq, 2 * MLA_V), 1)
    o_ref[0] = jnp.where(lane < MLA_V, outs[0], outs[1]).astype(BF16)


def _mla(qm, km, vm):
    nb, s, _ = qm.shape
    tq = min(MLA_TQ, s)
    return pl.pallas_call(
        _mla_kernel,
        out_shape=jax.ShapeDtypeStruct((nb, s, MLA_HEADS * MLA_V), BF16),
        grid=(nb, MLA_HEADS // 2, s // tq),
        in_specs=[pl.BlockSpec((1, tq, 2 * SLOT), lambda b, p, i: (b, i, p)),
                  pl.BlockSpec((1, s, 2 * SLOT), lambda b, p, i: (b, 0, p)),
                  pl.BlockSpec((1, s, 2 * MLA_V), lambda b, p, i: (b, 0, p))],
        out_specs=pl.BlockSpec((1, tq, 2 * MLA_V), lambda b, p, i: (b, i, p)),
        compiler_params=_cparams(("arbitrary", "arbitrary", "arbitrary")),
        name="mla",
    )(qm, km, vm)


def _merge_kernel(of_ref, ob_ref, zg_ref, zga_ref, zgm_ref, om_ref, x_ref, mod_ref, ggla_ref, wglo_ref,
                  wmo_ref, wout_ref, g2_ref, wr_ref, br_ref, ltri_ref,
                  x1_o, h2_o, route_o, cnt_o, cnt_scr):
    first = (pl.program_id(0) == 0) & (pl.program_id(1) == 0)

    @pl.when(first)
    def _():
        cnt_scr[...] = jnp.zeros_like(cnt_scr)

    tm = x_ref.shape[1]
    o = of_ref[0] + ob_ref[0]
    zg = zg_ref[0].astype(F32)
    parts = []
    for hh in range(GLA_HEADS):
        seg = o[:, hh * GLA_DV:(hh + 1) * GLA_DV]
        r = lax.rsqrt(jnp.mean(seg * seg, axis=-1, keepdims=True) + NORM_EPS)
        parts.append((seg * r * ggla_ref[...]) * _silu(zg[:, hh * GLA_DV:(hh + 1) * GLA_DV]))
    og = jnp.concatenate(parts, axis=-1).astype(BF16)
    y_gla = _dot(og, wglo_ref[...])
    y_mla = _dot(om_ref[0], wmo_ref[...])
    merged = _sigmoid(zga_ref[0].astype(F32)) * y_gla + _sigmoid(zgm_ref[0].astype(F32)) * y_mla
    mix = _dot(merged.astype(BF16), wout_ref[...])
    x1 = x_ref[0] + mod_ref[0, 2:3, :] * mix
    x1_o[0] = x1
    ms = jnp.mean(x1 * x1, axis=-1, keepdims=True)
    h2 = (x1 * lax.rsqrt(ms + NORM_EPS) * g2_ref[...]) * (1.0 + mod_ref[0, 4:5, :]) + mod_ref[0, 3:4, :]
    h2_o[0] = h2

    hhi, hlo = _split_bf16(h2)
    w = wr_ref[...]
    whi, wlo = _split_bf16(w)
    logits = _dot(hhi, whi) + _dot(hhi, wlo) + _dot(hlo, whi) + br_ref[...]
    lane = lax.broadcasted_iota(jnp.int32, (tm, LANES), 1)
    neg = -jnp.inf
    is_g = (lane >= N_EXPERTS) & (lane < N_EXPERTS + N_GROUPS)
    lgm = jnp.where(is_g, logits, neg)
    mg = jnp.max(lgm, axis=-1, keepdims=True)
    sg = jnp.sum(jnp.exp(lgm - mg), axis=-1, keepdims=True)
    p_top = 1.0 / sg
    g_idx = jnp.min(jnp.where(lgm == mg, lane, 2 * LANES), axis=-1, keepdims=True) - N_EXPERTS
    in_grp = (lane >= g_idx * EXPERTS_PER_GROUP) & (lane < (g_idx + 1) * EXPERTS_PER_GROUP)
    el = jnp.where(in_grp, logits, neg)
    m1 = jnp.max(el, axis=-1, keepdims=True)
    e1 = jnp.min(jnp.where(el == m1, lane, 2 * LANES), axis=-1, keepdims=True)
    el2 = jnp.where(lane == e1, neg, el)
    m2 = jnp.max(el2, axis=-1, keepdims=True)
    e2 = jnp.min(jnp.where(el2 == m2, lane, 2 * LANES), axis=-1, keepdims=True)
    se = jnp.sum(jnp.exp(el - m1), axis=-1, keepdims=True)
    pv1 = 1.0 / se
    pv2 = jnp.exp(m2 - m1) / se
    w1 = p_top * pv1 / (pv1 + pv2)
    w2 = p_top * pv2 / (pv1 + pv2)

    oh1 = lane == e1
    oh2 = lane == e2
    oh = jnp.where(oh1 | oh2, 1.0, 0.0)
    prefix = _dot(ltri_ref[...], oh.astype(BF16)) + cnt_scr[...]
    r1 = jnp.sum(jnp.where(oh1, prefix, 0.0), axis=-1, keepdims=True)
    r2 = jnp.sum(jnp.where(oh2, prefix, 0.0), axis=-1, keepdims=True)
    cnt_scr[...] = cnt_scr[...] + jnp.sum(oh, axis=0, keepdims=True)
    cnt_o[...] = jnp.broadcast_to(cnt_scr[...], cnt_o.shape)

    route = jnp.where(lane == 0, e1.astype(F32), 0.0)
    route = jnp.where(lane == 1, e2.astype(F32), route)
    route = jnp.where(lane == 2, r1, route)
    route = jnp.where(lane == 3, r2, route)
    route = jnp.where(lane == 4, w1, route)
    route = jnp.where(lane == 5, w2, route)
    route_o[0] = route


def _merge(o_f, o_b, zg, zga, zgm, o_mla, x, mod, wp):
    nb, s, d = x.shape
    tm = min(ROW_TILE, s)
    const = lambda b, i: (0, 0)
    row = lambda b, i: (b, i, 0)

    def full(a):
        return pl.BlockSpec(a.shape, const)

    def rows(n):
        return pl.BlockSpec((1, tm, n), row)

    ltri = jnp.asarray(np.tril(np.ones((tm, tm), np.float32), -1), BF16)
    return pl.pallas_call(
        _merge_kernel,
        out_shape=[jax.ShapeDtypeStruct((nb, s, d), F32), jax.ShapeDtypeStruct((nb, s, d), F32),
                   jax.ShapeDtypeStruct((nb, s, LANES), F32), jax.ShapeDtypeStruct((8, LANES), F32)],
        grid=(nb, s // tm),
        in_specs=[rows(1024), rows(1024), rows(1024), rows(1024), rows(1024), rows(MLA_HEADS * MLA_V),
                  rows(d), pl.BlockSpec((1, 8, d), lambda b, i: (b, 0, 0)),
                  full(wp["g_gla"]), full(wp["w_gla_o"]), full(wp["w_mla_o"]), full(wp["w_out"]),
                  full(wp["g2"]), full(wp["w_route"]), full(wp["b_route"]), full(ltri)],
        out_specs=[rows(d), rows(d), rows(LANES), pl.BlockSpec((8, LANES), const)],
        scratch_shapes=[pltpu.VMEM((1, LANES), F32)],
        compiler_params=_cparams(("arbitrary", "arbitrary")),
        name="merge",
    )(o_f, o_b, zg, zga, zgm, o_mla, x, mod, wp["g_gla"], wp["w_gla_o"], wp["w_mla_o"], wp["w_out"],
      wp["g2"], wp["w_route"], wp["b_route"], ltri)


def _dispatch_kernel(dest_ref, h_ref, xs_hbm, xs_out, sem):
    del xs_hbm
    tt = h_ref.shape[0]

    def copy(r, slot):
        return pltpu.make_async_copy(h_ref.at[pl.ds(r, 1)], xs_out.at[pl.ds(dest_ref[0, 0, 2 * r + slot], 1)],
                                     sem.at[0])

    def start(r, c):
        copy(r, 0).start()
        copy(r, 1).start()
        return c

    def wait(r, c):
        copy(r, 0).wait()
        copy(r, 1).wait()
        return c

    lax.fori_loop(0, tt, start, 0)
    lax.fori_loop(0, tt, wait, 0)


def _dispatch(h2, dest, n_rows):
    t, d = h2.shape
    tt = min(ROW_TILE, t)
    nt = t // tt
    xs0 = jnp.zeros((n_rows, d), F32)
    return pl.pallas_call(
        _dispatch_kernel,
        out_shape=jax.ShapeDtypeStruct((n_rows, d), F32),
        grid=(nt,),
        in_specs=[pl.BlockSpec((1, 1, 2 * tt), lambda i: (i, 0, 0), memory_space=pltpu.SMEM),
                  pl.BlockSpec((tt, d), lambda i: (i, 0)),
                  pl.BlockSpec(memory_space=pl.ANY)],
        out_specs=pl.BlockSpec(memory_space=pl.ANY),
        scratch_shapes=[pltpu.SemaphoreType.DMA((1,))],
        input_output_aliases={2: 0},
        compiler_params=_cparams(("arbitrary",)),
        name="dispatch",
    )(dest.reshape(nt, 1, 2 * tt), h2, xs0)


def _expert_kernel(te_ref, na_ref, xs_ref, wg_ref, wu_ref, wd_ref, ys_ref):
    active = pl.program_id(0) < na_ref[0]

    @pl.when(active)
    def _():
        xb = xs_ref[...].astype(BF16)
        a = _silu(_dot(xb, wg_ref[0])) * _dot(xb, wu_ref[0])
        ys_ref[...] = _dot(a.astype(BF16), wd_ref[0])

    @pl.when(jnp.logical_not(active))
    def _():
        ys_ref[...] = jnp.zeros_like(ys_ref)


def _experts(xs, tile_expert, n_active, wp):
    n_rows, d = xs.shape
    nt = n_rows // MOE_TILE

    def tile(i, te, na):
        return (jnp.minimum(i, na[0] - 1), 0)

    def wmap(i, te, na):
        return (te[jnp.minimum(i, na[0] - 1)], 0, 0)

    return pl.pallas_call(
        _expert_kernel,
        out_shape=jax.ShapeDtypeStruct((n_rows, d), F32),
        grid_spec=pltpu.PrefetchScalarGridSpec(
            num_scalar_prefetch=2,
            grid=(nt,),
            in_specs=[pl.BlockSpec((MOE_TILE, d), tile),
                      pl.BlockSpec((1, d, D_EXPERT), wmap),
                      pl.BlockSpec((1, d, D_EXPERT), wmap),
                      pl.BlockSpec((1, D_EXPERT, d), wmap)],
            out_specs=pl.BlockSpec((MOE_TILE, d), lambda i, te, na: (i, 0))),
        compiler_params=_cparams(("arbitrary",)),
        name="experts",
    )(tile_expert, n_active, xs, wp["w_e_gate"], wp["w_e_up"], wp["w_e_down"])


def _combine_kernel(dest_ref, x1_ref, rw_ref, ga_ref, ys_hbm, o_ref, buf, sem):
    tt = x1_ref.shape[0]

    def copy(r, slot):
        return pltpu.make_async_copy(ys_hbm.at[pl.ds(dest_ref[0, 0, 2 * r + slot], 1)],
                                     buf.at[slot, pl.ds(r, 1)], sem.at[0])

    def start(r, c):
        copy(r, 0).start()
        copy(r, 1).start()
        return c

    def wait(r, c):
        copy(r, 0).wait()
        copy(r, 1).wait()
        return c

    lax.fori_loop(0, tt, start, 0)
    lax.fori_loop(0, tt, wait, 0)
    rw = rw_ref[...]
    moe = buf[0] * rw[:, 4:5] + buf[1] * rw[:, 5:6]
    o_ref[...] = x1_ref[...] + ga_ref[0, 5:6, :] * moe


def _combine(x1, route, mod, ys, dest, s):
    t, d = x1.shape
    tt = min(ROW_TILE, s)
    nt = t // tt
    per_b = s // tt
    return pl.pallas_call(
        _combine_kernel,
        out_shape=jax.ShapeDtypeStruct((t, d), F32),
        grid=(nt,),
        in_specs=[pl.BlockSpec((1, 1, 2 * tt), lambda i: (i, 0, 0), memory_space=pltpu.SMEM),
                  pl.BlockSpec((tt, d), lambda i: (i, 0)),
                  pl.BlockSpec((tt, LANES), lambda i: (i, 0)),
                  pl.BlockSpec((1, 8, d), lambda i: (i // per_b, 0, 0)),
                  pl.BlockSpec(memory_space=pl.ANY)],
        out_specs=pl.BlockSpec((tt, d), lambda i: (i, 0)),
        scratch_shapes=[pltpu.VMEM((2, tt, d), F32), pltpu.SemaphoreType.DMA((1,))],
        compiler_params=_cparams(("arbitrary",)),
        name="combine",
    )(dest.reshape(nt, 1, 2 * tt), x1, route, mod, ys)


def _moe(x1, h2, route, counts, mod, wp):
    nb, s, d = x1.shape
    t = nb * s
    route2 = route.reshape(t, LANES)
    e_flat = route2[:, 0:2].astype(jnp.int32)
    rank = route2[:, 2:4].astype(jnp.int32)
    cnt = counts[0, :N_EXPERTS].astype(jnp.int32)
    padded = (cnt + MOE_TILE - 1) // MOE_TILE * MOE_TILE
    pad_end = jnp.cumsum(padded)
    pad_start = pad_end - padded
    dest = (pad_start[e_flat] + rank).reshape(t * TOP_K)
    n_rows = t * TOP_K + N_EXPERTS * MOE_TILE
    nt = n_rows // MOE_TILE
    tile_expert = jnp.minimum(
        jnp.searchsorted(pad_end, jnp.arange(nt, dtype=jnp.int32) * MOE_TILE, side="right"),
        N_EXPERTS - 1).astype(jnp.int32)
    n_active = (pad_end[-1:] // MOE_TILE).astype(jnp.int32)
    xs = _dispatch(h2.reshape(t, d), dest, n_rows)
    ys = _experts(xs, tile_expert, n_active, wp)
    out = _combine(x1.reshape(t, d), route2, mod, ys, dest, s)
    return out.reshape(nb, s, d)


def _swap_halves(a, axis=-1):
    n = a.shape[axis] // 2
    lo, hi = jnp.split(a, [n], axis=axis)
    return jnp.concatenate([hi, lo], axis=axis)


def _slot(parts, n_lead):
    out = jnp.zeros((n_lead, SLOT), F32)
    for off, a in parts:
        out = out.at[:, off:off + a.shape[-1]].set(a)
    return out


def _prep_weights(s_max, g_norm1, w_in, w_gk_fwd, b_gk_fwd, w_gk_bwd, b_gk_bwd, g_gla_out, w_gla_o, g_q_a,
                  w_q_b, g_kv_a, w_kv_b, g_q_nope, g_k_nope, g_q_rope, g_k_rope, w_mla_o, w_out, g_norm2,
                  w_group, b_group, w_router, b_router, w_e_gate, w_e_up, w_e_down):
    d = D_MODEL
    hk, hv = GLA_HEADS * GLA_DK, GLA_HEADS * GLA_DV
    offs = np.cumsum([0, hk, hk, hv, hv, GLA_RANK, GLA_RANK, MLA_Q_RANK, MLA_KV_RANK, MLA_ROPE, d, d])
    cols = [w_in[:, offs[i]:offs[i + 1]] for i in range(11)]
    wq_, wk_, wv_, wg_, wgf, wgb, wqa, wkva, wkr, wga, wgm = cols
    kr_slot = _slot([(MLA_NOPE, wkr)], d)
    krs_slot = _slot([(MLA_NOPE, _swap_halves(wkr))], d)
    gate_slot = _slot([(0, wgf), (GLA_RANK, wgb)], d)
    w_in_ext = jnp.concatenate([wq_, wk_, wv_, wg_, wga, wgm, wqa, wkva, kr_slot, krs_slot, gate_slot],
                               axis=1).astype(BF16)
    w_gk = jnp.zeros((SLOT, 2 * hk), F32)
    w_gk = w_gk.at[0:GLA_RANK, 0:hk].set(w_gk_fwd).at[GLA_RANK:2 * GLA_RANK, hk:].set(w_gk_bwd).astype(BF16)
    b_gk = jnp.concatenate([b_gk_fwd, b_gk_bwd]).reshape(1, -1)

    wq3 = w_q_b.reshape(MLA_Q_RANK, MLA_HEADS, MLA_NOPE + MLA_ROPE)
    zq = jnp.zeros((MLA_Q_RANK, MLA_HEADS, SLOT - MLA_NOPE - MLA_ROPE), F32)
    main = jnp.concatenate([wq3, zq], axis=-1)
    swp = jnp.concatenate([jnp.zeros((MLA_Q_RANK, MLA_HEADS, MLA_NOPE), F32),
                           _swap_halves(wq3[..., MLA_NOPE:]), zq], axis=-1)
    w_q = jnp.concatenate([main.reshape(MLA_Q_RANK, -1), swp.reshape(MLA_Q_RANK, -1)], axis=1).astype(BF16)
    wkv3 = w_kv_b.reshape(MLA_KV_RANK, MLA_HEADS, MLA_NOPE + MLA_V)
    w_k = jnp.concatenate([wkv3[..., :MLA_NOPE], jnp.zeros((MLA_KV_RANK, MLA_HEADS, SLOT - MLA_NOPE), F32)],
                          axis=-1).reshape(MLA_KV_RANK, -1).astype(BF16)
    w_v = wkv3[..., MLA_NOPE:].reshape(MLA_KV_RANK, -1).astype(BF16)

    lanevecs = jnp.concatenate([
        _slot([(0, g_q_nope[None]), (MLA_NOPE, g_q_rope[None])], 1),
        _slot([(MLA_NOPE, _swap_halves(g_q_rope)[None])], 1),
        _slot([(0, g_k_nope[None])], 1),
        _slot([(MLA_NOPE, g_k_rope[None])], 1),
        _slot([(MLA_NOPE, _swap_halves(g_k_rope)[None])], 1),
        jnp.zeros((3, SLOT), F32)], axis=0)
    seg = np.zeros((SLOT, SLOT), np.float32)
    seg[:MLA_NOPE, :MLA_NOPE] = 1.0 / MLA_NOPE
    seg[MLA_NOPE:MLA_NOPE + MLA_ROPE, MLA_NOPE:MLA_NOPE + MLA_ROPE] = 1.0 / MLA_ROPE
    bd = jnp.asarray(seg, BF16)

    half = MLA_ROPE // 2
    inv = ROPE_BASE ** (-jnp.arange(half, dtype=F32) / half)
    ang = jnp.arange(s_max, dtype=F32)[:, None] * inv[None, :]
    cos, sin = jnp.cos(ang), jnp.sin(ang)
    pad = jnp.zeros((s_max, SLOT - MLA_NOPE - MLA_ROPE), F32)
    cos_t = jnp.concatenate([jnp.ones((s_max, MLA_NOPE), F32), cos, cos, pad], axis=1)
    sin_t = jnp.concatenate([jnp.zeros((s_max, MLA_NOPE), F32), -sin, sin, pad], axis=1)

    w_route = jnp.zeros((d, LANES), F32).at[:, :N_EXPERTS].set(w_router)
    w_route = w_route.at[:, N_EXPERTS:N_EXPERTS + N_GROUPS].set(w_group)
    b_route = jnp.zeros((1, LANES), F32).at[0, :N_EXPERTS].set(b_router)
    b_route = b_route.at[0, N_EXPERTS:N_EXPERTS + N_GROUPS].set(b_group)

    return dict(
        g1=g_norm1.reshape(1, d), w_in=w_in_ext, w_gk=w_gk, b_gk=b_gk,
        g_qa=g_q_a.reshape(1, -1), w_q=w_q, g_kva=g_kv_a.reshape(1, -1), w_k=w_k, w_v=w_v,
        bd=bd, lanevecs=lanevecs, cos=cos_t, sin=sin_t,
        g_gla=g_gla_out.reshape(1, -1), w_gla_o=w_gla_o.astype(BF16), w_mla_o=w_mla_o.astype(BF16),
        w_out=w_out.astype(BF16), g2=g_norm2.reshape(1, d), w_route=w_route, b_route=b_route,
        w_e_gate=w_e_gate.astype(BF16), w_e_up=w_e_up.astype(BF16), w_e_down=w_e_down.astype(BF16))


def _layer(x, c, w_ada, b_ada, wp):
    mod = _ada(c, w_ada, b_ada)
    q, k, v, zg, zga, zgm, lgf, lgb, qm, km, vm = _inproj(x, mod, wp)
    o_f, o_b = _gla(q, k, v, lgf, lgb)
    o_mla = _mla(qm, km, vm)
    x1, h2, route, counts = _merge(o_f, o_b, zg, zga, zgm, o_mla, x, mod, wp)
    return _moe(x1, h2, route, counts, mod, wp)


def kernel(x_prompt, x_sample, c_prompt, c_sample, w_ada, b_ada, g_norm1, w_in, w_gk_fwd, b_gk_fwd, w_gk_bwd, b_gk_bwd, g_gla_out, w_gla_o, g_q_a, w_q_b, g_kv_a, w_kv_b, g_q_nope, g_k_nope, g_q_rope, g_k_rope, w_mla_o, w_out, g_norm2, w_group, b_group, w_router, b_router, w_e_gate, w_e_up, w_e_down):
    assert w_ada.shape[0] == 1, "single-layer trunk"
    s_max = max(x_prompt.shape[1], x_sample.shape[1])
    wp = _prep_weights(s_max, *[p[0] for p in (
        g_norm1, w_in, w_gk_fwd, b_gk_fwd, w_gk_bwd, b_gk_bwd, g_gla_out, w_gla_o, g_q_a, w_q_b, g_kv_a,
        w_kv_b, g_q_nope, g_k_nope, g_q_rope, g_k_rope, w_mla_o, w_out, g_norm2, w_group, b_group,
        w_router, b_router, w_e_gate, w_e_up, w_e_down)])
    y_prompt = _layer(x_prompt, c_prompt, w_ada[0], b_ada[0], wp)
    y_sample = _layer(x_sample, c_sample, w_ada[0], b_ada[0], wp)
    return (y_prompt, y_sample)
```

```python
import functools
import math

import numpy as np
import jax
import jax.numpy as jnp
from jax import lax
from jax.experimental import pallas as pl
from jax.experimental.pallas import tpu as pltpu

F32 = jnp.float32
BF16 = jnp.bfloat16

D_MODEL = 1024
GLA_HEADS, GLA_DK, GLA_DV, GLA_RANK = 4, 128, 256, 16
GLA_GATE_NORM = 16.0
MLA_HEADS, MLA_Q_RANK, MLA_KV_RANK = 8, 256, 128
MLA_NOPE, MLA_ROPE, MLA_V = 64, 32, 64
ROPE_BASE = 10000.0
N_GROUPS, EXPERTS_PER_GROUP, TOP_K, D_EXPERT = 4, 8, 2, 512
N_EXPERTS = N_GROUPS * EXPERTS_PER_GROUP
NORM_EPS = 1e-6

LANES = 128
SLOT = LANES
VMEM_LIMIT = 56 * 1024 * 1024

ROW_TILE = 256
GLA_CHUNK = 64
GLA_BLOCK = 512
GLA_EXP_CLAMP = 80.0
MLA_TQ = 256
MLA_UNROLL = 2
MOE_TILE = 256
MISC_COLS = MLA_Q_RANK + MLA_KV_RANK + 3 * SLOT


def _cparams(sem):
    return pltpu.CompilerParams(dimension_semantics=sem, vmem_limit_bytes=VMEM_LIMIT)


def _dot(a, b):
    return jnp.dot(a, b, preferred_element_type=F32)


def _dot_nt(a, b):
    return lax.dot_general(a, b, (((1,), (1,)), ((), ())), preferred_element_type=F32)


def _split_bf16(x):
    hi = x.astype(BF16)
    lo = (x - hi.astype(F32)).astype(BF16)
    return hi, lo


def _dot_split_lhs(x, w):
    hi, lo = _split_bf16(x)
    return _dot(hi, w) + _dot(lo, w)


def _sigmoid(x):
    return 1.0 / (1.0 + jnp.exp(-x))


def _silu(x):
    return x * _sigmoid(x)


def _ada_kernel(c_ref, w_ref, b_ref, o_ref):
    a = _silu(c_ref[...])
    ahi, alo = _split_bf16(a)
    w = w_ref[...]
    whi, wlo = _split_bf16(w)
    o_ref[...] = _dot(ahi, whi) + _dot(ahi, wlo) + _dot(alo, whi) + b_ref[...]


def _ada(c, w_ada, b_ada):
    nb = c.shape[0]
    cp = jnp.zeros((8, D_MODEL), F32).at[:nb].set(c)
    tn = 1536
    out = pl.pallas_call(
        _ada_kernel,
        out_shape=jax.ShapeDtypeStruct((8, 6 * D_MODEL), F32),
        grid=(6 * D_MODEL // tn,),
        in_specs=[pl.BlockSpec((8, D_MODEL), lambda j: (0, 0)),
                  pl.BlockSpec((D_MODEL, tn), lambda j: (0, j)),
                  pl.BlockSpec((1, tn), lambda j: (0, j))],
        out_specs=pl.BlockSpec((8, tn), lambda j: (0, j)),
        compiler_params=_cparams(("arbitrary",)),
        name="ada",
    )(cp, w_ada, b_ada.reshape(1, -1))
    mod = out[:nb].reshape(nb, 6, D_MODEL)
    return jnp.concatenate([mod, jnp.zeros((nb, 2, D_MODEL), F32)], axis=1)


_C_Q, _C_K, _C_V, _C_G, _C_GA, _C_GM = 0, 512, 1024, 2048, 3072, 4096
_C_MISC = 5120
IN_EXT = _C_MISC + MISC_COLS


def _segmean(xsq, bd):
    return _dot_split_lhs(xsq, bd)


def _inproj_kernel(x_ref, mod_ref, g1_ref, win_ref, wgk_ref, bgk_ref, gqa_ref, wq_ref, gkva_ref,
                   wk_ref, wv_ref, bd_ref, lv_ref, cos_ref, sin_ref,
                   q_o, k_o, v_o, zg_o, zga_o, zgm_o, lgf_o, lgb_o, qm_o, km_o, vm_o, h_scr):
    x = x_ref[0]
    ms = jnp.mean(x * x, axis=-1, keepdims=True)
    h = (x * lax.rsqrt(ms + NORM_EPS) * g1_ref[...]) * (1.0 + mod_ref[0, 1:2, :]) + mod_ref[0, 0:1, :]
    h_scr[...] = h.astype(BF16)

    def proj(c0, n):
        return _dot(h_scr[...], win_ref[:, c0:c0 + n])

    q_o[0] = (proj(_C_Q, 512) * (GLA_DK ** -0.5)).astype(BF16)
    k_o[0] = proj(_C_K, 512).astype(BF16)
    v_o[0] = proj(_C_V, 1024).astype(BF16)
    zg_o[0] = proj(_C_G, 1024).astype(BF16)
    zga_o[0] = proj(_C_GA, 1024).astype(BF16)
    zgm_o[0] = proj(_C_GM, 1024).astype(BF16)
    misc = proj(_C_MISC, MISC_COLS)

    bd = bd_ref[...]
    cos_t = cos_ref[...]
    sin_t = sin_ref[...]
    g_q, g_qs = lv_ref[0:1, :], lv_ref[1:2, :]
    g_kn, g_kr, g_krs = lv_ref[2:3, :], lv_ref[3:4, :], lv_ref[4:5, :]
    scale = (MLA_NOPE + MLA_ROPE) ** -0.5 * math.log2(math.e)

    zqa = misc[:, 0:MLA_Q_RANK]
    qa = zqa * lax.rsqrt(jnp.mean(zqa * zqa, axis=-1, keepdims=True) + NORM_EPS) * gqa_ref[...]
    qq = _dot(qa.astype(BF16), wq_ref[...])
    for hh in range(MLA_HEADS):
        xq = qq[:, hh * SLOT:(hh + 1) * SLOT]
        xs = qq[:, (MLA_HEADS + hh) * SLOT:(MLA_HEADS + hh + 1) * SLOT]
        r = lax.rsqrt(_segmean(xq * xq, bd) + NORM_EPS)
        qm = (xq * g_q * cos_t + xs * g_qs * sin_t) * (r * scale)
        qm_o[0, :, hh * SLOT:(hh + 1) * SLOT] = qm.astype(BF16)

    c0 = MLA_Q_RANK
    zkva = misc[:, c0:c0 + MLA_KV_RANK]
    kva = zkva * lax.rsqrt(jnp.mean(zkva * zkva, axis=-1, keepdims=True) + NORM_EPS) * gkva_ref[...]
    kvab = kva.astype(BF16)
    kk = _dot(kvab, wk_ref[...])
    vm_o[0, 0] = _dot_nt(wv_ref[...], kvab).astype(BF16)
    c0 += MLA_KV_RANK
    kr = misc[:, c0:c0 + SLOT]
    krs = misc[:, c0 + SLOT:c0 + 2 * SLOT]
    r_kr = lax.rsqrt(_segmean(kr * kr, bd) + NORM_EPS)
    kpe = (kr * g_kr * cos_t + krs * g_krs * sin_t) * r_kr
    for hh in range(MLA_HEADS):
        xk = kk[:, hh * SLOT:(hh + 1) * SLOT]
        r = lax.rsqrt(_segmean(xk * xk, bd) + NORM_EPS)
        km_o[0, :, hh * SLOT:(hh + 1) * SLOT] = (xk * r * g_kn + kpe).astype(BF16)

    c0 += 2 * SLOT
    zgate = misc[:, c0:c0 + SLOT].astype(BF16)
    pre = _dot(zgate, wgk_ref[...]) + bgk_ref[...]
    lg = (jnp.minimum(pre, 0.0) - jnp.log(1.0 + jnp.exp(-jnp.abs(pre)))) * (1.0 / GLA_GATE_NORM)
    lgf_o[0] = lg[:, :GLA_HEADS * GLA_DK]
    lgb_o[0] = lg[:, GLA_HEADS * GLA_DK:]


def _inproj(x, mod, wp):
    nb, s, d = x.shape
    tm = min(ROW_TILE, s)
    const = lambda b, i: (0, 0)
    row = lambda b, i: (b, i, 0)

    def full(a):
        return pl.BlockSpec(a.shape, const)

    def out(n, dt):
        return jax.ShapeDtypeStruct((nb, s, n), dt), pl.BlockSpec((1, tm, n), row)

    outs = [out(512, BF16), out(512, BF16), out(1024, BF16), out(1024, BF16), out(1024, BF16),
            out(1024, BF16), out(512, F32), out(512, F32),
            out(MLA_HEADS * SLOT, BF16), out(MLA_HEADS * SLOT, BF16),
            (jax.ShapeDtypeStruct((nb, s // tm, MLA_HEADS * MLA_V, tm), BF16),
             pl.BlockSpec((1, 1, MLA_HEADS * MLA_V, tm), lambda b, i: (b, i, 0, 0)))]
    return pl.pallas_call(
        _inproj_kernel,
        out_shape=[o[0] for o in outs],
        grid=(nb, s // tm),
        in_specs=[pl.BlockSpec((1, tm, d), row),
                  pl.BlockSpec((1, 8, d), lambda b, i: (b, 0, 0)),
                  full(wp["g1"]), full(wp["w_in"]), full(wp["w_gk"]), full(wp["b_gk"]),
                  full(wp["g_qa"]), full(wp["w_q"]), full(wp["g_kva"]), full(wp["w_k"]), full(wp["w_v"]),
                  full(wp["bd"]), full(wp["lanevecs"]),
                  pl.BlockSpec((tm, SLOT), lambda b, i: (i, 0)),
                  pl.BlockSpec((tm, SLOT), lambda b, i: (i, 0))],
        out_specs=[o[1] for o in outs],
        scratch_shapes=[pltpu.VMEM((tm, d), BF16)],
        compiler_params=_cparams(("arbitrary", "arbitrary")),
        name="inproj",
    )(x, mod, wp["g1"], wp["w_in"], wp["w_gk"], wp["b_gk"], wp["g_qa"], wp["w_q"], wp["g_kva"],
      wp["w_k"], wp["w_v"], wp["bd"], wp["lanevecs"], wp["cos"][:s], wp["sin"][:s])


def _gla_unit(q_ref, k_ref, v_ref, g_ref, o_ref, st_ref, r0, hh, cum, mask, first_is_total):
    c = GLA_CHUNK
    ks = slice(hh * GLA_DK, (hh + 1) * GLA_DK)
    vs = slice(hh * GLA_DV, (hh + 1) * GLA_DV)
    g = g_ref[0, pl.ds(r0, c), ks]
    b = _dot_split_lhs_rhs(cum, g)
    mid = b[c // 2:c // 2 + 1, :]
    tot = b[0:1, :] if first_is_total else b[c - 1:c, :]
    q = q_ref[0, pl.ds(r0, c), ks].astype(F32)
    k = k_ref[0, pl.ds(r0, c), ks].astype(F32)
    v = v_ref[0, pl.ds(r0, c), vs]
    q_in = (q * jnp.exp(b)).astype(BF16)
    q_mid = (q * jnp.exp(jnp.minimum(b - mid, GLA_EXP_CLAMP))).astype(BF16)
    k_mid = (k * jnp.exp(jnp.minimum(mid - b, GLA_EXP_CLAMP))).astype(BF16)
    k_out = (k * jnp.exp(tot - b)).astype(BF16)
    st = st_ref[hh]
    attn = jnp.where(mask, _dot_nt(q_mid, k_mid), 0.0).astype(BF16)
    o = _dot_nt(q_in, st.astype(BF16)) + _dot(attn, v)
    o_ref[0, pl.ds(r0, c), vs] = o
    vt = v.astype(F32).T.astype(BF16)
    st_ref[hh] = jnp.exp(tot) * st + _dot(vt, k_out)


def _dot_split_lhs_rhs(cum, g):
    hi, lo = _split_bf16(g)
    return _dot(cum, hi) + _dot(cum, lo)


def _gla_kernel(qf, kf, vf, gf, qb, kb, vb, gb, of_ref, ob_ref, sf_ref, sb_ref):
    @pl.when(pl.program_id(1) == 0)
    def _():
        sf_ref[...] = jnp.zeros_like(sf_ref)
        sb_ref[...] = jnp.zeros_like(sb_ref)

    c = GLA_CHUNK
    nch = qf.shape[1] // c
    ri = lax.broadcasted_iota(jnp.int32, (c, c), 0)
    ci = lax.broadcasted_iota(jnp.int32, (c, c), 1)
    lower = ri >= ci
    upper_strict = ci > ri
    cum_f = jnp.where(lower, 1.0, 0.0).astype(BF16)
    cum_b = jnp.where(ci >= ri, 1.0, 0.0).astype(BF16)

    def body(j, carry):
        rf = pl.multiple_of(j * c, c)
        rb = pl.multiple_of((nch - 1 - j) * c, c)
        for hh in range(GLA_HEADS):
            _gla_unit(qf, kf, vf, gf, of_ref, sf_ref, rf, hh, cum_f, lower, False)
            _gla_unit(qb, kb, vb, gb, ob_ref, sb_ref, rb, hh, cum_b, upper_strict, True)
        return carry

    lax.fori_loop(0, nch, body, 0)


def _gla(q, k, v, lgf, lgb):
    nb, s, _ = q.shape
    cb = min(GLA_BLOCK, s)
    ns = s // cb
    fwd = lambda b, i: (b, i, 0)
    bwd = lambda b, i: (b, ns - 1 - i, 0)
    hk, hv = GLA_HEADS * GLA_DK, GLA_HEADS * GLA_DV

    def specs(im):
        return [pl.BlockSpec((1, cb, hk), im), pl.BlockSpec((1, cb, hk), im),
                pl.BlockSpec((1, cb, hv), im), pl.BlockSpec((1, cb, hk), im)]

    return pl.pallas_call(
        _gla_kernel,
        out_shape=[jax.ShapeDtypeStruct((nb, s, hv), F32)] * 2,
        grid=(nb, ns),
        in_specs=specs(fwd) + specs(bwd),
        out_specs=[pl.BlockSpec((1, cb, hv), fwd), pl.BlockSpec((1, cb, hv), bwd)],
        scratch_shapes=[pltpu.VMEM((GLA_HEADS, GLA_DV, GLA_DK), F32)] * 2,
        compiler_params=_cparams(("arbitrary", "arbitrary")),
        name="gla",
    )(q, k, v, lgf, q, k, v, lgb)


def _mla_kernel(q_ref, k_ref, vt_ref, o_ref, sc0, sc1, p0, p1):
    tq = q_ref.shape[1]
    nk = vt_ref.shape[1]
    tk = vt_ref.shape[3]
    gu = MLA_UNROLL
    ng = nk // gu
    sc, pp = (sc0, sc1), (p0, p1)
    qs = [q_ref[0, :, hh * SLOT:(hh + 1) * SLOT] for hh in range(2)]
    tiles = [(u, hh) for u in range(gu) for hh in range(2)]

    def issue_scores(g, slot):
        for u, hh in tiles:
            r0 = pl.multiple_of((g * gu + u) * tk, tk)
            kt = k_ref[0, pl.ds(r0, tk), hh * SLOT:(hh + 1) * SLOT]
            sc[slot][2 * u + hh] = _dot_nt(kt, qs[hh])

    def softmax(slot, ml):
        ml = list(ml)
        alphas = []
        for u, hh in tiles:
            m, l = ml[hh]
            st = sc[slot][2 * u + hh]
            m_new = jnp.maximum(m, jnp.max(st, axis=0, keepdims=True))
            alpha = jnp.exp2(m - m_new)
            p = jnp.exp2(st - m_new)
            pp[slot][2 * u + hh] = p.astype(BF16)
            ml[hh] = (m_new, alpha * l + jnp.sum(p, axis=0, keepdims=True))
            alphas.append(alpha)
        return tuple(ml), tuple(alphas)

    def values(g, slot, alphas, accs):
        accs = list(accs)
        for u, hh in tiles:
            vt = vt_ref[0, g * gu + u, hh * MLA_V:(hh + 1) * MLA_V, :]
            accs[hh] = alphas[2 * u + hh] * accs[hh] + _dot(vt, pp[slot][2 * u + hh])
        return tuple(accs)

    def step(g, slot, carry, more_scores):
        ml, alphas, accs = carry
        accs = values(g - 1, 1 - slot, alphas, accs)
        ml, alphas = softmax(slot, ml)
        if more_scores:
            issue_scores(g + 1, 1 - slot)
        return ml, alphas, accs

    def double_step(i, carry):
        g = 2 * i + 1
        carry = step(g, 1, carry, True)
        return step(g + 1, 0, carry, True)

    ml0 = (jnp.full((1, tq), -jnp.inf, F32), jnp.zeros((1, tq), F32))
    acc0 = jnp.zeros((MLA_V, tq), F32)
    issue_scores(0, 0)
    ml, alphas = softmax(0, (ml0, ml0))
    issue_scores(1, 1)
    carry = lax.fori_loop(0, (ng - 2) // 2, double_step, (ml, alphas, (acc0, acc0)))
    ml, alphas, accs = step(ng - 1, 1, carry, False)
    accs = values(ng - 1, 1, alphas, accs)
    outs = [(accs[hh] / ml[hh][1]).T for hh in range(2)]
    o_ref[0] = jnp.concatenate(outs, axis=-1).astype(BF16)


def _mla(qm, km, vmt):
    nb, s, _ = qm.shape
    nk, tk = vmt.shape[1], vmt.shape[3]
    tq = min(MLA_TQ, s)
    assert (nk // MLA_UNROLL) % 2 == 0 and nk % MLA_UNROLL == 0
    slot_tiles = 2 * MLA_UNROLL
    return pl.pallas_call(
        _mla_kernel,
        out_shape=jax.ShapeDtypeStruct((nb, s, MLA_HEADS * MLA_V), BF16),
        grid=(nb, MLA_HEADS // 2, s // tq),
        in_specs=[pl.BlockSpec((1, tq, 2 * SLOT), lambda b, p, i: (b, i, p)),
                  pl.BlockSpec((1, s, 2 * SLOT), lambda b, p, i: (b, 0, p)),
                  pl.BlockSpec((1, nk, 2 * MLA_V, tk), lambda b, p, i: (b, 0, p, 0))],
        out_specs=pl.BlockSpec((1, tq, 2 * MLA_V), lambda b, p, i: (b, i, p)),
        scratch_shapes=[pltpu.VMEM((slot_tiles, tk, tq), F32)] * 2 + [pltpu.VMEM((slot_tiles, tk, tq), BF16)] * 2,
        compiler_params=_cparams(("arbitrary", "arbitrary", "arbitrary")),
        name="mla",
    )(qm, km, vmt)


def _merge_kernel(of_ref, ob_ref, zg_ref, zga_ref, zgm_ref, om_ref, x_ref, mod_ref, ggla_ref, wglo_ref,
                  wmo_ref, wout_ref, g2_ref, wr_ref, br_ref, ltri_ref,
                  x1_o, h2_o, route_o, cnt_o, cnt_scr):
    first = (pl.program_id(0) == 0) & (pl.program_id(1) == 0)

    @pl.when(first)
    def _():
        cnt_scr[...] = jnp.zeros_like(cnt_scr)

    tm = x_ref.shape[1]
    o = of_ref[0] + ob_ref[0]
    zg = zg_ref[0].astype(F32)
    parts = []
    for hh in range(GLA_HEADS):
        seg = o[:, hh * GLA_DV:(hh + 1) * GLA_DV]
        r = lax.rsqrt(jnp.mean(seg * seg, axis=-1, keepdims=True) + NORM_EPS)
        parts.append((seg * r * ggla_ref[...]) * _silu(zg[:, hh * GLA_DV:(hh + 1) * GLA_DV]))
    og = jnp.concatenate(parts, axis=-1).astype(BF16)
    y_gla = _dot(og, wglo_ref[...])
    y_mla = _dot(om_ref[0], wmo_ref[...])
    merged = _sigmoid(zga_ref[0].astype(F32)) * y_gla + _sigmoid(zgm_ref[0].astype(F32)) * y_mla
    mix = _dot(merged.astype(BF16), wout_ref[...])
    x1 = x_ref[0] + mod_ref[0, 2:3, :] * mix
    x1_o[0] = x1
    ms = jnp.mean(x1 * x1, axis=-1, keepdims=True)
    h2 = (x1 * lax.rsqrt(ms + NORM_EPS) * g2_ref[...]) * (1.0 + mod_ref[0, 4:5, :]) + mod_ref[0, 3:4, :]
    h2_o[0] = h2

    hhi, hlo = _split_bf16(h2)
    w = wr_ref[...]
    whi, wlo = _split_bf16(w)
    logits = _dot(hhi, whi) + _dot(hhi, wlo) + _dot(hlo, whi) + br_ref[...]
    lane = lax.broadcasted_iota(jnp.int32, (tm, LANES), 1)
    neg = -jnp.inf
    is_g = (lane >= N_EXPERTS) & (lane < N_EXPERTS + N_GROUPS)
    lgm = jnp.where(is_g, logits, neg)
    mg = jnp.max(lgm, axis=-1, keepdims=True)
    sg = jnp.sum(jnp.exp(lgm - mg), axis=-1, keepdims=True)
    p_top = 1.0 / sg
    g_idx = jnp.min(jnp.where(lgm == mg, lane, 2 * LANES), axis=-1, keepdims=True) - N_EXPERTS
    in_grp = (lane >= g_idx * EXPERTS_PER_GROUP) & (lane < (g_idx + 1) * EXPERTS_PER_GROUP)
    el = jnp.where(in_grp, logits, neg)
    m1 = jnp.max(el, axis=-1, keepdims=True)
    e1 = jnp.min(jnp.where(el == m1, lane, 2 * LANES), axis=-1, keepdims=True)
    el2 = jnp.where(lane == e1, neg, el)
    m2 = jnp.max(el2, axis=-1, keepdims=True)
    e2 = jnp.min(jnp.where(el2 == m2, lane, 2 * LANES), axis=-1, keepdims=True)
    se = jnp.sum(jnp.exp(el - m1), axis=-1, keepdims=True)
    pv1 = 1.0 / se
    pv2 = jnp.exp(m2 - m1) / se
    w1 = p_top * pv1 / (pv1 + pv2)
    w2 = p_top * pv2 / (pv1 + pv2)

    oh1 = lane == e1
    oh2 = lane == e2
    oh = jnp.where(oh1 | oh2, 1.0, 0.0)
    prefix = _dot(ltri_ref[...], oh.astype(BF16)) + cnt_scr[...]
    r1 = jnp.sum(jnp.where(oh1, prefix, 0.0), axis=-1, keepdims=True)
    r2 = jnp.sum(jnp.where(oh2, prefix, 0.0), axis=-1, keepdims=True)
    cnt_scr[...] = cnt_scr[...] + jnp.sum(oh, axis=0, keepdims=True)
    cnt_o[...] = jnp.broadcast_to(cnt_scr[...], cnt_o.shape)

    route = jnp.where(lane == 0, e1.astype(F32), 0.0)
    route = jnp.where(lane == 1, e2.astype(F32), route)
    route = jnp.where(lane == 2, r1, route)
    route = jnp.where(lane == 3, r2, route)
    route = jnp.where(lane == 4, w1, route)
    route = jnp.where(lane == 5, w2, route)
    route_o[0] = route


def _merge(o_f, o_b, zg, zga, zgm, o_mla, x, mod, wp):
    nb, s, d = x.shape
    tm = min(ROW_TILE, s)
    const = lambda b, i: (0, 0)
    row = lambda b, i: (b, i, 0)

    def full(a):
        return pl.BlockSpec(a.shape, const)

    def rows(n):
        return pl.BlockSpec((1, tm, n), row)

    ltri = jnp.asarray(np.tril(np.ones((tm, tm), np.float32), -1), BF16)
    return pl.pallas_call(
        _merge_kernel,
        out_shape=[jax.ShapeDtypeStruct((nb, s, d), F32), jax.ShapeDtypeStruct((nb, s, d), F32),
                   jax.ShapeDtypeStruct((nb, s, LANES), F32), jax.ShapeDtypeStruct((8, LANES), F32)],
        grid=(nb, s // tm),
        in_specs=[rows(1024), rows(1024), rows(1024), rows(1024), rows(1024), rows(MLA_HEADS * MLA_V),
                  rows(d), pl.BlockSpec((1, 8, d), lambda b, i: (b, 0, 0)),
                  full(wp["g_gla"]), full(wp["w_gla_o"]), full(wp["w_mla_o"]), full(wp["w_out"]),
                  full(wp["g2"]), full(wp["w_route"]), full(wp["b_route"]), full(ltri)],
        out_specs=[rows(d), rows(d), rows(LANES), pl.BlockSpec((8, LANES), const)],
        scratch_shapes=[pltpu.VMEM((1, LANES), F32)],
        compiler_params=_cparams(("arbitrary", "arbitrary")),
        name="merge",
    )(o_f, o_b, zg, zga, zgm, o_mla, x, mod, wp["g_gla"], wp["w_gla_o"], wp["w_mla_o"], wp["w_out"],
      wp["g2"], wp["w_route"], wp["b_route"], ltri)


def _dispatch_kernel(dest_ref, h_ref, xs_hbm, xs_out, sem):
    del xs_hbm
    tt = h_ref.shape[0]

    def copy(r, slot):
        return pltpu.make_async_copy(h_ref.at[pl.ds(r, 1)], xs_out.at[pl.ds(dest_ref[0, 0, 2 * r + slot], 1)],
                                     sem.at[0])

    def start(r, c):
        copy(r, 0).start()
        copy(r, 1).start()
        return c

    def wait(r, c):
        copy(r, 0).wait()
        copy(r, 1).wait()
        return c

    lax.fori_loop(0, tt, start, 0)
    lax.fori_loop(0, tt, wait, 0)


def _dispatch(h2, dest, n_rows):
    t, d = h2.shape
    tt = min(ROW_TILE, t)
    nt = t // tt
    xs0 = jnp.zeros((n_rows, d), F32)
    return pl.pallas_call(
        _dispatch_kernel,
        out_shape=jax.ShapeDtypeStruct((n_rows, d), F32),
        grid=(nt,),
        in_specs=[pl.BlockSpec((1, 1, 2 * tt), lambda i: (i, 0, 0), memory_space=pltpu.SMEM),
                  pl.BlockSpec((tt, d), lambda i: (i, 0)),
                  pl.BlockSpec(memory_space=pl.ANY)],
        out_specs=pl.BlockSpec(memory_space=pl.ANY),
        scratch_shapes=[pltpu.SemaphoreType.DMA((1,))],
        input_output_aliases={2: 0},
        compiler_params=_cparams(("arbitrary",)),
        name="dispatch",
    )(dest.reshape(nt, 1, 2 * tt), h2, xs0)


def _expert_kernel(te_ref, na_ref, xs_ref, wg_ref, wu_ref, wd_ref, ys_ref):
    active = pl.program_id(0) < na_ref[0]

    @pl.when(active)
    def _():
        xb = xs_ref[...].astype(BF16)
        a = _silu(_dot(xb, wg_ref[0])) * _dot(xb, wu_ref[0])
        ys_ref[...] = _dot(a.astype(BF16), wd_ref[0])

    @pl.when(jnp.logical_not(active))
    def _():
        ys_ref[...] = jnp.zeros_like(ys_ref)


def _experts(xs, tile_expert, n_active, wp):
    n_rows, d = xs.shape
    nt = n_rows // MOE_TILE

    def tile(i, te, na):
        return (jnp.minimum(i, na[0] - 1), 0)

    def wmap(i, te, na):
        return (te[jnp.minimum(i, na[0] - 1)], 0, 0)

    return pl.pallas_call(
        _expert_kernel,
        out_shape=jax.ShapeDtypeStruct((n_rows, d), F32),
        grid_spec=pltpu.PrefetchScalarGridSpec(
            num_scalar_prefetch=2,
            grid=(nt,),
            in_specs=[pl.BlockSpec((MOE_TILE, d), tile),
                      pl.BlockSpec((1, d, D_EXPERT), wmap),
                      pl.BlockSpec((1, d, D_EXPERT), wmap),
                      pl.BlockSpec((1, D_EXPERT, d), wmap)],
            out_specs=pl.BlockSpec((MOE_TILE, d), lambda i, te, na: (i, 0))),
        compiler_params=_cparams(("arbitrary",)),
        name="experts",
    )(tile_expert, n_active, xs, wp["w_e_gate"], wp["w_e_up"], wp["w_e_down"])


def _combine_kernel(dest_ref, x1_ref, rw_ref, ga_ref, ys_hbm, o_ref, buf, sem):
    tt = x1_ref.shape[0]

    def copy(r, slot):
        return pltpu.make_async_copy(ys_hbm.at[pl.ds(dest_ref[0, 0, 2 * r + slot], 1)],
                                     buf.at[slot, pl.ds(r, 1)], sem.at[0])

    def start(r, c):
        copy(r, 0).start()
        copy(r, 1).start()
        return c

    def wait(r, c):
        copy(r, 0).wait()
        copy(r, 1).wait()
        return c

    lax.fori_loop(0, tt, start, 0)
    lax.fori_loop(0, tt, wait, 0)
    rw = rw_ref[...]
    moe = buf[0] * rw[:, 4:5] + buf[1] * rw[:, 5:6]
    o_ref[...] = x1_ref[...] + ga_ref[0, 5:6, :] * moe


def _combine(x1, route, mod, ys, dest, s):
    t, d = x1.shape
    tt = min(ROW_TILE, s)
    nt = t // tt
    per_b = s // tt
    return pl.pallas_call(
        _combine_kernel,
        out_shape=jax.ShapeDtypeStruct((t, d), F32),
        grid=(nt,),
        in_specs=[pl.BlockSpec((1, 1, 2 * tt), lambda i: (i, 0, 0), memory_space=pltpu.SMEM),
                  pl.BlockSpec((tt, d), lambda i: (i, 0)),
                  pl.BlockSpec((tt, LANES), lambda i: (i, 0)),
                  pl.BlockSpec((1, 8, d), lambda i: (i // per_b, 0, 0)),
                  pl.BlockSpec(memory_space=pl.ANY)],
        out_specs=pl.BlockSpec((tt, d), lambda i: (i, 0)),
        scratch_shapes=[pltpu.VMEM((2, tt, d), F32), pltpu.SemaphoreType.DMA((1,))],
        compiler_params=_cparams(("arbitrary",)),
        name="combine",
    )(dest.reshape(nt, 1, 2 * tt), x1, route, mod, ys)


def _moe(x1, h2, route, counts, mod, wp):
    nb, s, d = x1.shape
    t = nb * s
    route2 = route.reshape(t, LANES)
    e_flat = route2[:, 0:2].astype(jnp.int32)
    rank = route2[:, 2:4].astype(jnp.int32)
    cnt = counts[0, :N_EXPERTS].astype(jnp.int32)
    padded = (cnt + MOE_TILE - 1) // MOE_TILE * MOE_TILE
    pad_end = jnp.cumsum(padded)
    pad_start = pad_end - padded
    dest = (pad_start[e_flat] + rank).reshape(t * TOP_K)
    n_rows = t * TOP_K + N_EXPERTS * MOE_TILE
    nt = n_rows // MOE_TILE
    tile_row = jnp.arange(nt, dtype=jnp.int32) * MOE_TILE
    tile_expert = jnp.minimum(jnp.sum((pad_end[None, :] <= tile_row[:, None]).astype(jnp.int32), axis=1),
                              N_EXPERTS - 1)
    n_active = (pad_end[-1:] // MOE_TILE).astype(jnp.int32)
    xs = _dispatch(h2.reshape(t, d), dest, n_rows)
    ys = _experts(xs, tile_expert, n_active, wp)
    out = _combine(x1.reshape(t, d), route2, mod, ys, dest, s)
    return out.reshape(nb, s, d)


def _swap_halves(a, axis=-1):
    n = a.shape[axis] // 2
    lo, hi = jnp.split(a, [n], axis=axis)
    return jnp.concatenate([hi, lo], axis=axis)


def _slot(parts, n_lead):
    out = jnp.zeros((n_lead, SLOT), F32)
    for off, a in parts:
        out = out.at[:, off:off + a.shape[-1]].set(a)
    return out


def _prep_weights(s_max, g_norm1, w_in, w_gk_fwd, b_gk_fwd, w_gk_bwd, b_gk_bwd, g_gla_out, w_gla_o, g_q_a,
                  w_q_b, g_kv_a, w_kv_b, g_q_nope, g_k_nope, g_q_rope, g_k_rope, w_mla_o, w_out, g_norm2,
                  w_group, b_group, w_router, b_router, w_e_gate, w_e_up, w_e_down):
    d = D_MODEL
    hk, hv = GLA_HEADS * GLA_DK, GLA_HEADS * GLA_DV
    offs = np.cumsum([0, hk, hk, hv, hv, GLA_RANK, GLA_RANK, MLA_Q_RANK, MLA_KV_RANK, MLA_ROPE, d, d])
    cols = [w_in[:, offs[i]:offs[i + 1]] for i in range(11)]
    wq_, wk_, wv_, wg_, wgf, wgb, wqa, wkva, wkr, wga, wgm = cols
    kr_slot = _slot([(MLA_NOPE, wkr)], d)
    krs_slot = _slot([(MLA_NOPE, _swap_halves(wkr))], d)
    gate_slot = _slot([(0, wgf), (GLA_RANK, wgb)], d)
    w_in_ext = jnp.concatenate([wq_, wk_, wv_, wg_, wga, wgm, wqa, wkva, kr_slot, krs_slot, gate_slot],
                               axis=1).astype(BF16)
    w_gk = jnp.zeros((SLOT, 2 * hk), F32)
    w_gk = w_gk.at[0:GLA_RANK, 0:hk].set(w_gk_fwd).at[GLA_RANK:2 * GLA_RANK, hk:].set(w_gk_bwd).astype(BF16)
    b_gk = jnp.concatenate([b_gk_fwd, b_gk_bwd]).reshape(1, -1)

    wq3 = w_q_b.reshape(MLA_Q_RANK, MLA_HEADS, MLA_NOPE + MLA_ROPE)
    zq = jnp.zeros((MLA_Q_RANK, MLA_HEADS, SLOT - MLA_NOPE - MLA_ROPE), F32)
    main = jnp.concatenate([wq3, zq], axis=-1)
    swp = jnp.concatenate([jnp.zeros((MLA_Q_RANK, MLA_HEADS, MLA_NOPE), F32),
                           _swap_halves(wq3[..., MLA_NOPE:]), zq], axis=-1)
    w_q = jnp.concatenate([main.reshape(MLA_Q_RANK, -1), swp.reshape(MLA_Q_RANK, -1)], axis=1).astype(BF16)
    wkv3 = w_kv_b.reshape(MLA_KV_RANK, MLA_HEADS, MLA_NOPE + MLA_V)
    w_k = jnp.concatenate([wkv3[..., :MLA_NOPE], jnp.zeros((MLA_KV_RANK, MLA_HEADS, SLOT - MLA_NOPE), F32)],
                          axis=-1).reshape(MLA_KV_RANK, -1).astype(BF16)
    w_v = wkv3[..., MLA_NOPE:].reshape(MLA_KV_RANK, -1).T.astype(BF16)

    lanevecs = jnp.concatenate([
        _slot([(0, g_q_nope[None]), (MLA_NOPE, g_q_rope[None])], 1),
        _slot([(MLA_NOPE, _swap_halves(g_q_rope)[None])], 1),
        _slot([(0, g_k_nope[None])], 1),
        _slot([(MLA_NOPE, g_k_rope[None])], 1),
        _slot([(MLA_NOPE, _swap_halves(g_k_rope)[None])], 1),
        jnp.zeros((3, SLOT), F32)], axis=0)
    seg = np.zeros((SLOT, SLOT), np.float32)
    seg[:MLA_NOPE, :MLA_NOPE] = 1.0 / MLA_NOPE
    seg[MLA_NOPE:MLA_NOPE + MLA_ROPE, MLA_NOPE:MLA_NOPE + MLA_ROPE] = 1.0 / MLA_ROPE
    bd = jnp.asarray(seg, BF16)

    half = MLA_ROPE // 2
    inv = ROPE_BASE ** (-jnp.arange(half, dtype=F32) / half)
    ang = jnp.arange(s_max, dtype=F32)[:, None] * inv[None, :]
    cos, sin = jnp.cos(ang), jnp.sin(ang)
    pad = jnp.zeros((s_max, SLOT - MLA_NOPE - MLA_ROPE), F32)
    cos_t = jnp.concatenate([jnp.ones((s_max, MLA_NOPE), F32), cos, cos, pad], axis=1)
    sin_t = jnp.concatenate([jnp.zeros((s_max, MLA_NOPE), F32), -sin, sin, pad], axis=1)

    w_route = jnp.zeros((d, LANES), F32).at[:, :N_EXPERTS].set(w_router)
    w_route = w_route.at[:, N_EXPERTS:N_EXPERTS + N_GROUPS].set(w_group)
    b_route = jnp.zeros((1, LANES), F32).at[0, :N_EXPERTS].set(b_router)
    b_route = b_route.at[0, N_EXPERTS:N_EXPERTS + N_GROUPS].set(b_group)

    return dict(
        g1=g_norm1.reshape(1, d), w_in=w_in_ext, w_gk=w_gk, b_gk=b_gk,
        g_qa=g_q_a.reshape(1, -1), w_q=w_q, g_kva=g_kv_a.reshape(1, -1), w_k=w_k, w_v=w_v,
        bd=bd, lanevecs=lanevecs, cos=cos_t, sin=sin_t,
        g_gla=g_gla_out.reshape(1, -1), w_gla_o=w_gla_o.astype(BF16), w_mla_o=w_mla_o.astype(BF16),
        w_out=w_out.astype(BF16), g2=g_norm2.reshape(1, d), w_route=w_route, b_route=b_route,
        w_e_gate=w_e_gate.astype(BF16), w_e_up=w_e_up.astype(BF16), w_e_down=w_e_down.astype(BF16))


def _layer(x, c, w_ada, b_ada, wp):
    mod = _ada(c, w_ada, b_ada)
    q, k, v, zg, zga, zgm, lgf, lgb, qm, km, vm = _inproj(x, mod, wp)
    o_f, o_b = _gla(q, k, v, lgf, lgb)
    o_mla = _mla(qm, km, vm)
    x1, h2, route, counts = _merge(o_f, o_b, zg, zga, zgm, o_mla, x, mod, wp)
    return _moe(x1, h2, route, counts, mod, wp)


def kernel(x_prompt, x_sample, c_prompt, c_sample, w_ada, b_ada, g_norm1, w_in, w_gk_fwd, b_gk_fwd, w_gk_bwd, b_gk_bwd, g_gla_out, w_gla_o, g_q_a, w_q_b, g_kv_a, w_kv_b, g_q_nope, g_k_nope, g_q_rope, g_k_rope, w_mla_o, w_out, g_norm2, w_group, b_group, w_router, b_router, w_e_gate, w_e_up, w_e_down):
    assert w_ada.shape[0] == 1, "single-layer trunk"
    s_max = max(x_prompt.shape[1], x_sample.shape[1])
    wp = _prep_weights(s_max, *[p[0] for p in (
        g_norm1, w_in, w_gk_fwd, b_gk_fwd, w_gk_bwd, b_gk_bwd, g_gla_out, w_gla_o, g_q_a, w_q_b, g_kv_a,
        w_kv_b, g_q_nope, g_k_nope, g_q_rope, g_k_rope, w_mla_o, w_out, g_norm2, w_group, b_group,
        w_router, b_router, w_e_gate, w_e_up, w_e_down)])
    y_prompt = _layer(x_prompt, c_prompt, w_ada[0], b_ada[0], wp)
    y_sample = _layer(x_sample, c_sample, w_ada[0], b_ada[0], wp)
    return (y_prompt, y_sample)
```

```python
import functools
import math

import numpy as np
import jax
import jax.numpy as jnp
from jax import lax
from jax.experimental import pallas as pl
from jax.experimental.pallas import tpu as pltpu

F32 = jnp.float32
BF16 = jnp.bfloat16

D_MODEL = 1024
GLA_HEADS, GLA_DK, GLA_DV, GLA_RANK = 4, 128, 256, 16
GLA_GATE_NORM = 16.0
MLA_HEADS, MLA_Q_RANK, MLA_KV_RANK = 8, 256, 128
MLA_NOPE, MLA_ROPE, MLA_V = 64, 32, 64
ROPE_BASE = 10000.0
N_GROUPS, EXPERTS_PER_GROUP, TOP_K, D_EXPERT = 4, 8, 2, 512
N_EXPERTS = N_GROUPS * EXPERTS_PER_GROUP
NORM_EPS = 1e-6

LANES = 128
SLOT = LANES
VMEM_LIMIT = 56 * 1024 * 1024

ROW_TILE = 256
GLA_CHUNK = 64
GLA_BLOCK = 256
GLA_EXP_CLAMP = 80.0
MLA_TQ = 256
MLA_VROWS = MLA_V + 16
MLA_UNROLL = 2
MOE_TILE = 256
MISC_COLS = MLA_Q_RANK + MLA_KV_RANK + 3 * SLOT


def _cparams(sem):
    return pltpu.CompilerParams(dimension_semantics=sem, vmem_limit_bytes=VMEM_LIMIT)


def _dot(a, b):
    return jnp.dot(a, b, preferred_element_type=F32)


def _dot_nt(a, b):
    return lax.dot_general(a, b, (((1,), (1,)), ((), ())), preferred_element_type=F32)


def _split_bf16(x):
    hi = x.astype(BF16)
    lo = (x - hi.astype(F32)).astype(BF16)
    return hi, lo


def _dot_split_lhs(x, w):
    hi, lo = _split_bf16(x)
    return _dot(hi, w) + _dot(lo, w)


def _sigmoid(x):
    return 1.0 / (1.0 + jnp.exp(-x))


def _silu(x):
    return x * _sigmoid(x)


def _ada_kernel(c_ref, w_ref, b_ref, o_ref):
    a = _silu(c_ref[...])
    ahi, alo = _split_bf16(a)
    w = w_ref[...]
    whi, wlo = _split_bf16(w)
    o_ref[...] = _dot(ahi, whi) + _dot(ahi, wlo) + _dot(alo, whi) + b_ref[...]


def _ada(c, w_ada, b_ada):
    nb = c.shape[0]
    cp = jnp.zeros((8, D_MODEL), F32).at[:nb].set(c)
    tn = 1536
    out = pl.pallas_call(
        _ada_kernel,
        out_shape=jax.ShapeDtypeStruct((8, 6 * D_MODEL), F32),
        grid=(6 * D_MODEL // tn,),
        in_specs=[pl.BlockSpec((8, D_MODEL), lambda j: (0, 0)),
                  pl.BlockSpec((D_MODEL, tn), lambda j: (0, j)),
                  pl.BlockSpec((1, tn), lambda j: (0, j))],
        out_specs=pl.BlockSpec((8, tn), lambda j: (0, j)),
        compiler_params=_cparams(("arbitrary",)),
        name="ada",
    )(cp, w_ada, b_ada.reshape(1, -1))
    mod = out[:nb].reshape(nb, 6, D_MODEL)
    return jnp.concatenate([mod, jnp.zeros((nb, 2, D_MODEL), F32)], axis=1)


_C_Q, _C_K, _C_V, _C_G, _C_GA, _C_GM = 0, 512, 1024, 2048, 3072, 4096
_C_MISC = 5120
IN_EXT = _C_MISC + MISC_COLS


def _segmean(xsq, bd):
    return _dot_split_lhs(xsq, bd)


def _inproj_kernel(x_ref, mod_ref, g1_ref, win_ref, wgk_ref, bgk_ref, gqa_ref, wq_ref, gkva_ref,
                   wk_ref, wv_ref, vone_ref, bd_ref, lv_ref, cv_ref, cos_ref, sin_ref, cosc_ref, sinc_ref,
                   q_o, k_o, v_o, zg_o, zga_o, zgm_o, lgf_o, lgb_o, qm_o, km_o, vm_o, h_scr):
    x = x_ref[0]
    ms = jnp.mean(x * x, axis=-1, keepdims=True)
    h = (x * lax.rsqrt(ms + NORM_EPS) * g1_ref[...]) * (1.0 + mod_ref[0, 1:2, :]) + mod_ref[0, 0:1, :]
    h_scr[...] = h.astype(BF16)

    def proj(c0, n):
        return _dot(h_scr[...], win_ref[:, c0:c0 + n])

    q_o[0] = (proj(_C_Q, 512) * (GLA_DK ** -0.5)).astype(BF16)
    k_o[0] = proj(_C_K, 512).astype(BF16)
    v_o[0] = proj(_C_V, 1024).astype(BF16)
    zg_o[0] = proj(_C_G, 1024).astype(BF16)
    zga_o[0] = proj(_C_GA, 1024).astype(BF16)
    zgm_o[0] = proj(_C_GM, 1024).astype(BF16)
    misc = proj(_C_MISC, MISC_COLS)

    bd = bd_ref[...]
    cos_t = cos_ref[...]
    sin_t = sin_ref[...]
    g_kn, g_kr, g_krs = lv_ref[2:3, :], lv_ref[3:4, :], lv_ref[4:5, :]
    scale = (MLA_NOPE + MLA_ROPE) ** -0.5 * math.log2(math.e)

    zqa = misc[:, 0:MLA_Q_RANK]
    qa = zqa * lax.rsqrt(jnp.mean(zqa * zqa, axis=-1, keepdims=True) + NORM_EPS) * gqa_ref[...]
    qq = _dot_nt(wq_ref[...], qa.astype(BF16))
    cos_c, sin_c = cosc_ref[...], sinc_ref[...]
    gc_q, gc_qs = cv_ref[:, 0:1], cv_ref[:, 1:2]
    for hh in range(MLA_HEADS):
        xq = qq[hh * SLOT:(hh + 1) * SLOT, :]
        xs = qq[(MLA_HEADS + hh) * SLOT:(MLA_HEADS + hh + 1) * SLOT, :]
        hi, lo = _split_bf16(xq * xq)
        r = lax.rsqrt(_dot(bd, hi) + _dot(bd, lo) + NORM_EPS)
        qm = (xq * gc_q * cos_c + xs * gc_qs * sin_c) * (r * scale)
        qm_o[0, hh * SLOT:(hh + 1) * SLOT, :] = qm.astype(BF16)

    c0 = MLA_Q_RANK
    zkva = misc[:, c0:c0 + MLA_KV_RANK]
    kva = zkva * lax.rsqrt(jnp.mean(zkva * zkva, axis=-1, keepdims=True) + NORM_EPS) * gkva_ref[...]
    kvab = kva.astype(BF16)
    kk = _dot(kvab, wk_ref[...])
    vm_o[0, 0] = (_dot_nt(wv_ref[...], kvab) + vone_ref[...]).astype(BF16)
    c0 += MLA_KV_RANK
    kr = misc[:, c0:c0 + SLOT]
    krs = misc[:, c0 + SLOT:c0 + 2 * SLOT]
    r_kr = lax.rsqrt(_segmean(kr * kr, bd) + NORM_EPS)
    kpe = (kr * g_kr * cos_t + krs * g_krs * sin_t) * r_kr
    for hh in range(MLA_HEADS):
        xk = kk[:, hh * SLOT:(hh + 1) * SLOT]
        r = lax.rsqrt(_segmean(xk * xk, bd) + NORM_EPS)
        km_o[0, :, hh * SLOT:(hh + 1) * SLOT] = (xk * r * g_kn + kpe).astype(BF16)

    c0 += 2 * SLOT
    zgate = misc[:, c0:c0 + SLOT].astype(BF16)
    pre = _dot(zgate, wgk_ref[...]) + bgk_ref[...]
    lg = (jnp.minimum(pre, 0.0) - jnp.log(1.0 + jnp.exp(-jnp.abs(pre)))) * (1.0 / GLA_GATE_NORM)
    lgf_o[0] = lg[:, :GLA_HEADS * GLA_DK]
    lgb_o[0] = lg[:, GLA_HEADS * GLA_DK:]


def _inproj(x, mod, wp):
    nb, s, d = x.shape
    tm = min(ROW_TILE, s)
    const = lambda b, i: (0, 0)
    row = lambda b, i: (b, i, 0)

    def full(a):
        return pl.BlockSpec(a.shape, const)

    def out(n, dt):
        return jax.ShapeDtypeStruct((nb, s, n), dt), pl.BlockSpec((1, tm, n), row)

    outs = [out(512, BF16), out(512, BF16), out(1024, BF16), out(1024, BF16), out(1024, BF16),
            out(1024, BF16), out(512, F32), out(512, F32),
            (jax.ShapeDtypeStruct((nb, MLA_HEADS * SLOT, s), BF16),
             pl.BlockSpec((1, MLA_HEADS * SLOT, tm), lambda b, i: (b, 0, i))),
            out(MLA_HEADS * SLOT, BF16),
            (jax.ShapeDtypeStruct((nb, s // tm, MLA_HEADS * MLA_VROWS, tm), BF16),
             pl.BlockSpec((1, 1, MLA_HEADS * MLA_VROWS, tm), lambda b, i: (b, i, 0, 0)))]
    return pl.pallas_call(
        _inproj_kernel,
        out_shape=[o[0] for o in outs],
        grid=(nb, s // tm),
        in_specs=[pl.BlockSpec((1, tm, d), row),
                  pl.BlockSpec((1, 8, d), lambda b, i: (b, 0, 0)),
                  full(wp["g1"]), full(wp["w_in"]), full(wp["w_gk"]), full(wp["b_gk"]),
                  full(wp["g_qa"]), full(wp["w_q"]), full(wp["g_kva"]), full(wp["w_k"]), full(wp["w_v"]),
                  full(wp["v_ones"]), full(wp["bd"]), full(wp["lanevecs"]), full(wp["colvecs"]),
                  pl.BlockSpec((tm, SLOT), lambda b, i: (i, 0)),
                  pl.BlockSpec((tm, SLOT), lambda b, i: (i, 0)),
                  pl.BlockSpec((SLOT, tm), lambda b, i: (0, i)),
                  pl.BlockSpec((SLOT, tm), lambda b, i: (0, i))],
        out_specs=[o[1] for o in outs],
        scratch_shapes=[pltpu.VMEM((tm, d), BF16)],
        compiler_params=_cparams(("arbitrary", "arbitrary")),
        name="inproj",
    )(x, mod, wp["g1"], wp["w_in"], wp["w_gk"], wp["b_gk"], wp["g_qa"], wp["w_q"], wp["g_kva"],
      wp["w_k"], wp["w_v"], wp["v_ones"], wp["bd"], wp["lanevecs"], wp["colvecs"],
      wp["cos"][:s], wp["sin"][:s], wp["cos"][:s].T, wp["sin"][:s].T)


def _dot_tn(a, b):
    return lax.dot_general(a, b, (((0,), (0,)), ((), ())), preferred_element_type=F32)


def _gla_direction(q_ref, k_ref, v_ref, g_ref, o_ref, st_ref, scr, cum, mask, reverse):
    qin_s, qmid_s, kmid_s, kout_s, tot_s, u_s = scr
    c = GLA_CHUNK
    nch = q_ref.shape[1] // c
    rows = [slice(j * c, (j + 1) * c) for j in range(nch)]
    ks = [slice(h * GLA_DK, (h + 1) * GLA_DK) for h in range(GLA_HEADS)]
    vs = [slice(h * GLA_DV, (h + 1) * GLA_DV) for h in range(GLA_HEADS)]

    for j in range(nch):
        hi, lo = _split_bf16(g_ref[0, rows[j], :])
        b = _dot(cum, hi) + _dot(cum, lo)
        mid = b[c // 2:c // 2 + 1, :]
        tot = b[0:1, :] if reverse else b[c - 1:c, :]
        q = q_ref[0, rows[j], :].astype(F32)
        k = k_ref[0, rows[j], :].astype(F32)
        qin_s[rows[j], :] = (q * jnp.exp(b)).astype(BF16)
        qmid_s[rows[j], :] = (q * jnp.exp(jnp.minimum(b - mid, GLA_EXP_CLAMP))).astype(BF16)
        kmid_s[rows[j], :] = (k * jnp.exp(jnp.minimum(mid - b, GLA_EXP_CLAMP))).astype(BF16)
        kout_s[rows[j], :] = (k * jnp.exp(tot - b)).astype(BF16)
        tot_s[j:j + 1, :] = tot

    attn = {}
    for j in range(nch):
        for h in range(GLA_HEADS):
            a = _dot_nt(qmid_s[rows[j], ks[h]], kmid_s[rows[j], ks[h]])
            attn[j, h] = jnp.where(mask, a, 0.0).astype(BF16)

    for j in range(nch):
        for h in range(GLA_HEADS):
            v = v_ref[0, rows[j], vs[h]]
            o_ref[0, rows[j], vs[h]] = _dot(attn[j, h], v)
            u_s[j * GLA_HEADS + h] = _dot_tn(kout_s[rows[j], ks[h]], v)

    for j in (range(nch - 1, -1, -1) if reverse else range(nch)):
        for h in range(GLA_HEADS):
            st = st_ref[h]
            o_ref[0, rows[j], vs[h]] += _dot(qin_s[rows[j], ks[h]], st.astype(BF16))
            tot = jnp.broadcast_to(tot_s[j:j + 1, ks[h]], (GLA_DK, GLA_DK))
            dec = jnp.exp(tot.T)
            st_ref[h] = jnp.concatenate([dec] * (GLA_DV // GLA_DK), axis=1) * st + u_s[j * GLA_HEADS + h]


def _gla_kernel(qf, kf, vf, gf, qb, kb, vb, gb, of_ref, ob_ref, sf_ref, sb_ref, *scr):
    @pl.when(pl.program_id(1) == 0)
    def _():
        sf_ref[...] = jnp.zeros_like(sf_ref)
        sb_ref[...] = jnp.zeros_like(sb_ref)

    c = GLA_CHUNK
    ri = lax.broadcasted_iota(jnp.int32, (c, c), 0)
    ci = lax.broadcasted_iota(jnp.int32, (c, c), 1)
    cum_f = jnp.where(ri >= ci, 1.0, 0.0).astype(BF16)
    cum_b = jnp.where(ci >= ri, 1.0, 0.0).astype(BF16)
    n = len(scr) // 2
    _gla_direction(qf, kf, vf, gf, of_ref, sf_ref, scr[:n], cum_f, ri >= ci, False)
    _gla_direction(qb, kb, vb, gb, ob_ref, sb_ref, scr[n:], cum_b, ci > ri, True)


def _gla(q, k, v, lgf, lgb):
    nb, s, _ = q.shape
    cb = min(GLA_BLOCK, s)
    ns = s // cb
    nch = cb // GLA_CHUNK
    fwd = lambda b, i: (b, i, 0)
    bwd = lambda b, i: (b, ns - 1 - i, 0)
    hk, hv = GLA_HEADS * GLA_DK, GLA_HEADS * GLA_DV

    def specs(im):
        return [pl.BlockSpec((1, cb, hk), im), pl.BlockSpec((1, cb, hk), im),
                pl.BlockSpec((1, cb, hv), im), pl.BlockSpec((1, cb, hk), im)]

    per_dir = [pltpu.VMEM((cb, hk), BF16)] * 4 + [pltpu.VMEM((8, hk), F32),
                                                  pltpu.VMEM((nch * GLA_HEADS, GLA_DK, GLA_DV), F32)]
    return pl.pallas_call(
        _gla_kernel,
        out_shape=[jax.ShapeDtypeStruct((nb, s, hv), F32)] * 2,
        grid=(nb, ns),
        in_specs=specs(fwd) + specs(bwd),
        out_specs=[pl.BlockSpec((1, cb, hv), fwd), pl.BlockSpec((1, cb, hv), bwd)],
        scratch_shapes=[pltpu.VMEM((GLA_HEADS, GLA_DK, GLA_DV), F32)] * 2 + per_dir * 2,
        compiler_params=_cparams(("arbitrary", "arbitrary")),
        name="gla",
    )(q, k, v, lgf, q, k, v, lgb)


def _mla_kernel(q_ref, k_ref, vt_ref, o_ref, sc0, sc1, p0, p1):
    tq = q_ref.shape[2]
    nk = vt_ref.shape[1]
    tk = vt_ref.shape[3]
    gu = MLA_UNROLL
    ng = nk // gu
    sc, pp = (sc0, sc1), (p0, p1)
    qs = [q_ref[0, hh * SLOT:(hh + 1) * SLOT, :] for hh in range(2)]
    tiles = [(u, hh) for u in range(gu) for hh in range(2)]

    def issue_scores(g, slot):
        for u, hh in tiles:
            r0 = pl.multiple_of((g * gu + u) * tk, tk)
            kt = k_ref[0, pl.ds(r0, tk), hh * SLOT:(hh + 1) * SLOT]
            sc[slot][2 * u + hh] = _dot(kt, qs[hh])

    def softmax(slot, ms):
        ms = list(ms)
        alphas = []
        for u, hh in tiles:
            st = sc[slot][2 * u + hh]
            m_new = jnp.maximum(ms[hh], jnp.max(st, axis=0, keepdims=True))
            alphas.append(jnp.exp2(ms[hh] - m_new))
            pp[slot][2 * u + hh] = jnp.exp2(st - m_new).astype(BF16)
            ms[hh] = m_new
        return tuple(ms), tuple(alphas)

    def values(g, slot, alphas, accs):
        accs = list(accs)
        for u, hh in tiles:
            vt = vt_ref[0, g * gu + u, hh * MLA_VROWS:(hh + 1) * MLA_VROWS, :]
            accs[hh] = alphas[2 * u + hh] * accs[hh] + _dot(vt, pp[slot][2 * u + hh])
        return tuple(accs)

    def step(g, slot, carry, more_scores):
        ms, alphas, accs = carry
        accs = values(g - 1, 1 - slot, alphas, accs)
        ms, alphas = softmax(slot, ms)
        if more_scores:
            issue_scores(g + 1, 1 - slot)
        return ms, alphas, accs

    def double_step(i, carry):
        g = 2 * i + 1
        carry = step(g, 1, carry, True)
        return step(g + 1, 0, carry, True)

    m0 = jnp.full((1, tq), -jnp.inf, F32)
    acc0 = jnp.zeros((MLA_VROWS, tq), F32)
    issue_scores(0, 0)
    ms, alphas = softmax(0, (m0, m0))
    issue_scores(1, 1)
    carry = lax.fori_loop(0, (ng - 2) // 2, double_step, (ms, alphas, (acc0, acc0)))
    ms, alphas, accs = step(ng - 1, 1, carry, False)
    accs = values(ng - 1, 1, alphas, accs)
    outs = [(a[:MLA_V] / a[MLA_V:MLA_V + 1]).T for a in accs]
    o_ref[0] = jnp.concatenate(outs, axis=-1).astype(BF16)


def _mla(qm, km, vmt):
    nb, s, _ = km.shape
    nk, tk = vmt.shape[1], vmt.shape[3]
    tq = min(MLA_TQ, s)
    assert (nk // MLA_UNROLL) % 2 == 0 and nk % MLA_UNROLL == 0
    slot_tiles = 2 * MLA_UNROLL
    return pl.pallas_call(
        _mla_kernel,
        out_shape=jax.ShapeDtypeStruct((nb, s, MLA_HEADS * MLA_V), BF16),
        grid=(nb, MLA_HEADS // 2, s // tq),
        in_specs=[pl.BlockSpec((1, 2 * SLOT, tq), lambda b, p, i: (b, p, i)),
                  pl.BlockSpec((1, s, 2 * SLOT), lambda b, p, i: (b, 0, p)),
                  pl.BlockSpec((1, nk, 2 * MLA_VROWS, tk), lambda b, p, i: (b, 0, p, 0))],
        out_specs=pl.BlockSpec((1, tq, 2 * MLA_V), lambda b, p, i: (b, i, p)),
        scratch_shapes=[pltpu.VMEM((slot_tiles, tk, tq), F32)] * 2 + [pltpu.VMEM((slot_tiles, tk, tq), BF16)] * 2,
        compiler_params=_cparams(("arbitrary", "arbitrary", "arbitrary")),
        name="mla",
    )(qm, km, vmt)


def _merge_kernel(of_ref, ob_ref, zg_ref, zga_ref, zgm_ref, om_ref, x_ref, mod_ref, ggla_ref, wglo_ref,
                  wmo_ref, wout_ref, g2_ref, wr_ref, br_ref, ltri_ref,
                  x1_o, h2_o, route_o, cnt_o, cnt_scr):
    first = (pl.program_id(0) == 0) & (pl.program_id(1) == 0)

    @pl.when(first)
    def _():
        cnt_scr[...] = jnp.zeros_like(cnt_scr)

    tm = x_ref.shape[1]
    o = of_ref[0] + ob_ref[0]
    zg = zg_ref[0].astype(F32)
    parts = []
    for hh in range(GLA_HEADS):
        seg = o[:, hh * GLA_DV:(hh + 1) * GLA_DV]
        r = lax.rsqrt(jnp.mean(seg * seg, axis=-1, keepdims=True) + NORM_EPS)
        parts.append((seg * r * ggla_ref[...]) * _silu(zg[:, hh * GLA_DV:(hh + 1) * GLA_DV]))
    og = jnp.concatenate(parts, axis=-1).astype(BF16)
    y_gla = _dot(og, wglo_ref[...])
    y_mla = _dot(om_ref[0], wmo_ref[...])
    merged = _sigmoid(zga_ref[0].astype(F32)) * y_gla + _sigmoid(zgm_ref[0].astype(F32)) * y_mla
    mix = _dot(merged.astype(BF16), wout_ref[...])
    x1 = x_ref[0] + mod_ref[0, 2:3, :] * mix
    x1_o[0] = x1
    ms = jnp.mean(x1 * x1, axis=-1, keepdims=True)
    h2 = (x1 * lax.rsqrt(ms + NORM_EPS) * g2_ref[...]) * (1.0 + mod_ref[0, 4:5, :]) + mod_ref[0, 3:4, :]
    h2_o[0] = h2

    hhi, hlo = _split_bf16(h2)
    w = wr_ref[...]
    whi, wlo = _split_bf16(w)
    logits = _dot(hhi, whi) + _dot(hhi, wlo) + _dot(hlo, whi) + br_ref[...]
    lane = lax.broadcasted_iota(jnp.int32, (tm, LANES), 1)
    neg = -jnp.inf
    is_g = (lane >= N_EXPERTS) & (lane < N_EXPERTS + N_GROUPS)
    lgm = jnp.where(is_g, logits, neg)
    mg = jnp.max(lgm, axis=-1, keepdims=True)
    sg = jnp.sum(jnp.exp(lgm - mg), axis=-1, keepdims=True)
    p_top = 1.0 / sg
    g_idx = jnp.min(jnp.where(lgm == mg, lane, 2 * LANES), axis=-1, keepdims=True) - N_EXPERTS
    in_grp = (lane >= g_idx * EXPERTS_PER_GROUP) & (lane < (g_idx + 1) * EXPERTS_PER_GROUP)
    el = jnp.where(in_grp, logits, neg)
    m1 = jnp.max(el, axis=-1, keepdims=True)
    e1 = jnp.min(jnp.where(el == m1, lane, 2 * LANES), axis=-1, keepdims=True)
    el2 = jnp.where(lane == e1, neg, el)
    m2 = jnp.max(el2, axis=-1, keepdims=True)
    e2 = jnp.min(jnp.where(el2 == m2, lane, 2 * LANES), axis=-1, keepdims=True)
    se = jnp.sum(jnp.exp(el - m1), axis=-1, keepdims=True)
    pv1 = 1.0 / se
    pv2 = jnp.exp(m2 - m1) / se
    w1 = p_top * pv1 / (pv1 + pv2)
    w2 = p_top * pv2 / (pv1 + pv2)

    oh1 = lane == e1
    oh2 = lane == e2
    oh = jnp.where(oh1 | oh2, 1.0, 0.0)
    prefix = _dot(ltri_ref[...], oh.astype(BF16)) + cnt_scr[...]
    r1 = jnp.sum(jnp.where(oh1, prefix, 0.0), axis=-1, keepdims=True)
    r2 = jnp.sum(jnp.where(oh2, prefix, 0.0), axis=-1, keepdims=True)
    cnt_scr[...] = cnt_scr[...] + jnp.sum(oh, axis=0, keepdims=True)
    cnt_o[...] = jnp.broadcast_to(cnt_scr[...], cnt_o.shape)

    route = jnp.where(lane == 0, e1.astype(F32), 0.0)
    route = jnp.where(lane == 1, e2.astype(F32), route)
    route = jnp.where(lane == 2, r1, route)
    route = jnp.where(lane == 3, r2, route)
    route = jnp.where(lane == 4, w1, route)
    route = jnp.where(lane == 5, w2, route)
    route_o[0] = route


def _merge(o_f, o_b, zg, zga, zgm, o_mla, x, mod, wp):
    nb, s, d = x.shape
    tm = min(ROW_TILE, s)
    const = lambda b, i: (0, 0)
    row = lambda b, i: (b, i, 0)

    def full(a):
        return pl.BlockSpec(a.shape, const)

    def rows(n):
        return pl.BlockSpec((1, tm, n), row)

    ltri = jnp.asarray(np.tril(np.ones((tm, tm), np.float32), -1), BF16)
    return pl.pallas_call(
        _merge_kernel,
        out_shape=[jax.ShapeDtypeStruct((nb, s, d), F32), jax.ShapeDtypeStruct((nb, s, d), F32),
                   jax.ShapeDtypeStruct((nb, s, LANES), F32), jax.ShapeDtypeStruct((8, LANES), F32)],
        grid=(nb, s // tm),
        in_specs=[rows(1024), rows(1024), rows(1024), rows(1024), rows(1024), rows(MLA_HEADS * MLA_V),
                  rows(d), pl.BlockSpec((1, 8, d), lambda b, i: (b, 0, 0)),
                  full(wp["g_gla"]), full(wp["w_gla_o"]), full(wp["w_mla_o"]), full(wp["w_out"]),
                  full(wp["g2"]), full(wp["w_route"]), full(wp["b_route"]), full(ltri)],
        out_specs=[rows(d), rows(d), rows(LANES), pl.BlockSpec((8, LANES), const)],
        scratch_shapes=[pltpu.VMEM((1, LANES), F32)],
        compiler_params=_cparams(("arbitrary", "arbitrary")),
        name="merge",
    )(o_f, o_b, zg, zga, zgm, o_mla, x, mod, wp["g_gla"], wp["w_gla_o"], wp["w_mla_o"], wp["w_out"],
      wp["g2"], wp["w_route"], wp["b_route"], ltri)


def _dispatch_kernel(dest_ref, h_ref, xs_hbm, xs_out, sem):
    del xs_hbm
    tt = h_ref.shape[0]

    def copy(r, slot):
        return pltpu.make_async_copy(h_ref.at[pl.ds(r, 1)], xs_out.at[pl.ds(dest_ref[0, 0, 2 * r + slot], 1)],
                                     sem.at[0])

    def start(r, c):
        copy(r, 0).start()
        copy(r, 1).start()
        return c

    def wait(r, c):
        copy(r, 0).wait()
        copy(r, 1).wait()
        return c

    lax.fori_loop(0, tt, start, 0, unroll=8)
    lax.fori_loop(0, tt, wait, 0, unroll=8)


def _dispatch(h2, dest, n_rows):
    t, d = h2.shape
    tt = min(ROW_TILE, t)
    nt = t // tt
    xs0 = jnp.zeros((n_rows, d), F32)
    return pl.pallas_call(
        _dispatch_kernel,
        out_shape=jax.ShapeDtypeStruct((n_rows, d), F32),
        grid=(nt,),
        in_specs=[pl.BlockSpec((1, 1, 2 * tt), lambda i: (i, 0, 0), memory_space=pltpu.SMEM),
                  pl.BlockSpec((tt, d), lambda i: (i, 0)),
                  pl.BlockSpec(memory_space=pl.ANY)],
        out_specs=pl.BlockSpec(memory_space=pl.ANY),
        scratch_shapes=[pltpu.SemaphoreType.DMA((1,))],
        input_output_aliases={2: 0},
        compiler_params=_cparams(("arbitrary",)),
        name="dispatch",
    )(dest.reshape(nt, 1, 2 * tt), h2, xs0)


def _expert_kernel(te_ref, na_ref, xs_ref, wg_ref, wu_ref, wd_ref, ys_ref):
    active = pl.program_id(0) < na_ref[0]

    @pl.when(active)
    def _():
        xb = xs_ref[...].astype(BF16)
        a = _silu(_dot(xb, wg_ref[0])) * _dot(xb, wu_ref[0])
        ys_ref[...] = _dot(a.astype(BF16), wd_ref[0])

    @pl.when(jnp.logical_not(active))
    def _():
        ys_ref[...] = jnp.zeros_like(ys_ref)


def _experts(xs, tile_expert, n_active, wp):
    n_rows, d = xs.shape
    nt = n_rows // MOE_TILE

    def tile(i, te, na):
        return (jnp.minimum(i, na[0] - 1), 0)

    def wmap(i, te, na):
        return (te[jnp.minimum(i, na[0] - 1)], 0, 0)

    return pl.pallas_call(
        _expert_kernel,
        out_shape=jax.ShapeDtypeStruct((n_rows, d), F32),
        grid_spec=pltpu.PrefetchScalarGridSpec(
            num_scalar_prefetch=2,
            grid=(nt,),
            in_specs=[pl.BlockSpec((MOE_TILE, d), tile),
                      pl.BlockSpec((1, d, D_EXPERT), wmap),
                      pl.BlockSpec((1, d, D_EXPERT), wmap),
                      pl.BlockSpec((1, D_EXPERT, d), wmap)],
            out_specs=pl.BlockSpec((MOE_TILE, d), lambda i, te, na: (i, 0))),
        compiler_params=_cparams(("arbitrary",)),
        name="experts",
    )(tile_expert, n_active, xs, wp["w_e_gate"], wp["w_e_up"], wp["w_e_down"])


def _combine_kernel(dest_ref, x1_ref, rw_ref, ga_ref, ys_hbm, o_ref, buf, sem):
    tt = x1_ref.shape[0]

    def copy(r, slot):
        return pltpu.make_async_copy(ys_hbm.at[pl.ds(dest_ref[0, 0, 2 * r + slot], 1)],
                                     buf.at[slot, pl.ds(r, 1)], sem.at[0])

    def start(r, c):
        copy(r, 0).start()
        copy(r, 1).start()
        return c

    def wait(r, c):
        copy(r, 0).wait()
        copy(r, 1).wait()
        return c

    lax.fori_loop(0, tt, start, 0, unroll=8)
    lax.fori_loop(0, tt, wait, 0, unroll=8)
    rw = rw_ref[...]
    moe = buf[0] * rw[:, 4:5] + buf[1] * rw[:, 5:6]
    o_ref[...] = x1_ref[...] + ga_ref[0, 5:6, :] * moe


def _combine(x1, route, mod, ys, dest, s):
    t, d = x1.shape
    tt = min(ROW_TILE, s)
    nt = t // tt
    per_b = s // tt
    return pl.pallas_call(
        _combine_kernel,
        out_shape=jax.ShapeDtypeStruct((t, d), F32),
        grid=(nt,),
        in_specs=[pl.BlockSpec((1, 1, 2 * tt), lambda i: (i, 0, 0), memory_space=pltpu.SMEM),
                  pl.BlockSpec((tt, d), lambda i: (i, 0)),
                  pl.BlockSpec((tt, LANES), lambda i: (i, 0)),
                  pl.BlockSpec((1, 8, d), lambda i: (i // per_b, 0, 0)),
                  pl.BlockSpec(memory_space=pl.ANY)],
        out_specs=pl.BlockSpec((tt, d), lambda i: (i, 0)),
        scratch_shapes=[pltpu.VMEM((2, tt, d), F32), pltpu.SemaphoreType.DMA((1,))],
        compiler_params=_cparams(("arbitrary",)),
        name="combine",
    )(dest.reshape(nt, 1, 2 * tt), x1, route, mod, ys)


def _moe(x1, h2, route, counts, mod, wp):
    nb, s, d = x1.shape
    t = nb * s
    route2 = route.reshape(t, LANES)
    e_flat = route2[:, 0:2].astype(jnp.int32)
    rank = route2[:, 2:4].astype(jnp.int32)
    cnt = counts[0, :N_EXPERTS].astype(jnp.int32)
    padded = (cnt + MOE_TILE - 1) // MOE_TILE * MOE_TILE
    pad_end = jnp.cumsum(padded)
    pad_start = pad_end - padded
    dest = (pad_start[e_flat] + rank).reshape(t * TOP_K)
    n_rows = t * TOP_K + N_EXPERTS * MOE_TILE
    nt = n_rows // MOE_TILE
    tile_row = jnp.arange(nt, dtype=jnp.int32) * MOE_TILE
    tile_expert = jnp.minimum(jnp.sum((pad_end[None, :] <= tile_row[:, None]).astype(jnp.int32), axis=1),
                              N_EXPERTS - 1)
    n_active = (pad_end[-1:] // MOE_TILE).astype(jnp.int32)
    xs = _dispatch(h2.reshape(t, d), dest, n_rows)
    ys = _experts(xs, tile_expert, n_active, wp)
    out = _combine(x1.reshape(t, d), route2, mod, ys, dest, s)
    return out.reshape(nb, s, d)


def _swap_halves(a, axis=-1):
    n = a.shape[axis] // 2
    lo, hi = jnp.split(a, [n], axis=axis)
    return jnp.concatenate([hi, lo], axis=axis)


def _slot(parts, n_lead):
    out = jnp.zeros((n_lead, SLOT), F32)
    for off, a in parts:
        out = out.at[:, off:off + a.shape[-1]].set(a)
    return out


def _prep_weights(s_max, g_norm1, w_in, w_gk_fwd, b_gk_fwd, w_gk_bwd, b_gk_bwd, g_gla_out, w_gla_o, g_q_a,
                  w_q_b, g_kv_a, w_kv_b, g_q_nope, g_k_nope, g_q_rope, g_k_rope, w_mla_o, w_out, g_norm2,
                  w_group, b_group, w_router, b_router, w_e_gate, w_e_up, w_e_down):
    d = D_MODEL
    hk, hv = GLA_HEADS * GLA_DK, GLA_HEADS * GLA_DV
    offs = np.cumsum([0, hk, hk, hv, hv, GLA_RANK, GLA_RANK, MLA_Q_RANK, MLA_KV_RANK, MLA_ROPE, d, d])
    cols = [w_in[:, offs[i]:offs[i + 1]] for i in range(11)]
    wq_, wk_, wv_, wg_, wgf, wgb, wqa, wkva, wkr, wga, wgm = cols
    kr_slot = _slot([(MLA_NOPE, wkr)], d)
    krs_slot = _slot([(MLA_NOPE, _swap_halves(wkr))], d)
    gate_slot = _slot([(0, wgf), (GLA_RANK, wgb)], d)
    w_in_ext = jnp.concatenate([wq_, wk_, wv_, wg_, wga, wgm, wqa, wkva, kr_slot, krs_slot, gate_slot],
                               axis=1).astype(BF16)
    w_gk = jnp.zeros((SLOT, 2 * hk), F32)
    w_gk = w_gk.at[0:GLA_RANK, 0:hk].set(w_gk_fwd).at[GLA_RANK:2 * GLA_RANK, hk:].set(w_gk_bwd).astype(BF16)
    b_gk = jnp.concatenate([b_gk_fwd, b_gk_bwd]).reshape(1, -1)

    wq3 = w_q_b.reshape(MLA_Q_RANK, MLA_HEADS, MLA_NOPE + MLA_ROPE)
    zq = jnp.zeros((MLA_Q_RANK, MLA_HEADS, SLOT - MLA_NOPE - MLA_ROPE), F32)
    main = jnp.concatenate([wq3, zq], axis=-1)
    swp = jnp.concatenate([jnp.zeros((MLA_Q_RANK, MLA_HEADS, MLA_NOPE), F32),
                           _swap_halves(wq3[..., MLA_NOPE:]), zq], axis=-1)
    w_q = jnp.concatenate([main.reshape(MLA_Q_RANK, -1), swp.reshape(MLA_Q_RANK, -1)], axis=1).T.astype(BF16)
    wkv3 = w_kv_b.reshape(MLA_KV_RANK, MLA_HEADS, MLA_NOPE + MLA_V)
    w_k = jnp.concatenate([wkv3[..., :MLA_NOPE], jnp.zeros((MLA_KV_RANK, MLA_HEADS, SLOT - MLA_NOPE), F32)],
                          axis=-1).reshape(MLA_KV_RANK, -1).astype(BF16)
    w_v = jnp.concatenate([wkv3[..., MLA_NOPE:], jnp.zeros((MLA_KV_RANK, MLA_HEADS, MLA_VROWS - MLA_V), F32)],
                          axis=-1).reshape(MLA_KV_RANK, -1).T.astype(BF16)
    v_ones = jnp.zeros((MLA_HEADS, MLA_VROWS), F32).at[:, MLA_V].set(1.0).reshape(-1, 1)

    lanevecs = jnp.concatenate([
        _slot([(0, g_q_nope[None]), (MLA_NOPE, g_q_rope[None])], 1),
        _slot([(MLA_NOPE, _swap_halves(g_q_rope)[None])], 1),
        _slot([(0, g_k_nope[None])], 1),
        _slot([(MLA_NOPE, g_k_rope[None])], 1),
        _slot([(MLA_NOPE, _swap_halves(g_k_rope)[None])], 1),
        jnp.zeros((3, SLOT), F32)], axis=0)
    seg = np.zeros((SLOT, SLOT), np.float32)
    seg[:MLA_NOPE, :MLA_NOPE] = 1.0 / MLA_NOPE
    seg[MLA_NOPE:MLA_NOPE + MLA_ROPE, MLA_NOPE:MLA_NOPE + MLA_ROPE] = 1.0 / MLA_ROPE
    bd = jnp.asarray(seg, BF16)

    half = MLA_ROPE // 2
    inv = ROPE_BASE ** (-jnp.arange(half, dtype=F32) / half)
    ang = jnp.arange(s_max, dtype=F32)[:, None] * inv[None, :]
    cos, sin = jnp.cos(ang), jnp.sin(ang)
    pad = jnp.zeros((s_max, SLOT - MLA_NOPE - MLA_ROPE), F32)
    cos_t = jnp.concatenate([jnp.ones((s_max, MLA_NOPE), F32), cos, cos, pad], axis=1)
    sin_t = jnp.concatenate([jnp.zeros((s_max, MLA_NOPE), F32), -sin, sin, pad], axis=1)

    w_route = jnp.zeros((d, LANES), F32).at[:, :N_EXPERTS].set(w_router)
    w_route = w_route.at[:, N_EXPERTS:N_EXPERTS + N_GROUPS].set(w_group)
    b_route = jnp.zeros((1, LANES), F32).at[0, :N_EXPERTS].set(b_router)
    b_route = b_route.at[0, N_EXPERTS:N_EXPERTS + N_GROUPS].set(b_group)

    return dict(
        g1=g_norm1.reshape(1, d), w_in=w_in_ext, w_gk=w_gk, b_gk=b_gk,
        g_qa=g_q_a.reshape(1, -1), w_q=w_q, g_kva=g_kv_a.reshape(1, -1), w_k=w_k, w_v=w_v,
        v_ones=v_ones, bd=bd, lanevecs=lanevecs, colvecs=lanevecs.T, cos=cos_t, sin=sin_t,
        g_gla=g_gla_out.reshape(1, -1), w_gla_o=w_gla_o.astype(BF16), w_mla_o=w_mla_o.astype(BF16),
        w_out=w_out.astype(BF16), g2=g_norm2.reshape(1, d), w_route=w_route, b_route=b_route,
        w_e_gate=w_e_gate.astype(BF16), w_e_up=w_e_up.astype(BF16), w_e_down=w_e_down.astype(BF16))


def _layer(x, c, w_ada, b_ada, wp):
    mod = _ada(c, w_ada, b_ada)
    q, k, v, zg, zga, zgm, lgf, lgb, qm, km, vm = _inproj(x, mod, wp)
    o_f, o_b = _gla(q, k, v, lgf, lgb)
    o_mla = _mla(qm, km, vm)
    x1, h2, route, counts = _merge(o_f, o_b, zg, zga, zgm, o_mla, x, mod, wp)
    return _moe(x1, h2, route, counts, mod, wp)


def kernel(x_prompt, x_sample, c_prompt, c_sample, w_ada, b_ada, g_norm1, w_in, w_gk_fwd, b_gk_fwd, w_gk_bwd, b_gk_bwd, g_gla_out, w_gla_o, g_q_a, w_q_b, g_kv_a, w_kv_b, g_q_nope, g_k_nope, g_q_rope, g_k_rope, w_mla_o, w_out, g_norm2, w_group, b_group, w_router, b_router, w_e_gate, w_e_up, w_e_down):
    assert w_ada.shape[0] == 1, "single-layer trunk"
    s_max = max(x_prompt.shape[1], x_sample.shape[1])
    wp = _prep_weights(s_max, *[p[0] for p in (
        g_norm1, w_in, w_gk_fwd, b_gk_fwd, w_gk_bwd, b_gk_bwd, g_gla_out, w_gla_o, g_q_a, w_q_b, g_kv_a,
        w_kv_b, g_q_nope, g_k_nope, g_q_rope, g_k_rope, w_mla_o, w_out, g_norm2, w_group, b_group,
        w_router, b_router, w_e_gate, w_e_up, w_e_down)])
    y_prompt = _layer(x_prompt, c_prompt, w_ada[0], b_ada[0], wp)
    y_sample = _layer(x_sample, c_sample, w_ada[0], b_ada[0], wp)
    return (y_prompt, y_sample)
```

```python
import functools
import math

import numpy as np
import jax
import jax.numpy as jnp
from jax import lax
from jax.experimental import pallas as pl
from jax.experimental.pallas import tpu as pltpu

F32 = jnp.float32
BF16 = jnp.bfloat16

D_MODEL = 1024
GLA_HEADS, GLA_DK, GLA_DV, GLA_RANK = 4, 128, 256, 16
GLA_GATE_NORM = 16.0
MLA_HEADS, MLA_Q_RANK, MLA_KV_RANK = 8, 256, 128
MLA_NOPE, MLA_ROPE, MLA_V = 64, 32, 64
ROPE_BASE = 10000.0
N_GROUPS, EXPERTS_PER_GROUP, TOP_K, D_EXPERT = 4, 8, 2, 512
N_EXPERTS = N_GROUPS * EXPERTS_PER_GROUP
NORM_EPS = 1e-6

LANES = 128
SLOT = LANES
VMEM_LIMIT = 56 * 1024 * 1024

ROW_TILE = 256
GLA_CHUNK = 64
GLA_BLOCK = 256
GLA_EXP_CLAMP = 80.0
MLA_TQ = 512
MLA_VROWS = MLA_V + 16
MLA_UNROLL = 2
MOE_TILE = 256
MISC_COLS = MLA_Q_RANK + MLA_KV_RANK + 3 * SLOT


def _cparams(sem):
    return pltpu.CompilerParams(dimension_semantics=sem, vmem_limit_bytes=VMEM_LIMIT)


def _dot(a, b):
    return jnp.dot(a, b, preferred_element_type=F32)


def _dot_nt(a, b):
    return lax.dot_general(a, b, (((1,), (1,)), ((), ())), preferred_element_type=F32)


def _split_bf16(x):
    hi = x.astype(BF16)
    lo = (x - hi.astype(F32)).astype(BF16)
    return hi, lo


def _dot_split_lhs(x, w):
    hi, lo = _split_bf16(x)
    return _dot(hi, w) + _dot(lo, w)


def _sigmoid(x):
    return 1.0 / (1.0 + jnp.exp(-x))


def _silu(x):
    return x * _sigmoid(x)


def _ada_kernel(c_ref, w_ref, b_ref, o_ref):
    a = _silu(c_ref[...])
    ahi, alo = _split_bf16(a)
    w = w_ref[...]
    whi, wlo = _split_bf16(w)
    o_ref[...] = _dot(ahi, whi) + _dot(ahi, wlo) + _dot(alo, whi) + b_ref[...]


def _ada(c, w_ada, b_ada):
    nb = c.shape[0]
    cp = jnp.zeros((8, D_MODEL), F32).at[:nb].set(c)
    tn = 1536
    out = pl.pallas_call(
        _ada_kernel,
        out_shape=jax.ShapeDtypeStruct((8, 6 * D_MODEL), F32),
        grid=(6 * D_MODEL // tn,),
        in_specs=[pl.BlockSpec((8, D_MODEL), lambda j: (0, 0)),
                  pl.BlockSpec((D_MODEL, tn), lambda j: (0, j)),
                  pl.BlockSpec((1, tn), lambda j: (0, j))],
        out_specs=pl.BlockSpec((8, tn), lambda j: (0, j)),
        compiler_params=_cparams(("arbitrary",)),
        name="ada",
    )(cp, w_ada, b_ada.reshape(1, -1))
    mod = out[:nb].reshape(nb, 6, D_MODEL)
    return jnp.concatenate([mod, jnp.zeros((nb, 2, D_MODEL), F32)], axis=1)


_C_Q, _C_K, _C_V, _C_G, _C_GA, _C_GM = 0, 512, 1024, 2048, 3072, 4096
_C_MISC = 5120
IN_EXT = _C_MISC + MISC_COLS


def _segmean(xsq, bd):
    return _dot_split_lhs(xsq, bd)


def _inproj_kernel(x_ref, mod_ref, g1_ref, win_ref, wgk_ref, bgk_ref, gqa_ref, wq_ref, gkva_ref,
                   wk_ref, wv_ref, vone_ref, bd_ref, lv_ref, cv_ref, cos_ref, sin_ref, cosc_ref, sinc_ref,
                   q_o, k_o, v_o, zg_o, zga_o, zgm_o, lgf_o, lgb_o, qm_o, km_o, vm_o, h_scr):
    x = x_ref[0]
    ms = jnp.mean(x * x, axis=-1, keepdims=True)
    h = (x * lax.rsqrt(ms + NORM_EPS) * g1_ref[...]) * (1.0 + mod_ref[0, 1:2, :]) + mod_ref[0, 0:1, :]
    h_scr[...] = h.astype(BF16)

    def proj(c0, n):
        return _dot(h_scr[...], win_ref[:, c0:c0 + n])

    q_o[0] = (proj(_C_Q, 512) * (GLA_DK ** -0.5)).astype(BF16)
    k_o[0] = proj(_C_K, 512).astype(BF16)
    v_o[0] = proj(_C_V, 1024).astype(BF16)
    zg_o[0] = proj(_C_G, 1024).astype(BF16)
    zga_o[0] = proj(_C_GA, 1024).astype(BF16)
    zgm_o[0] = proj(_C_GM, 1024).astype(BF16)
    misc = proj(_C_MISC, MISC_COLS)

    bd = bd_ref[...]
    cos_t = cos_ref[...]
    sin_t = sin_ref[...]
    g_kn, g_kr, g_krs = lv_ref[2:3, :], lv_ref[3:4, :], lv_ref[4:5, :]
    scale = (MLA_NOPE + MLA_ROPE) ** -0.5 * math.log2(math.e)

    zqa = misc[:, 0:MLA_Q_RANK]
    qa = zqa * lax.rsqrt(jnp.mean(zqa * zqa, axis=-1, keepdims=True) + NORM_EPS) * gqa_ref[...]
    qq = _dot_nt(wq_ref[...], qa.astype(BF16))
    cos_c, sin_c = cosc_ref[...], sinc_ref[...]
    gc_q, gc_qs = cv_ref[:, 0:1], cv_ref[:, 1:2]
    for hh in range(MLA_HEADS):
        xq = qq[hh * SLOT:(hh + 1) * SLOT, :]
        xs = qq[(MLA_HEADS + hh) * SLOT:(MLA_HEADS + hh + 1) * SLOT, :]
        hi, lo = _split_bf16(xq * xq)
        r = lax.rsqrt(_dot(bd, hi) + _dot(bd, lo) + NORM_EPS)
        qm = (xq * gc_q * cos_c + xs * gc_qs * sin_c) * (r * scale)
        qm_o[0, hh * SLOT:(hh + 1) * SLOT, :] = qm.astype(BF16)

    c0 = MLA_Q_RANK
    zkva = misc[:, c0:c0 + MLA_KV_RANK]
    kva = zkva * lax.rsqrt(jnp.mean(zkva * zkva, axis=-1, keepdims=True) + NORM_EPS) * gkva_ref[...]
    kvab = kva.astype(BF16)
    kk = _dot(kvab, wk_ref[...])
    vm_o[0, 0] = (_dot_nt(wv_ref[...], kvab) + vone_ref[...]).astype(BF16)
    c0 += MLA_KV_RANK
    kr = misc[:, c0:c0 + SLOT]
    krs = misc[:, c0 + SLOT:c0 + 2 * SLOT]
    r_kr = lax.rsqrt(_segmean(kr * kr, bd) + NORM_EPS)
    kpe = (kr * g_kr * cos_t + krs * g_krs * sin_t) * r_kr
    for hh in range(MLA_HEADS):
        xk = kk[:, hh * SLOT:(hh + 1) * SLOT]
        r = lax.rsqrt(_segmean(xk * xk, bd) + NORM_EPS)
        km_o[0, :, hh * SLOT:(hh + 1) * SLOT] = (xk * r * g_kn + kpe).astype(BF16)

    c0 += 2 * SLOT
    zgate = misc[:, c0:c0 + SLOT].astype(BF16)
    pre = _dot(zgate, wgk_ref[...]) + bgk_ref[...]
    lg = (jnp.minimum(pre, 0.0) - jnp.log(1.0 + jnp.exp(-jnp.abs(pre)))) * (1.0 / GLA_GATE_NORM)
    lgf_o[0] = lg[:, :GLA_HEADS * GLA_DK]
    lgb_o[0] = lg[:, GLA_HEADS * GLA_DK:]


def _inproj(x, mod, wp):
    nb, s, d = x.shape
    tm = min(ROW_TILE, s)
    const = lambda b, i: (0, 0)
    row = lambda b, i: (b, i, 0)

    def full(a):
        return pl.BlockSpec(a.shape, const)

    def out(n, dt):
        return jax.ShapeDtypeStruct((nb, s, n), dt), pl.BlockSpec((1, tm, n), row)

    outs = [out(512, BF16), out(512, BF16), out(1024, BF16), out(1024, BF16), out(1024, BF16),
            out(1024, BF16), out(512, F32), out(512, F32),
            (jax.ShapeDtypeStruct((nb, MLA_HEADS * SLOT, s), BF16),
             pl.BlockSpec((1, MLA_HEADS * SLOT, tm), lambda b, i: (b, 0, i))),
            out(MLA_HEADS * SLOT, BF16),
            (jax.ShapeDtypeStruct((nb, s // tm, MLA_HEADS * MLA_VROWS, tm), BF16),
             pl.BlockSpec((1, 1, MLA_HEADS * MLA_VROWS, tm), lambda b, i: (b, i, 0, 0)))]
    return pl.pallas_call(
        _inproj_kernel,
        out_shape=[o[0] for o in outs],
        grid=(nb, s // tm),
        in_specs=[pl.BlockSpec((1, tm, d), row),
                  pl.BlockSpec((1, 8, d), lambda b, i: (b, 0, 0)),
                  full(wp["g1"]), full(wp["w_in"]), full(wp["w_gk"]), full(wp["b_gk"]),
                  full(wp["g_qa"]), full(wp["w_q"]), full(wp["g_kva"]), full(wp["w_k"]), full(wp["w_v"]),
                  full(wp["v_ones"]), full(wp["bd"]), full(wp["lanevecs"]), full(wp["colvecs"]),
                  pl.BlockSpec((tm, SLOT), lambda b, i: (i, 0)),
                  pl.BlockSpec((tm, SLOT), lambda b, i: (i, 0)),
                  pl.BlockSpec((SLOT, tm), lambda b, i: (0, i)),
                  pl.BlockSpec((SLOT, tm), lambda b, i: (0, i))],
        out_specs=[o[1] for o in outs],
        scratch_shapes=[pltpu.VMEM((tm, d), BF16)],
        compiler_params=_cparams(("arbitrary", "arbitrary")),
        name="inproj",
    )(x, mod, wp["g1"], wp["w_in"], wp["w_gk"], wp["b_gk"], wp["g_qa"], wp["w_q"], wp["g_kva"],
      wp["w_k"], wp["w_v"], wp["v_ones"], wp["bd"], wp["lanevecs"], wp["colvecs"],
      wp["cos"][:s], wp["sin"][:s], wp["cos"][:s].T, wp["sin"][:s].T)


def _dot_tn(a, b):
    return lax.dot_general(a, b, (((0,), (0,)), ((), ())), preferred_element_type=F32)


def _gla_direction(q_ref, k_ref, v_ref, g_ref, o_ref, st_ref, scr, cum, mask, reverse):
    qin_s, qmid_s, kmid_s, kout_s, tot_s, u_s = scr
    c = GLA_CHUNK
    nch = q_ref.shape[1] // c
    rows = [slice(j * c, (j + 1) * c) for j in range(nch)]
    ks = [slice(h * GLA_DK, (h + 1) * GLA_DK) for h in range(GLA_HEADS)]
    vs = [slice(h * GLA_DV, (h + 1) * GLA_DV) for h in range(GLA_HEADS)]

    for j in range(nch):
        hi, lo = _split_bf16(g_ref[0, rows[j], :])
        b = _dot(cum, hi) + _dot(cum, lo)
        mid = b[c // 2:c // 2 + 1, :]
        tot = b[0:1, :] if reverse else b[c - 1:c, :]
        q = q_ref[0, rows[j], :].astype(F32)
        k = k_ref[0, rows[j], :].astype(F32)
        qin_s[rows[j], :] = (q * jnp.exp(b)).astype(BF16)
        qmid_s[rows[j], :] = (q * jnp.exp(jnp.minimum(b - mid, GLA_EXP_CLAMP))).astype(BF16)
        kmid_s[rows[j], :] = (k * jnp.exp(jnp.minimum(mid - b, GLA_EXP_CLAMP))).astype(BF16)
        kout_s[rows[j], :] = (k * jnp.exp(tot - b)).astype(BF16)
        tot_s[j:j + 1, :] = tot

    attn = {}
    for j in range(nch):
        for h in range(GLA_HEADS):
            a = _dot_nt(qmid_s[rows[j], ks[h]], kmid_s[rows[j], ks[h]])
            attn[j, h] = jnp.where(mask, a, 0.0).astype(BF16)

    for j in range(nch):
        for h in range(GLA_HEADS):
            v = v_ref[0, rows[j], vs[h]]
            o_ref[0, rows[j], vs[h]] = _dot(attn[j, h], v)
            u_s[j * GLA_HEADS + h] = _dot_tn(kout_s[rows[j], ks[h]], v)

    for j in (range(nch - 1, -1, -1) if reverse else range(nch)):
        for h in range(GLA_HEADS):
            st = st_ref[h]
            o_ref[0, rows[j], vs[h]] += _dot(qin_s[rows[j], ks[h]], st.astype(BF16))
            tot = jnp.broadcast_to(tot_s[j:j + 1, ks[h]], (GLA_DK, GLA_DK))
            dec = jnp.exp(tot.T)
            st_ref[h] = jnp.concatenate([dec] * (GLA_DV // GLA_DK), axis=1) * st + u_s[j * GLA_HEADS + h]


def _gla_kernel(qf, kf, vf, gf, qb, kb, vb, gb, of_ref, ob_ref, sf_ref, sb_ref, *scr):
    @pl.when(pl.program_id(1) == 0)
    def _():
        sf_ref[...] = jnp.zeros_like(sf_ref)
        sb_ref[...] = jnp.zeros_like(sb_ref)

    c = GLA_CHUNK
    ri = lax.broadcasted_iota(jnp.int32, (c, c), 0)
    ci = lax.broadcasted_iota(jnp.int32, (c, c), 1)
    cum_f = jnp.where(ri >= ci, 1.0, 0.0).astype(BF16)
    cum_b = jnp.where(ci >= ri, 1.0, 0.0).astype(BF16)
    n = len(scr) // 2
    _gla_direction(qf, kf, vf, gf, of_ref, sf_ref, scr[:n], cum_f, ri >= ci, False)
    _gla_direction(qb, kb, vb, gb, ob_ref, sb_ref, scr[n:], cum_b, ci > ri, True)


def _gla(q, k, v, lgf, lgb):
    nb, s, _ = q.shape
    cb = min(GLA_BLOCK, s)
    ns = s // cb
    nch = cb // GLA_CHUNK
    fwd = lambda b, i: (b, i, 0)
    bwd = lambda b, i: (b, ns - 1 - i, 0)
    hk, hv = GLA_HEADS * GLA_DK, GLA_HEADS * GLA_DV

    def specs(im):
        return [pl.BlockSpec((1, cb, hk), im), pl.BlockSpec((1, cb, hk), im),
                pl.BlockSpec((1, cb, hv), im), pl.BlockSpec((1, cb, hk), im)]

    per_dir = [pltpu.VMEM((cb, hk), BF16)] * 4 + [pltpu.VMEM((8, hk), F32),
                                                  pltpu.VMEM((nch * GLA_HEADS, GLA_DK, GLA_DV), F32)]
    return pl.pallas_call(
        _gla_kernel,
        out_shape=[jax.ShapeDtypeStruct((nb, s, hv), F32)] * 2,
        grid=(nb, ns),
        in_specs=specs(fwd) + specs(bwd),
        out_specs=[pl.BlockSpec((1, cb, hv), fwd), pl.BlockSpec((1, cb, hv), bwd)],
        scratch_shapes=[pltpu.VMEM((GLA_HEADS, GLA_DK, GLA_DV), F32)] * 2 + per_dir * 2,
        compiler_params=_cparams(("arbitrary", "arbitrary")),
        name="gla",
    )(q, k, v, lgf, q, k, v, lgb)


def _mla_kernel(q_ref, k_ref, vt_ref, o_ref, sc0, sc1, p0, p1):
    tq = q_ref.shape[2]
    nk = vt_ref.shape[1]
    tk = vt_ref.shape[3]
    gu = MLA_UNROLL
    ng = nk // gu
    sc, pp = (sc0, sc1), (p0, p1)
    qs = [q_ref[0, hh * SLOT:(hh + 1) * SLOT, :] for hh in range(2)]
    tiles = [(u, hh) for u in range(gu) for hh in range(2)]

    def issue_scores(g, slot):
        tmax = []
        for u, hh in tiles:
            r0 = pl.multiple_of((g * gu + u) * tk, tk)
            kt = k_ref[0, pl.ds(r0, tk), hh * SLOT:(hh + 1) * SLOT]
            st = _dot(kt, qs[hh])
            sc[slot][2 * u + hh] = st.astype(BF16)
            tmax.append(jnp.max(st, axis=0, keepdims=True).astype(BF16).astype(F32))
        return tuple(tmax)

    def softmax(slot, ms, tmax):
        ms = list(ms)
        alphas = []
        for u, hh in tiles:
            m_new = jnp.maximum(ms[hh], tmax[2 * u + hh])
            alphas.append(jnp.exp2(ms[hh] - m_new))
            pp[slot][2 * u + hh] = jnp.exp2(sc[slot][2 * u + hh] - m_new.astype(BF16))
            ms[hh] = m_new
        return tuple(ms), tuple(alphas)

    def values(g, slot, alphas, accs):
        accs = list(accs)
        for u, hh in tiles:
            vt = vt_ref[0, g * gu + u, hh * MLA_VROWS:(hh + 1) * MLA_VROWS, :]
            accs[hh] = alphas[2 * u + hh] * accs[hh] + _dot(vt, pp[slot][2 * u + hh])
        return tuple(accs)

    def step(g, slot, carry, more_scores):
        ms, alphas, accs, tmax = carry
        accs = values(g - 1, 1 - slot, alphas, accs)
        ms, alphas = softmax(slot, ms, tmax)
        if more_scores:
            tmax = issue_scores(g + 1, 1 - slot)
        return ms, alphas, accs, tmax

    def double_step(i, carry):
        g = 2 * i + 1
        carry = step(g, 1, carry, True)
        return step(g + 1, 0, carry, True)

    m0 = jnp.full((1, tq), -jnp.inf, F32)
    acc0 = jnp.zeros((MLA_VROWS, tq), F32)
    tmax = issue_scores(0, 0)
    ms, alphas = softmax(0, (m0, m0), tmax)
    tmax = issue_scores(1, 1)
    carry = (ms, alphas, (acc0, acc0), tmax)
    for i in range((ng - 2) // 2):
        carry = double_step(i, carry)
    ms, alphas, accs, _ = step(ng - 1, 1, carry, False)
    accs = values(ng - 1, 1, alphas, accs)
    outs = [(a[:MLA_V] / a[MLA_V:MLA_V + 1]).T for a in accs]
    o_ref[0] = jnp.concatenate(outs, axis=-1).astype(BF16)


def _mla(qm, km, vmt):
    nb, s, _ = km.shape
    nk, tk = vmt.shape[1], vmt.shape[3]
    tq = min(MLA_TQ, s)
    assert (nk // MLA_UNROLL) % 2 == 0 and nk % MLA_UNROLL == 0
    slot_tiles = 2 * MLA_UNROLL
    return pl.pallas_call(
        _mla_kernel,
        out_shape=jax.ShapeDtypeStruct((nb, s, MLA_HEADS * MLA_V), BF16),
        grid=(nb, MLA_HEADS // 2, s // tq),
        in_specs=[pl.BlockSpec((1, 2 * SLOT, tq), lambda b, p, i: (b, p, i)),
                  pl.BlockSpec((1, s, 2 * SLOT), lambda b, p, i: (b, 0, p)),
                  pl.BlockSpec((1, nk, 2 * MLA_VROWS, tk), lambda b, p, i: (b, 0, p, 0))],
        out_specs=pl.BlockSpec((1, tq, 2 * MLA_V), lambda b, p, i: (b, i, p)),
        scratch_shapes=[pltpu.VMEM((slot_tiles, tk, tq), BF16)] * 4,
        compiler_params=_cparams(("arbitrary", "arbitrary", "arbitrary")),
        name="mla",
    )(qm, km, vmt)


def _merge_kernel(of_ref, ob_ref, zg_ref, zga_ref, zgm_ref, om_ref, x_ref, mod_ref, ggla_ref, wglo_ref,
                  wmo_ref, wout_ref, g2_ref, wr_ref, br_ref, ltri_ref,
                  x1_o, h2_o, route_o, cnt_o, cnt_scr):
    first = (pl.program_id(0) == 0) & (pl.program_id(1) == 0)

    @pl.when(first)
    def _():
        cnt_scr[...] = jnp.zeros_like(cnt_scr)

    tm = x_ref.shape[1]
    o = of_ref[0] + ob_ref[0]
    zg = zg_ref[0].astype(F32)
    parts = []
    for hh in range(GLA_HEADS):
        seg = o[:, hh * GLA_DV:(hh + 1) * GLA_DV]
        r = lax.rsqrt(jnp.mean(seg * seg, axis=-1, keepdims=True) + NORM_EPS)
        parts.append((seg * r * ggla_ref[...]) * _silu(zg[:, hh * GLA_DV:(hh + 1) * GLA_DV]))
    og = jnp.concatenate(parts, axis=-1).astype(BF16)
    y_gla = _dot(og, wglo_ref[...])
    y_mla = _dot(om_ref[0], wmo_ref[...])
    merged = _sigmoid(zga_ref[0].astype(F32)) * y_gla + _sigmoid(zgm_ref[0].astype(F32)) * y_mla
    mix = _dot(merged.astype(BF16), wout_ref[...])
    x1 = x_ref[0] + mod_ref[0, 2:3, :] * mix
    x1_o[0] = x1
    ms = jnp.mean(x1 * x1, axis=-1, keepdims=True)
    h2 = (x1 * lax.rsqrt(ms + NORM_EPS) * g2_ref[...]) * (1.0 + mod_ref[0, 4:5, :]) + mod_ref[0, 3:4, :]
    h2_o[0] = h2

    hhi, hlo = _split_bf16(h2)
    w = wr_ref[...]
    whi, wlo = _split_bf16(w)
    logits = _dot(hhi, whi) + _dot(hhi, wlo) + _dot(hlo, whi) + br_ref[...]
    lane = lax.broadcasted_iota(jnp.int32, (tm, LANES), 1)
    neg = -jnp.inf
    is_g = (lane >= N_EXPERTS) & (lane < N_EXPERTS + N_GROUPS)
    lgm = jnp.where(is_g, logits, neg)
    mg = jnp.max(lgm, axis=-1, keepdims=True)
    sg = jnp.sum(jnp.exp(lgm - mg), axis=-1, keepdims=True)
    p_top = 1.0 / sg
    g_idx = jnp.min(jnp.where(lgm == mg, lane, 2 * LANES), axis=-1, keepdims=True) - N_EXPERTS
    in_grp = (lane >= g_idx * EXPERTS_PER_GROUP) & (lane < (g_idx + 1) * EXPERTS_PER_GROUP)
    el = jnp.where(in_grp, logits, neg)
    m1 = jnp.max(el, axis=-1, keepdims=True)
    e1 = jnp.min(jnp.where(el == m1, lane, 2 * LANES), axis=-1, keepdims=True)
    el2 = jnp.where(lane == e1, neg, el)
    m2 = jnp.max(el2, axis=-1, keepdims=True)
    e2 = jnp.min(jnp.where(el2 == m2, lane, 2 * LANES), axis=-1, keepdims=True)
    se = jnp.sum(jnp.exp(el - m1), axis=-1, keepdims=True)
    pv1 = 1.0 / se
    pv2 = jnp.exp(m2 - m1) / se
    w1 = p_top * pv1 / (pv1 + pv2)
    w2 = p_top * pv2 / (pv1 + pv2)

    oh1 = lane == e1
    oh2 = lane == e2
    oh = jnp.where(oh1 | oh2, 1.0, 0.0)
    prefix = _dot(ltri_ref[...], oh.astype(BF16)) + cnt_scr[...]
    r1 = jnp.sum(jnp.where(oh1, prefix, 0.0), axis=-1, keepdims=True)
    r2 = jnp.sum(jnp.where(oh2, prefix, 0.0), axis=-1, keepdims=True)
    cnt_scr[...] = cnt_scr[...] + jnp.sum(oh, axis=0, keepdims=True)
    cnt_o[...] = jnp.broadcast_to(cnt_scr[...], cnt_o.shape)

    route = jnp.where(lane == 0, e1.astype(F32), 0.0)
    route = jnp.where(lane == 1, e2.astype(F32), route)
    route = jnp.where(lane == 2, r1, route)
    route = jnp.where(lane == 3, r2, route)
    route = jnp.where(lane == 4, w1, route)
    route = jnp.where(lane == 5, w2, route)
    route_o[0] = route


def _merge(o_f, o_b, zg, zga, zgm, o_mla, x, mod, wp):
    nb, s, d = x.shape
    tm = min(ROW_TILE, s)
    const = lambda b, i: (0, 0)
    row = lambda b, i: (b, i, 0)

    def full(a):
        return pl.BlockSpec(a.shape, const)

    def rows(n):
        return pl.BlockSpec((1, tm, n), row)

    ltri = jnp.asarray(np.tril(np.ones((tm, tm), np.float32), -1), BF16)
    return pl.pallas_call(
        _merge_kernel,
        out_shape=[jax.ShapeDtypeStruct((nb, s, d), F32), jax.ShapeDtypeStruct((nb, s, d), F32),
                   jax.ShapeDtypeStruct((nb, s, LANES), F32), jax.ShapeDtypeStruct((8, LANES), F32)],
        grid=(nb, s // tm),
        in_specs=[rows(1024), rows(1024), rows(1024), rows(1024), rows(1024), rows(MLA_HEADS * MLA_V),
                  rows(d), pl.BlockSpec((1, 8, d), lambda b, i: (b, 0, 0)),
                  full(wp["g_gla"]), full(wp["w_gla_o"]), full(wp["w_mla_o"]), full(wp["w_out"]),
                  full(wp["g2"]), full(wp["w_route"]), full(wp["b_route"]), full(ltri)],
        out_specs=[rows(d), rows(d), rows(LANES), pl.BlockSpec((8, LANES), const)],
        scratch_shapes=[pltpu.VMEM((1, LANES), F32)],
        compiler_params=_cparams(("arbitrary", "arbitrary")),
        name="merge",
    )(o_f, o_b, zg, zga, zgm, o_mla, x, mod, wp["g_gla"], wp["w_gla_o"], wp["w_mla_o"], wp["w_out"],
      wp["g2"], wp["w_route"], wp["b_route"], ltri)


def _dispatch_kernel(dest_ref, h_ref, xs_hbm, xs_out, sem):
    del xs_hbm
    tt = h_ref.shape[0]

    def copy(r, slot):
        return pltpu.make_async_copy(h_ref.at[pl.ds(r, 1)], xs_out.at[pl.ds(dest_ref[0, 0, 2 * r + slot], 1)],
                                     sem.at[0])

    def start(r, c):
        copy(r, 0).start()
        copy(r, 1).start()
        return c

    def wait(r, c):
        copy(r, 0).wait()
        copy(r, 1).wait()
        return c

    lax.fori_loop(0, tt, start, 0, unroll=8)
    lax.fori_loop(0, tt, wait, 0, unroll=8)


def _dispatch(h2, dest, n_rows):
    t, d = h2.shape
    tt = min(ROW_TILE, t)
    nt = t // tt
    xs0 = jnp.zeros((n_rows, d), F32)
    return pl.pallas_call(
        _dispatch_kernel,
        out_shape=jax.ShapeDtypeStruct((n_rows, d), F32),
        grid=(nt,),
        in_specs=[pl.BlockSpec((1, 1, 2 * tt), lambda i: (i, 0, 0), memory_space=pltpu.SMEM),
                  pl.BlockSpec((tt, d), lambda i: (i, 0)),
                  pl.BlockSpec(memory_space=pl.ANY)],
        out_specs=pl.BlockSpec(memory_space=pl.ANY),
        scratch_shapes=[pltpu.SemaphoreType.DMA((1,))],
        input_output_aliases={2: 0},
        compiler_params=_cparams(("arbitrary",)),
        name="dispatch",
    )(dest.reshape(nt, 1, 2 * tt), h2, xs0)


def _expert_kernel(te_ref, na_ref, xs_ref, wg_ref, wu_ref, wd_ref, ys_ref):
    active = pl.program_id(0) < na_ref[0]

    @pl.when(active)
    def _():
        xb = xs_ref[...].astype(BF16)
        a = _silu(_dot(xb, wg_ref[0])) * _dot(xb, wu_ref[0])
        ys_ref[...] = _dot(a.astype(BF16), wd_ref[0])

    @pl.when(jnp.logical_not(active))
    def _():
        ys_ref[...] = jnp.zeros_like(ys_ref)


def _experts(xs, tile_expert, n_active, wp):
    n_rows, d = xs.shape
    nt = n_rows // MOE_TILE

    def tile(i, te, na):
        return (jnp.minimum(i, na[0] - 1), 0)

    def wmap(i, te, na):
        return (te[jnp.minimum(i, na[0] - 1)], 0, 0)

    return pl.pallas_call(
        _expert_kernel,
        out_shape=jax.ShapeDtypeStruct((n_rows, d), F32),
        grid_spec=pltpu.PrefetchScalarGridSpec(
            num_scalar_prefetch=2,
            grid=(nt,),
            in_specs=[pl.BlockSpec((MOE_TILE, d), tile),
                      pl.BlockSpec((1, d, D_EXPERT), wmap),
                      pl.BlockSpec((1, d, D_EXPERT), wmap),
                      pl.BlockSpec((1, D_EXPERT, d), wmap)],
            out_specs=pl.BlockSpec((MOE_TILE, d), lambda i, te, na: (i, 0))),
        compiler_params=_cparams(("arbitrary",)),
        name="experts",
    )(tile_expert, n_active, xs, wp["w_e_gate"], wp["w_e_up"], wp["w_e_down"])


def _combine_kernel(dest_ref, x1_ref, rw_ref, ga_ref, ys_hbm, o_ref, buf, sem):
    tt = x1_ref.shape[0]

    def copy(r, slot):
        return pltpu.make_async_copy(ys_hbm.at[pl.ds(dest_ref[0, 0, 2 * r + slot], 1)],
                                     buf.at[slot, pl.ds(r, 1)], sem.at[0])

    def start(r, c):
        copy(r, 0).start()
        copy(r, 1).start()
        return c

    def wait(r, c):
        copy(r, 0).wait()
        copy(r, 1).wait()
        return c

    lax.fori_loop(0, tt, start, 0, unroll=8)
    lax.fori_loop(0, tt, wait, 0, unroll=8)
    rw = rw_ref[...]
    moe = buf[0] * rw[:, 4:5] + buf[1] * rw[:, 5:6]
    o_ref[...] = x1_ref[...] + ga_ref[0, 5:6, :] * moe


def _combine(x1, route, mod, ys, dest, s):
    t, d = x1.shape
    tt = min(ROW_TILE, s)
    nt = t // tt
    per_b = s // tt
    return pl.pallas_call(
        _combine_kernel,
        out_shape=jax.ShapeDtypeStruct((t, d), F32),
        grid=(nt,),
        in_specs=[pl.BlockSpec((1, 1, 2 * tt), lambda i: (i, 0, 0), memory_space=pltpu.SMEM),
                  pl.BlockSpec((tt, d), lambda i: (i, 0)),
                  pl.BlockSpec((tt, LANES), lambda i: (i, 0)),
                  pl.BlockSpec((1, 8, d), lambda i: (i // per_b, 0, 0)),
                  pl.BlockSpec(memory_space=pl.ANY)],
        out_specs=pl.BlockSpec((tt, d), lambda i: (i, 0)),
        scratch_shapes=[pltpu.VMEM((2, tt, d), F32), pltpu.SemaphoreType.DMA((1,))],
        compiler_params=_cparams(("arbitrary",)),
        name="combine",
    )(dest.reshape(nt, 1, 2 * tt), x1, route, mod, ys)


def _moe(x1, h2, route, counts, mod, wp):
    nb, s, d = x1.shape
    t = nb * s
    route2 = route.reshape(t, LANES)
    e_flat = route2[:, 0:2].astype(jnp.int32)
    rank = route2[:, 2:4].astype(jnp.int32)
    cnt = counts[0, :N_EXPERTS].astype(jnp.int32)
    padded = (cnt + MOE_TILE - 1) // MOE_TILE * MOE_TILE
    pad_end = jnp.cumsum(padded)
    pad_start = pad_end - padded
    dest = (pad_start[e_flat] + rank).reshape(t * TOP_K)
    n_rows = t * TOP_K + N_EXPERTS * MOE_TILE
    nt = n_rows // MOE_TILE
    tile_row = jnp.arange(nt, dtype=jnp.int32) * MOE_TILE
    tile_expert = jnp.minimum(jnp.sum((pad_end[None, :] <= tile_row[:, None]).astype(jnp.int32), axis=1),
                              N_EXPERTS - 1)
    n_active = (pad_end[-1:] // MOE_TILE).astype(jnp.int32)
    xs = _dispatch(h2.reshape(t, d), dest, n_rows)
    ys = _experts(xs, tile_expert, n_active, wp)
    out = _combine(x1.reshape(t, d), route2, mod, ys, dest, s)
    return out.reshape(nb, s, d)


def _swap_halves(a, axis=-1):
    n = a.shape[axis] // 2
    lo, hi = jnp.split(a, [n], axis=axis)
    return jnp.concatenate([hi, lo], axis=axis)


def _slot(parts, n_lead):
    out = jnp.zeros((n_lead, SLOT), F32)
    for off, a in parts:
        out = out.at[:, off:off + a.shape[-1]].set(a)
    return out


def _prep_weights(s_max, g_norm1, w_in, w_gk_fwd, b_gk_fwd, w_gk_bwd, b_gk_bwd, g_gla_out, w_gla_o, g_q_a,
                  w_q_b, g_kv_a, w_kv_b, g_q_nope, g_k_nope, g_q_rope, g_k_rope, w_mla_o, w_out, g_norm2,
                  w_group, b_group, w_router, b_router, w_e_gate, w_e_up, w_e_down):
    d = D_MODEL
    hk, hv = GLA_HEADS * GLA_DK, GLA_HEADS * GLA_DV
    offs = np.cumsum([0, hk, hk, hv, hv, GLA_RANK, GLA_RANK, MLA_Q_RANK, MLA_KV_RANK, MLA_ROPE, d, d])
    cols = [w_in[:, offs[i]:offs[i + 1]] for i in range(11)]
    wq_, wk_, wv_, wg_, wgf, wgb, wqa, wkva, wkr, wga, wgm = cols
    kr_slot = _slot([(MLA_NOPE, wkr)], d)
    krs_slot = _slot([(MLA_NOPE, _swap_halves(wkr))], d)
    gate_slot = _slot([(0, wgf), (GLA_RANK, wgb)], d)
    w_in_ext = jnp.concatenate([wq_, wk_, wv_, wg_, wga, wgm, wqa, wkva, kr_slot, krs_slot, gate_slot],
                               axis=1).astype(BF16)
    w_gk = jnp.zeros((SLOT, 2 * hk), F32)
    w_gk = w_gk.at[0:GLA_RANK, 0:hk].set(w_gk_fwd).at[GLA_RANK:2 * GLA_RANK, hk:].set(w_gk_bwd).astype(BF16)
    b_gk = jnp.concatenate([b_gk_fwd, b_gk_bwd]).reshape(1, -1)

    wq3 = w_q_b.reshape(MLA_Q_RANK, MLA_HEADS, MLA_NOPE + MLA_ROPE)
    zq = jnp.zeros((MLA_Q_RANK, MLA_HEADS, SLOT - MLA_NOPE - MLA_ROPE), F32)
    main = jnp.concatenate([wq3, zq], axis=-1)
    swp = jnp.concatenate([jnp.zeros((MLA_Q_RANK, MLA_HEADS, MLA_NOPE), F32),
                           _swap_halves(wq3[..., MLA_NOPE:]), zq], axis=-1)
    w_q = jnp.concatenate([main.reshape(MLA_Q_RANK, -1), swp.reshape(MLA_Q_RANK, -1)], axis=1).T.astype(BF16)
    wkv3 = w_kv_b.reshape(MLA_KV_RANK, MLA_HEADS, MLA_NOPE + MLA_V)
    w_k = jnp.concatenate([wkv3[..., :MLA_NOPE], jnp.zeros((MLA_KV_RANK, MLA_HEADS, SLOT - MLA_NOPE), F32)],
                          axis=-1).reshape(MLA_KV_RANK, -1).astype(BF16)
    w_v = jnp.concatenate([wkv3[..., MLA_NOPE:], jnp.zeros((MLA_KV_RANK, MLA_HEADS, MLA_VROWS - MLA_V), F32)],
                          axis=-1).reshape(MLA_KV_RANK, -1).T.astype(BF16)
    v_ones = jnp.zeros((MLA_HEADS, MLA_VROWS), F32).at[:, MLA_V].set(1.0).reshape(-1, 1)

    lanevecs = jnp.concatenate([
        _slot([(0, g_q_nope[None]), (MLA_NOPE, g_q_rope[None])], 1),
        _slot([(MLA_NOPE, _swap_halves(g_q_rope)[None])], 1),
        _slot([(0, g_k_nope[None])], 1),
        _slot([(MLA_NOPE, g_k_rope[None])], 1),
        _slot([(MLA_NOPE, _swap_halves(g_k_rope)[None])], 1),
        jnp.zeros((3, SLOT), F32)], axis=0)
    seg = np.zeros((SLOT, SLOT), np.float32)
    seg[:MLA_NOPE, :MLA_NOPE] = 1.0 / MLA_NOPE
    seg[MLA_NOPE:MLA_NOPE + MLA_ROPE, MLA_NOPE:MLA_NOPE + MLA_ROPE] = 1.0 / MLA_ROPE
    bd = jnp.asarray(seg, BF16)

    half = MLA_ROPE // 2
    inv = ROPE_BASE ** (-jnp.arange(half, dtype=F32) / half)
    ang = jnp.arange(s_max, dtype=F32)[:, None] * inv[None, :]
    cos, sin = jnp.cos(ang), jnp.sin(ang)
    pad = jnp.zeros((s_max, SLOT - MLA_NOPE - MLA_ROPE), F32)
    cos_t = jnp.concatenate([jnp.ones((s_max, MLA_NOPE), F32), cos, cos, pad], axis=1)
    sin_t = jnp.concatenate([jnp.zeros((s_max, MLA_NOPE), F32), -sin, sin, pad], axis=1)

    w_route = jnp.zeros((d, LANES), F32).at[:, :N_EXPERTS].set(w_router)
    w_route = w_route.at[:, N_EXPERTS:N_EXPERTS + N_GROUPS].set(w_group)
    b_route = jnp.zeros((1, LANES), F32).at[0, :N_EXPERTS].set(b_router)
    b_route = b_route.at[0, N_EXPERTS:N_EXPERTS + N_GROUPS].set(b_group)

    return dict(
        g1=g_norm1.reshape(1, d), w_in=w_in_ext, w_gk=w_gk, b_gk=b_gk,
        g_qa=g_q_a.reshape(1, -1), w_q=w_q, g_kva=g_kv_a.reshape(1, -1), w_k=w_k, w_v=w_v,
        v_ones=v_ones, bd=bd, lanevecs=lanevecs, colvecs=lanevecs.T, cos=cos_t, sin=sin_t,
        g_gla=g_gla_out.reshape(1, -1), w_gla_o=w_gla_o.astype(BF16), w_mla_o=w_mla_o.astype(BF16),
        w_out=w_out.astype(BF16), g2=g_norm2.reshape(1, d), w_route=w_route, b_route=b_route,
        w_e_gate=w_e_gate.astype(BF16), w_e_up=w_e_up.astype(BF16), w_e_down=w_e_down.astype(BF16))


def _layer(x, c, w_ada, b_ada, wp):
    mod = _ada(c, w_ada, b_ada)
    q, k, v, zg, zga, zgm, lgf, lgb, qm, km, vm = _inproj(x, mod, wp)
    o_f, o_b = _gla(q, k, v, lgf, lgb)
    o_mla = _mla(qm, km, vm)
    x1, h2, route, counts = _merge(o_f, o_b, zg, zga, zgm, o_mla, x, mod, wp)
    return _moe(x1, h2, route, counts, mod, wp)


def kernel(x_prompt, x_sample, c_prompt, c_sample, w_ada, b_ada, g_norm1, w_in, w_gk_fwd, b_gk_fwd, w_gk_bwd, b_gk_bwd, g_gla_out, w_gla_o, g_q_a, w_q_b, g_kv_a, w_kv_b, g_q_nope, g_k_nope, g_q_rope, g_k_rope, w_mla_o, w_out, g_norm2, w_group, b_group, w_router, b_router, w_e_gate, w_e_up, w_e_down):
    assert w_ada.shape[0] == 1, "single-layer trunk"
    s_max = max(x_prompt.shape[1], x_sample.shape[1])
    wp = _prep_weights(s_max, *[p[0] for p in (
        g_norm1, w_in, w_gk_fwd, b_gk_fwd, w_gk_bwd, b_gk_bwd, g_gla_out, w_gla_o, g_q_a, w_q_b, g_kv_a,
        w_kv_b, g_q_nope, g_k_nope, g_q_rope, g_k_rope, w_mla_o, w_out, g_norm2, w_group, b_group,
        w_router, b_router, w_e_gate, w_e_up, w_e_down)])
    y_prompt = _layer(x_prompt, c_prompt, w_ada[0], b_ada[0], wp)
    y_sample = _layer(x_sample, c_sample, w_ada[0], b_ada[0], wp)
    return (y_prompt, y_sample)
```

```python
import functools
import math

import numpy as np
import jax
import jax.numpy as jnp
from jax import lax
from jax.experimental import pallas as pl
from jax.experimental.pallas import tpu as pltpu

F32 = jnp.float32
BF16 = jnp.bfloat16

D_MODEL = 1024
GLA_HEADS, GLA_DK, GLA_DV, GLA_RANK = 4, 128, 256, 16
GLA_GATE_NORM = 16.0
MLA_HEADS, MLA_Q_RANK, MLA_KV_RANK = 8, 256, 128
MLA_NOPE, MLA_ROPE, MLA_V = 64, 32, 64
ROPE_BASE = 10000.0
N_GROUPS, EXPERTS_PER_GROUP, TOP_K, D_EXPERT = 4, 8, 2, 512
N_EXPERTS = N_GROUPS * EXPERTS_PER_GROUP
NORM_EPS = 1e-6

LANES = 128
ROW_CHUNKS = D_MODEL // LANES
SLOT = LANES
VMEM_LIMIT = 56 * 1024 * 1024

ROW_TILE = 256
GLA_CHUNK = 64
GLA_BLOCK = 256
GLA_EXP_CLAMP = 80.0
MLA_TQ = 512
MLA_VROWS = MLA_V + 16
MLA_UNROLL = 2
MOE_TILE = 256
MISC_COLS = MLA_Q_RANK + MLA_KV_RANK + 3 * SLOT


def _cparams(sem):
    return pltpu.CompilerParams(dimension_semantics=sem, vmem_limit_bytes=VMEM_LIMIT)


def _dot(a, b):
    return jnp.dot(a, b, preferred_element_type=F32)


def _dot_nt(a, b):
    return lax.dot_general(a, b, (((1,), (1,)), ((), ())), preferred_element_type=F32)


def _split_bf16(x):
    hi = x.astype(BF16)
    lo = (x - hi.astype(F32)).astype(BF16)
    return hi, lo


def _dot_split_lhs(x, w):
    hi, lo = _split_bf16(x)
    return _dot(hi, w) + _dot(lo, w)


def _sigmoid(x):
    return 1.0 / (1.0 + jnp.exp(-x))


def _silu(x):
    return x * _sigmoid(x)


def _ada_kernel(c_ref, w_ref, b_ref, o_ref):
    a = _silu(c_ref[...])
    ahi, alo = _split_bf16(a)
    w = w_ref[...]
    whi, wlo = _split_bf16(w)
    o_ref[...] = _dot(ahi, whi) + _dot(ahi, wlo) + _dot(alo, whi) + b_ref[...]


def _ada(c, w_ada, b_ada):
    nb = c.shape[0]
    cp = jnp.zeros((8, D_MODEL), F32).at[:nb].set(c)
    tn = 1536
    out = pl.pallas_call(
        _ada_kernel,
        out_shape=jax.ShapeDtypeStruct((8, 6 * D_MODEL), F32),
        grid=(6 * D_MODEL // tn,),
        in_specs=[pl.BlockSpec((8, D_MODEL), lambda j: (0, 0)),
                  pl.BlockSpec((D_MODEL, tn), lambda j: (0, j)),
                  pl.BlockSpec((1, tn), lambda j: (0, j))],
        out_specs=pl.BlockSpec((8, tn), lambda j: (0, j)),
        compiler_params=_cparams(("arbitrary",)),
        name="ada",
    )(cp, w_ada, b_ada.reshape(1, -1))
    mod = out[:nb].reshape(nb, 6, D_MODEL)
    return jnp.concatenate([mod, jnp.zeros((nb, 2, D_MODEL), F32)], axis=1)


_C_Q, _C_K, _C_V, _C_G, _C_GA, _C_GM = 0, 512, 1024, 2048, 3072, 4096
_C_MISC = 5120
IN_EXT = _C_MISC + MISC_COLS


def _segmean(xsq, bd):
    return _dot_split_lhs(xsq, bd)


def _inproj_kernel(x_ref, mod_ref, g1_ref, win_ref, wgk_ref, bgk_ref, gqa_ref, wq_ref, gkva_ref,
                   wk_ref, wv_ref, vone_ref, bd_ref, lv_ref, cv_ref, cos_ref, sin_ref, cosc_ref, sinc_ref,
                   q_o, k_o, v_o, zg_o, zga_o, zgm_o, lgf_o, lgb_o, qm_o, km_o, vm_o, h_scr):
    x = x_ref[0]
    ms = jnp.mean(x * x, axis=-1, keepdims=True)
    h = (x * lax.rsqrt(ms + NORM_EPS) * g1_ref[...]) * (1.0 + mod_ref[0, 1:2, :]) + mod_ref[0, 0:1, :]
    h_scr[...] = h.astype(BF16)

    def proj(c0, n):
        return _dot(h_scr[...], win_ref[:, c0:c0 + n])

    q_o[0] = (proj(_C_Q, 512) * (GLA_DK ** -0.5)).astype(BF16)
    k_o[0] = proj(_C_K, 512).astype(BF16)
    v_o[0] = proj(_C_V, 1024).astype(BF16)
    zg_o[0] = proj(_C_G, 1024).astype(BF16)
    zga_o[0] = proj(_C_GA, 1024).astype(BF16)
    zgm_o[0] = proj(_C_GM, 1024).astype(BF16)
    misc = proj(_C_MISC, MISC_COLS)

    bd = bd_ref[...]
    cos_t = cos_ref[...]
    sin_t = sin_ref[...]
    g_kn, g_kr, g_krs = lv_ref[2:3, :], lv_ref[3:4, :], lv_ref[4:5, :]
    scale = (MLA_NOPE + MLA_ROPE) ** -0.5 * math.log2(math.e)

    zqa = misc[:, 0:MLA_Q_RANK]
    qa = zqa * lax.rsqrt(jnp.mean(zqa * zqa, axis=-1, keepdims=True) + NORM_EPS) * gqa_ref[...]
    qq = _dot_nt(wq_ref[...], qa.astype(BF16))
    cos_c, sin_c = cosc_ref[...], sinc_ref[...]
    gc_q, gc_qs = cv_ref[:, 0:1], cv_ref[:, 1:2]
    for hh in range(MLA_HEADS):
        xq = qq[hh * SLOT:(hh + 1) * SLOT, :]
        xs = qq[(MLA_HEADS + hh) * SLOT:(MLA_HEADS + hh + 1) * SLOT, :]
        hi, lo = _split_bf16(xq * xq)
        r = lax.rsqrt(_dot(bd, hi) + _dot(bd, lo) + NORM_EPS)
        qm = (xq * gc_q * cos_c + xs * gc_qs * sin_c) * (r * scale)
        qm_o[0, hh * SLOT:(hh + 1) * SLOT, :] = qm.astype(BF16)

    c0 = MLA_Q_RANK
    zkva = misc[:, c0:c0 + MLA_KV_RANK]
    kva = zkva * lax.rsqrt(jnp.mean(zkva * zkva, axis=-1, keepdims=True) + NORM_EPS) * gkva_ref[...]
    kvab = kva.astype(BF16)
    kk = _dot(kvab, wk_ref[...])
    vm_o[0, 0] = (_dot_nt(wv_ref[...], kvab) + vone_ref[...]).astype(BF16)
    c0 += MLA_KV_RANK
    kr = misc[:, c0:c0 + SLOT]
    krs = misc[:, c0 + SLOT:c0 + 2 * SLOT]
    r_kr = lax.rsqrt(_segmean(kr * kr, bd) + NORM_EPS)
    kpe = (kr * g_kr * cos_t + krs * g_krs * sin_t) * r_kr
    for hh in range(MLA_HEADS):
        xk = kk[:, hh * SLOT:(hh + 1) * SLOT]
        r = lax.rsqrt(_segmean(xk * xk, bd) + NORM_EPS)
        km_o[0, :, hh * SLOT:(hh + 1) * SLOT] = (xk * r * g_kn + kpe).astype(BF16)

    c0 += 2 * SLOT
    zgate = misc[:, c0:c0 + SLOT].astype(BF16)
    pre = _dot(zgate, wgk_ref[...]) + bgk_ref[...]
    lg = (jnp.minimum(pre, 0.0) - jnp.log(1.0 + jnp.exp(-jnp.abs(pre)))) * (1.0 / GLA_GATE_NORM)
    lgf_o[0] = lg[:, :GLA_HEADS * GLA_DK]
    lgb_o[0] = lg[:, GLA_HEADS * GLA_DK:]


def _inproj(x, mod, wp):
    nb, s, d = x.shape
    tm = min(ROW_TILE, s)
    const = lambda b, i: (0, 0)
    row = lambda b, i: (b, i, 0)

    def full(a):
        return pl.BlockSpec(a.shape, const)

    def out(n, dt):
        return jax.ShapeDtypeStruct((nb, s, n), dt), pl.BlockSpec((1, tm, n), row)

    outs = [out(512, BF16), out(512, BF16), out(1024, BF16), out(1024, BF16), out(1024, BF16),
            out(1024, BF16), out(512, F32), out(512, F32),
            (jax.ShapeDtypeStruct((nb, MLA_HEADS * SLOT, s), BF16),
             pl.BlockSpec((1, MLA_HEADS * SLOT, tm), lambda b, i: (b, 0, i))),
            out(MLA_HEADS * SLOT, BF16),
            (jax.ShapeDtypeStruct((nb, s // tm, MLA_HEADS * MLA_VROWS, tm), BF16),
             pl.BlockSpec((1, 1, MLA_HEADS * MLA_VROWS, tm), lambda b, i: (b, i, 0, 0)))]
    return pl.pallas_call(
        _inproj_kernel,
        out_shape=[o[0] for o in outs],
        grid=(nb, s // tm),
        in_specs=[pl.BlockSpec((1, tm, d), row),
                  pl.BlockSpec((1, 8, d), lambda b, i: (b, 0, 0)),
                  full(wp["g1"]), full(wp["w_in"]), full(wp["w_gk"]), full(wp["b_gk"]),
                  full(wp["g_qa"]), full(wp["w_q"]), full(wp["g_kva"]), full(wp["w_k"]), full(wp["w_v"]),
                  full(wp["v_ones"]), full(wp["bd"]), full(wp["lanevecs"]), full(wp["colvecs"]),
                  pl.BlockSpec((tm, SLOT), lambda b, i: (i, 0)),
                  pl.BlockSpec((tm, SLOT), lambda b, i: (i, 0)),
                  pl.BlockSpec((SLOT, tm), lambda b, i: (0, i)),
                  pl.BlockSpec((SLOT, tm), lambda b, i: (0, i))],
        out_specs=[o[1] for o in outs],
        scratch_shapes=[pltpu.VMEM((tm, d), BF16)],
        compiler_params=_cparams(("arbitrary", "arbitrary")),
        name="inproj",
    )(x, mod, wp["g1"], wp["w_in"], wp["w_gk"], wp["b_gk"], wp["g_qa"], wp["w_q"], wp["g_kva"],
      wp["w_k"], wp["w_v"], wp["v_ones"], wp["bd"], wp["lanevecs"], wp["colvecs"],
      wp["cos"][:s], wp["sin"][:s], wp["cos"][:s].T, wp["sin"][:s].T)


def _dot_tn(a, b):
    return lax.dot_general(a, b, (((0,), (0,)), ((), ())), preferred_element_type=F32)


def _gla_direction(q_ref, k_ref, v_ref, g_ref, o_ref, st_ref, scr, cum, mask, reverse):
    qin_s, qmid_s, kmid_s, kout_s, tot_s, u_s = scr
    c = GLA_CHUNK
    nch = q_ref.shape[1] // c
    rows = [slice(j * c, (j + 1) * c) for j in range(nch)]
    ks = [slice(h * GLA_DK, (h + 1) * GLA_DK) for h in range(GLA_HEADS)]
    vs = [slice(h * GLA_DV, (h + 1) * GLA_DV) for h in range(GLA_HEADS)]

    for j in range(nch):
        hi, lo = _split_bf16(g_ref[0, rows[j], :])
        b = _dot(cum, hi) + _dot(cum, lo)
        mid = b[c // 2:c // 2 + 1, :]
        tot = b[0:1, :] if reverse else b[c - 1:c, :]
        q = q_ref[0, rows[j], :].astype(F32)
        k = k_ref[0, rows[j], :].astype(F32)
        qin_s[rows[j], :] = (q * jnp.exp(b)).astype(BF16)
        qmid_s[rows[j], :] = (q * jnp.exp(jnp.minimum(b - mid, GLA_EXP_CLAMP))).astype(BF16)
        kmid_s[rows[j], :] = (k * jnp.exp(jnp.minimum(mid - b, GLA_EXP_CLAMP))).astype(BF16)
        kout_s[rows[j], :] = (k * jnp.exp(tot - b)).astype(BF16)
        tot_s[j:j + 1, :] = tot

    attn = {}
    for j in range(nch):
        for h in range(GLA_HEADS):
            a = _dot_nt(qmid_s[rows[j], ks[h]], kmid_s[rows[j], ks[h]])
            attn[j, h] = jnp.where(mask, a, 0.0).astype(BF16)

    for j in range(nch):
        for h in range(GLA_HEADS):
            v = v_ref[0, rows[j], vs[h]]
            o_ref[0, rows[j], vs[h]] = _dot(attn[j, h], v)
            u_s[j * GLA_HEADS + h] = _dot_tn(kout_s[rows[j], ks[h]], v)

    for j in (range(nch - 1, -1, -1) if reverse else range(nch)):
        for h in range(GLA_HEADS):
            st = st_ref[h]
            o_ref[0, rows[j], vs[h]] += _dot(qin_s[rows[j], ks[h]], st.astype(BF16))
            tot = jnp.broadcast_to(tot_s[j:j + 1, ks[h]], (GLA_DK, GLA_DK))
            dec = jnp.exp(tot.T)
            st_ref[h] = jnp.concatenate([dec] * (GLA_DV // GLA_DK), axis=1) * st + u_s[j * GLA_HEADS + h]


def _gla_kernel(qf, kf, vf, gf, qb, kb, vb, gb, of_ref, ob_ref, sf_ref, sb_ref, *scr):
    @pl.when(pl.program_id(1) == 0)
    def _():
        sf_ref[...] = jnp.zeros_like(sf_ref)
        sb_ref[...] = jnp.zeros_like(sb_ref)

    c = GLA_CHUNK
    ri = lax.broadcasted_iota(jnp.int32, (c, c), 0)
    ci = lax.broadcasted_iota(jnp.int32, (c, c), 1)
    cum_f = jnp.where(ri >= ci, 1.0, 0.0).astype(BF16)
    cum_b = jnp.where(ci >= ri, 1.0, 0.0).astype(BF16)
    n = len(scr) // 2
    _gla_direction(qf, kf, vf, gf, of_ref, sf_ref, scr[:n], cum_f, ri >= ci, False)
    _gla_direction(qb, kb, vb, gb, ob_ref, sb_ref, scr[n:], cum_b, ci > ri, True)


def _gla(q, k, v, lgf, lgb):
    nb, s, _ = q.shape
    cb = min(GLA_BLOCK, s)
    ns = s // cb
    nch = cb // GLA_CHUNK
    fwd = lambda b, i: (b, i, 0)
    bwd = lambda b, i: (b, ns - 1 - i, 0)
    hk, hv = GLA_HEADS * GLA_DK, GLA_HEADS * GLA_DV

    def specs(im):
        return [pl.BlockSpec((1, cb, hk), im), pl.BlockSpec((1, cb, hk), im),
                pl.BlockSpec((1, cb, hv), im), pl.BlockSpec((1, cb, hk), im)]

    per_dir = [pltpu.VMEM((cb, hk), BF16)] * 4 + [pltpu.VMEM((8, hk), F32),
                                                  pltpu.VMEM((nch * GLA_HEADS, GLA_DK, GLA_DV), F32)]
    return pl.pallas_call(
        _gla_kernel,
        out_shape=[jax.ShapeDtypeStruct((nb, s, hv), F32)] * 2,
        grid=(nb, ns),
        in_specs=specs(fwd) + specs(bwd),
        out_specs=[pl.BlockSpec((1, cb, hv), fwd), pl.BlockSpec((1, cb, hv), bwd)],
        scratch_shapes=[pltpu.VMEM((GLA_HEADS, GLA_DK, GLA_DV), F32)] * 2 + per_dir * 2,
        compiler_params=_cparams(("arbitrary", "arbitrary")),
        name="gla",
    )(q, k, v, lgf, q, k, v, lgb)


def _mla_kernel(q_ref, k_ref, vt_ref, o_ref, sc0, sc1, p0, p1):
    tq = q_ref.shape[2]
    nk = vt_ref.shape[1]
    tk = vt_ref.shape[3]
    gu = MLA_UNROLL
    ng = nk // gu
    sc, pp = (sc0, sc1), (p0, p1)
    qs = [q_ref[0, hh * SLOT:(hh + 1) * SLOT, :] for hh in range(2)]
    tiles = [(u, hh) for u in range(gu) for hh in range(2)]

    def issue_scores(g, slot):
        tmax = []
        for u, hh in tiles:
            r0 = pl.multiple_of((g * gu + u) * tk, tk)
            kt = k_ref[0, pl.ds(r0, tk), hh * SLOT:(hh + 1) * SLOT]
            st = _dot(kt, qs[hh])
            sc[slot][2 * u + hh] = st.astype(BF16)
            tmax.append(jnp.max(st, axis=0, keepdims=True).astype(BF16).astype(F32))
        return tuple(tmax)

    def softmax(slot, ms, tmax):
        ms = list(ms)
        alphas = []
        for u, hh in tiles:
            m_new = jnp.maximum(ms[hh], tmax[2 * u + hh])
            alphas.append(jnp.exp2(ms[hh] - m_new))
            pp[slot][2 * u + hh] = jnp.exp2(sc[slot][2 * u + hh] - m_new.astype(BF16))
            ms[hh] = m_new
        return tuple(ms), tuple(alphas)

    def values(g, slot, alphas, accs):
        accs = list(accs)
        for u, hh in tiles:
            vt = vt_ref[0, g * gu + u, hh * MLA_VROWS:(hh + 1) * MLA_VROWS, :]
            accs[hh] = alphas[2 * u + hh] * accs[hh] + _dot(vt, pp[slot][2 * u + hh])
        return tuple(accs)

    def step(g, slot, carry, more_scores):
        ms, alphas, accs, tmax = carry
        accs = values(g - 1, 1 - slot, alphas, accs)
        ms, alphas = softmax(slot, ms, tmax)
        if more_scores:
            tmax = issue_scores(g + 1, 1 - slot)
        return ms, alphas, accs, tmax

    def double_step(i, carry):
        g = 2 * i + 1
        carry = step(g, 1, carry, True)
        return step(g + 1, 0, carry, True)

    m0 = jnp.full((1, tq), -jnp.inf, F32)
    acc0 = jnp.zeros((MLA_VROWS, tq), F32)
    tmax = issue_scores(0, 0)
    ms, alphas = softmax(0, (m0, m0), tmax)
    tmax = issue_scores(1, 1)
    carry = (ms, alphas, (acc0, acc0), tmax)
    for i in range((ng - 2) // 2):
        carry = double_step(i, carry)
    ms, alphas, accs, _ = step(ng - 1, 1, carry, False)
    accs = values(ng - 1, 1, alphas, accs)
    outs = [(a[:MLA_V] / a[MLA_V:MLA_V + 1]).T for a in accs]
    o_ref[0] = jnp.concatenate(outs, axis=-1).astype(BF16)


def _mla(qm, km, vmt):
    nb, s, _ = km.shape
    nk, tk = vmt.shape[1], vmt.shape[3]
    tq = min(MLA_TQ, s)
    assert (nk // MLA_UNROLL) % 2 == 0 and nk % MLA_UNROLL == 0
    slot_tiles = 2 * MLA_UNROLL
    return pl.pallas_call(
        _mla_kernel,
        out_shape=jax.ShapeDtypeStruct((nb, s, MLA_HEADS * MLA_V), BF16),
        grid=(nb, MLA_HEADS // 2, s // tq),
        in_specs=[pl.BlockSpec((1, 2 * SLOT, tq), lambda b, p, i: (b, p, i)),
                  pl.BlockSpec((1, s, 2 * SLOT), lambda b, p, i: (b, 0, p)),
                  pl.BlockSpec((1, nk, 2 * MLA_VROWS, tk), lambda b, p, i: (b, 0, p, 0))],
        out_specs=pl.BlockSpec((1, tq, 2 * MLA_V), lambda b, p, i: (b, i, p)),
        scratch_shapes=[pltpu.VMEM((slot_tiles, tk, tq), BF16)] * 4,
        compiler_params=_cparams(("arbitrary", "arbitrary", "arbitrary")),
        name="mla",
    )(qm, km, vmt)


def _merge_kernel(of_ref, ob_ref, zg_ref, zga_ref, zgm_ref, om_ref, x_ref, mod_ref, ggla_ref, wglo_ref,
                  wmo_ref, wout_ref, g2_ref, wr_ref, br_ref, ltri_ref,
                  x1_o, h2_o, route_o, cnt_o, cnt_scr):
    first = (pl.program_id(0) == 0) & (pl.program_id(1) == 0)

    @pl.when(first)
    def _():
        cnt_scr[...] = jnp.zeros_like(cnt_scr)

    tm = x_ref.shape[1]
    o = of_ref[0] + ob_ref[0]
    zg = zg_ref[0].astype(F32)
    parts = []
    for hh in range(GLA_HEADS):
        seg = o[:, hh * GLA_DV:(hh + 1) * GLA_DV]
        r = lax.rsqrt(jnp.mean(seg * seg, axis=-1, keepdims=True) + NORM_EPS)
        parts.append((seg * r * ggla_ref[...]) * _silu(zg[:, hh * GLA_DV:(hh + 1) * GLA_DV]))
    og = jnp.concatenate(parts, axis=-1).astype(BF16)
    y_gla = _dot(og, wglo_ref[...])
    y_mla = _dot(om_ref[0], wmo_ref[...])
    merged = _sigmoid(zga_ref[0].astype(F32)) * y_gla + _sigmoid(zgm_ref[0].astype(F32)) * y_mla
    mix = _dot(merged.astype(BF16), wout_ref[...])
    x1 = x_ref[0] + mod_ref[0, 2:3, :] * mix
    x1_o[0] = x1
    ms = jnp.mean(x1 * x1, axis=-1, keepdims=True)
    h2 = (x1 * lax.rsqrt(ms + NORM_EPS) * g2_ref[...]) * (1.0 + mod_ref[0, 4:5, :]) + mod_ref[0, 3:4, :]
    for c in range(ROW_CHUNKS):
        h2_o[pl.ds(c, tm, stride=ROW_CHUNKS), :] = h2[:, c * LANES:(c + 1) * LANES]

    hhi, hlo = _split_bf16(h2)
    w = wr_ref[...]
    whi, wlo = _split_bf16(w)
    logits = _dot(hhi, whi) + _dot(hhi, wlo) + _dot(hlo, whi) + br_ref[...]
    lane = lax.broadcasted_iota(jnp.int32, (tm, LANES), 1)
    neg = -jnp.inf
    is_g = (lane >= N_EXPERTS) & (lane < N_EXPERTS + N_GROUPS)
    lgm = jnp.where(is_g, logits, neg)
    mg = jnp.max(lgm, axis=-1, keepdims=True)
    sg = jnp.sum(jnp.exp(lgm - mg), axis=-1, keepdims=True)
    p_top = 1.0 / sg
    g_idx = jnp.min(jnp.where(lgm == mg, lane, 2 * LANES), axis=-1, keepdims=True) - N_EXPERTS
    in_grp = (lane >= g_idx * EXPERTS_PER_GROUP) & (lane < (g_idx + 1) * EXPERTS_PER_GROUP)
    el = jnp.where(in_grp, logits, neg)
    m1 = jnp.max(el, axis=-1, keepdims=True)
    e1 = jnp.min(jnp.where(el == m1, lane, 2 * LANES), axis=-1, keepdims=True)
    el2 = jnp.where(lane == e1, neg, el)
    m2 = jnp.max(el2, axis=-1, keepdims=True)
    e2 = jnp.min(jnp.where(el2 == m2, lane, 2 * LANES), axis=-1, keepdims=True)
    se = jnp.sum(jnp.exp(el - m1), axis=-1, keepdims=True)
    pv1 = 1.0 / se
    pv2 = jnp.exp(m2 - m1) / se
    w1 = p_top * pv1 / (pv1 + pv2)
    w2 = p_top * pv2 / (pv1 + pv2)

    oh1 = lane == e1
    oh2 = lane == e2
    oh = jnp.where(oh1 | oh2, 1.0, 0.0)
    prefix = _dot(ltri_ref[...], oh.astype(BF16)) + cnt_scr[...]
    r1 = jnp.sum(jnp.where(oh1, prefix, 0.0), axis=-1, keepdims=True)
    r2 = jnp.sum(jnp.where(oh2, prefix, 0.0), axis=-1, keepdims=True)
    cnt_scr[...] = cnt_scr[...] + jnp.sum(oh, axis=0, keepdims=True)
    cnt_o[...] = jnp.broadcast_to(cnt_scr[...], cnt_o.shape)

    route = jnp.where(lane == 0, e1.astype(F32), 0.0)
    route = jnp.where(lane == 1, e2.astype(F32), route)
    route = jnp.where(lane == 2, r1, route)
    route = jnp.where(lane == 3, r2, route)
    route = jnp.where(lane == 4, w1, route)
    route = jnp.where(lane == 5, w2, route)
    route_o[0] = route


def _merge(o_f, o_b, zg, zga, zgm, o_mla, x, mod, wp):
    nb, s, d = x.shape
    tm = min(ROW_TILE, s)
    const = lambda b, i: (0, 0)
    row = lambda b, i: (b, i, 0)

    def full(a):
        return pl.BlockSpec(a.shape, const)

    def rows(n):
        return pl.BlockSpec((1, tm, n), row)

    ltri = jnp.asarray(np.tril(np.ones((tm, tm), np.float32), -1), BF16)
    return pl.pallas_call(
        _merge_kernel,
        out_shape=[jax.ShapeDtypeStruct((nb, s, d), F32), jax.ShapeDtypeStruct((nb * s * ROW_CHUNKS, LANES), F32),
                   jax.ShapeDtypeStruct((nb, s, LANES), F32), jax.ShapeDtypeStruct((8, LANES), F32)],
        grid=(nb, s // tm),
        in_specs=[rows(1024), rows(1024), rows(1024), rows(1024), rows(1024), rows(MLA_HEADS * MLA_V),
                  rows(d), pl.BlockSpec((1, 8, d), lambda b, i: (b, 0, 0)),
                  full(wp["g_gla"]), full(wp["w_gla_o"]), full(wp["w_mla_o"]), full(wp["w_out"]),
                  full(wp["g2"]), full(wp["w_route"]), full(wp["b_route"]), full(ltri)],
        out_specs=[rows(d), pl.BlockSpec((tm * ROW_CHUNKS, LANES), lambda b, i: (b * (s // tm) + i, 0)),
                   rows(LANES), pl.BlockSpec((8, LANES), const)],
        scratch_shapes=[pltpu.VMEM((1, LANES), F32)],
        compiler_params=_cparams(("arbitrary", "arbitrary")),
        name="merge",
    )(o_f, o_b, zg, zga, zgm, o_mla, x, mod, wp["g_gla"], wp["w_gla_o"], wp["w_mla_o"], wp["w_out"],
      wp["g2"], wp["w_route"], wp["b_route"], ltri)


def _dispatch_kernel(dest_ref, h_ref, xs_hbm, xs_out, sem):
    del xs_hbm
    tt = h_ref.shape[0] // ROW_CHUNKS

    def copy(r, slot):
        d0 = pl.multiple_of(dest_ref[0, 0, 2 * r + slot] * ROW_CHUNKS, ROW_CHUNKS)
        return pltpu.make_async_copy(h_ref.at[pl.ds(pl.multiple_of(r * ROW_CHUNKS, ROW_CHUNKS), ROW_CHUNKS)],
                                     xs_out.at[pl.ds(d0, ROW_CHUNKS)], sem.at[0])

    def start(r, c):
        copy(r, 0).start()
        copy(r, 1).start()
        return c

    def wait(r, c):
        copy(r, 0).wait()
        copy(r, 1).wait()
        return c

    lax.fori_loop(0, tt, start, 0, unroll=8)
    lax.fori_loop(0, tt, wait, 0, unroll=8)


def _dispatch(h2, dest, n_rows):
    t = h2.shape[0] // ROW_CHUNKS
    tt = min(ROW_TILE, t)
    nt = t // tt
    xs0 = jnp.zeros((n_rows * ROW_CHUNKS, LANES), F32)
    return pl.pallas_call(
        _dispatch_kernel,
        out_shape=jax.ShapeDtypeStruct((n_rows * ROW_CHUNKS, LANES), F32),
        grid=(nt,),
        in_specs=[pl.BlockSpec((1, 1, 2 * tt), lambda i: (i, 0, 0), memory_space=pltpu.SMEM),
                  pl.BlockSpec((tt * ROW_CHUNKS, LANES), lambda i: (i, 0)),
                  pl.BlockSpec(memory_space=pl.ANY)],
        out_specs=pl.BlockSpec(memory_space=pl.ANY),
        scratch_shapes=[pltpu.SemaphoreType.DMA((1,))],
        input_output_aliases={2: 0},
        compiler_params=_cparams(("arbitrary",)),
        name="dispatch",
    )(dest.reshape(nt, 1, 2 * tt), h2, xs0)


def _expert_kernel(te_ref, na_ref, xs_ref, wg_ref, wu_ref, wd_ref, ys_ref, xb_scr):
    active = pl.program_id(0) < na_ref[0]

    @pl.when(active)
    def _():
        for c in range(ROW_CHUNKS):
            xb_scr[:, c * LANES:(c + 1) * LANES] = xs_ref[pl.ds(c, MOE_TILE, stride=ROW_CHUNKS), :].astype(BF16)
        xb = xb_scr[...]
        a = _silu(_dot(xb, wg_ref[0])) * _dot(xb, wu_ref[0])
        y = _dot(a.astype(BF16), wd_ref[0])
        for c in range(ROW_CHUNKS):
            ys_ref[pl.ds(c, MOE_TILE, stride=ROW_CHUNKS), :] = y[:, c * LANES:(c + 1) * LANES]

    @pl.when(jnp.logical_not(active))
    def _():
        ys_ref[...] = jnp.zeros_like(ys_ref)


def _experts(xs, tile_expert, n_active, wp):
    d = D_MODEL
    n_rows = xs.shape[0] // ROW_CHUNKS
    nt = n_rows // MOE_TILE
    rows = MOE_TILE * ROW_CHUNKS

    def tile(i, te, na):
        return (jnp.minimum(i, na[0] - 1), 0)

    def wmap(i, te, na):
        return (te[jnp.minimum(i, na[0] - 1)], 0, 0)

    return pl.pallas_call(
        _expert_kernel,
        out_shape=jax.ShapeDtypeStruct((n_rows * ROW_CHUNKS, LANES), F32),
        grid_spec=pltpu.PrefetchScalarGridSpec(
            num_scalar_prefetch=2,
            grid=(nt,),
            in_specs=[pl.BlockSpec((rows, LANES), tile),
                      pl.BlockSpec((1, d, D_EXPERT), wmap),
                      pl.BlockSpec((1, d, D_EXPERT), wmap),
                      pl.BlockSpec((1, D_EXPERT, d), wmap)],
            out_specs=pl.BlockSpec((rows, LANES), lambda i, te, na: (i, 0)),
            scratch_shapes=[pltpu.VMEM((MOE_TILE, d), BF16)]),
        compiler_params=_cparams(("arbitrary",)),
        name="experts",
    )(tile_expert, n_active, xs, wp["w_e_gate"], wp["w_e_up"], wp["w_e_down"])


def _combine_kernel(dest_ref, x1_ref, rw_ref, ga_ref, ys_hbm, o_ref, buf, sem):
    tt = x1_ref.shape[0]

    def copy(r, slot):
        d0 = pl.multiple_of(dest_ref[0, 0, 2 * r + slot] * ROW_CHUNKS, ROW_CHUNKS)
        return pltpu.make_async_copy(ys_hbm.at[pl.ds(d0, ROW_CHUNKS)],
                                     buf.at[slot, pl.ds(pl.multiple_of(r * ROW_CHUNKS, ROW_CHUNKS), ROW_CHUNKS)],
                                     sem.at[0])

    def start(r, c):
        copy(r, 0).start()
        copy(r, 1).start()
        return c

    def wait(r, c):
        copy(r, 0).wait()
        copy(r, 1).wait()
        return c

    lax.fori_loop(0, tt, start, 0, unroll=8)
    lax.fori_loop(0, tt, wait, 0, unroll=8)
    rw = rw_ref[...]
    for c in range(ROW_CHUNKS):
        cs = slice(c * LANES, (c + 1) * LANES)
        rows = pl.ds(c, tt, stride=ROW_CHUNKS)
        moe = buf[0, rows, :] * rw[:, 4:5] + buf[1, rows, :] * rw[:, 5:6]
        o_ref[:, cs] = x1_ref[:, cs] + ga_ref[0, 5:6, cs] * moe


def _combine(x1, route, mod, ys, dest, s):
    t, d = x1.shape
    tt = min(ROW_TILE, s)
    nt = t // tt
    per_b = s // tt
    return pl.pallas_call(
        _combine_kernel,
        out_shape=jax.ShapeDtypeStruct((t, d), F32),
        grid=(nt,),
        in_specs=[pl.BlockSpec((1, 1, 2 * tt), lambda i: (i, 0, 0), memory_space=pltpu.SMEM),
                  pl.BlockSpec((tt, d), lambda i: (i, 0)),
                  pl.BlockSpec((tt, LANES), lambda i: (i, 0)),
                  pl.BlockSpec((1, 8, d), lambda i: (i // per_b, 0, 0)),
                  pl.BlockSpec(memory_space=pl.ANY)],
        out_specs=pl.BlockSpec((tt, d), lambda i: (i, 0)),
        scratch_shapes=[pltpu.VMEM((2, tt * ROW_CHUNKS, LANES), F32), pltpu.SemaphoreType.DMA((1,))],
        compiler_params=_cparams(("arbitrary",)),
        name="combine",
    )(dest.reshape(nt, 1, 2 * tt), x1, route, mod, ys)


def _moe(x1, h2, route, counts, mod, wp):
    nb, s, d = x1.shape
    t = nb * s
    route2 = route.reshape(t, LANES)
    e_flat = route2[:, 0:2].astype(jnp.int32)
    rank = route2[:, 2:4].astype(jnp.int32)
    cnt = counts[0, :N_EXPERTS].astype(jnp.int32)
    padded = (cnt + MOE_TILE - 1) // MOE_TILE * MOE_TILE
    pad_end = jnp.cumsum(padded)
    pad_start = pad_end - padded
    dest = (pad_start[e_flat] + rank).reshape(t * TOP_K)
    n_rows = t * TOP_K + N_EXPERTS * MOE_TILE
    nt = n_rows // MOE_TILE
    tile_row = jnp.arange(nt, dtype=jnp.int32) * MOE_TILE
    tile_expert = jnp.minimum(jnp.sum((pad_end[None, :] <= tile_row[:, None]).astype(jnp.int32), axis=1),
                              N_EXPERTS - 1)
    n_active = (pad_end[-1:] // MOE_TILE).astype(jnp.int32)
    xs = _dispatch(h2, dest, n_rows)
    ys = _experts(xs, tile_expert, n_active, wp)
    out = _combine(x1.reshape(t, d), route2, mod, ys, dest, s)
    return out.reshape(nb, s, d)


def _swap_halves(a, axis=-1):
    n = a.shape[axis] // 2
    lo, hi = jnp.split(a, [n], axis=axis)
    return jnp.concatenate([hi, lo], axis=axis)


def _slot(parts, n_lead):
    out = jnp.zeros((n_lead, SLOT), F32)
    for off, a in parts:
        out = out.at[:, off:off + a.shape[-1]].set(a)
    return out


def _prep_weights(s_max, g_norm1, w_in, w_gk_fwd, b_gk_fwd, w_gk_bwd, b_gk_bwd, g_gla_out, w_gla_o, g_q_a,
                  w_q_b, g_kv_a, w_kv_b, g_q_nope, g_k_nope, g_q_rope, g_k_rope, w_mla_o, w_out, g_norm2,
                  w_group, b_group, w_router, b_router, w_e_gate, w_e_up, w_e_down):
    d = D_MODEL
    hk, hv = GLA_HEADS * GLA_DK, GLA_HEADS * GLA_DV
    offs = np.cumsum([0, hk, hk, hv, hv, GLA_RANK, GLA_RANK, MLA_Q_RANK, MLA_KV_RANK, MLA_ROPE, d, d])
    cols = [w_in[:, offs[i]:offs[i + 1]] for i in range(11)]
    wq_, wk_, wv_, wg_, wgf, wgb, wqa, wkva, wkr, wga, wgm = cols
    kr_slot = _slot([(MLA_NOPE, wkr)], d)
    krs_slot = _slot([(MLA_NOPE, _swap_halves(wkr))], d)
    gate_slot = _slot([(0, wgf), (GLA_RANK, wgb)], d)
    w_in_ext = jnp.concatenate([wq_, wk_, wv_, wg_, wga, wgm, wqa, wkva, kr_slot, krs_slot, gate_slot],
                               axis=1).astype(BF16)
    w_gk = jnp.zeros((SLOT, 2 * hk), F32)
    w_gk = w_gk.at[0:GLA_RANK, 0:hk].set(w_gk_fwd).at[GLA_RANK:2 * GLA_RANK, hk:].set(w_gk_bwd).astype(BF16)
    b_gk = jnp.concatenate([b_gk_fwd, b_gk_bwd]).reshape(1, -1)

    wq3 = w_q_b.reshape(MLA_Q_RANK, MLA_HEADS, MLA_NOPE + MLA_ROPE)
    zq = jnp.zeros((MLA_Q_RANK, MLA_HEADS, SLOT - MLA_NOPE - MLA_ROPE), F32)
    main = jnp.concatenate([wq3, zq], axis=-1)
    swp = jnp.concatenate([jnp.zeros((MLA_Q_RANK, MLA_HEADS, MLA_NOPE), F32),
                           _swap_halves(wq3[..., MLA_NOPE:]), zq], axis=-1)
    w_q = jnp.concatenate([main.reshape(MLA_Q_RANK, -1), swp.reshape(MLA_Q_RANK, -1)], axis=1).T.astype(BF16)
    wkv3 = w_kv_b.reshape(MLA_KV_RANK, MLA_HEADS, MLA_NOPE + MLA_V)
    w_k = jnp.concatenate([wkv3[..., :MLA_NOPE], jnp.zeros((MLA_KV_RANK, MLA_HEADS, SLOT - MLA_NOPE), F32)],
                          axis=-1).reshape(MLA_KV_RANK, -1).astype(BF16)
    w_v = jnp.concatenate([wkv3[..., MLA_NOPE:], jnp.zeros((MLA_KV_RANK, MLA_HEADS, MLA_VROWS - MLA_V), F32)],
                          axis=-1).reshape(MLA_KV_RANK, -1).T.astype(BF16)
    v_ones = jnp.zeros((MLA_HEADS, MLA_VROWS), F32).at[:, MLA_V].set(1.0).reshape(-1, 1)

    lanevecs = jnp.concatenate([
        _slot([(0, g_q_nope[None]), (MLA_NOPE, g_q_rope[None])], 1),
        _slot([(MLA_NOPE, _swap_halves(g_q_rope)[None])], 1),
        _slot([(0, g_k_nope[None])], 1),
        _slot([(MLA_NOPE, g_k_rope[None])], 1),
        _slot([(MLA_NOPE, _swap_halves(g_k_rope)[None])], 1),
        jnp.zeros((3, SLOT), F32)], axis=0)
    seg = np.zeros((SLOT, SLOT), np.float32)
    seg[:MLA_NOPE, :MLA_NOPE] = 1.0 / MLA_NOPE
    seg[MLA_NOPE:MLA_NOPE + MLA_ROPE, MLA_NOPE:MLA_NOPE + MLA_ROPE] = 1.0 / MLA_ROPE
    bd = jnp.asarray(seg, BF16)

    half = MLA_ROPE // 2
    inv = ROPE_BASE ** (-jnp.arange(half, dtype=F32) / half)
    ang = jnp.arange(s_max, dtype=F32)[:, None] * inv[None, :]
    cos, sin = jnp.cos(ang), jnp.sin(ang)
    pad = jnp.zeros((s_max, SLOT - MLA_NOPE - MLA_ROPE), F32)
    cos_t = jnp.concatenate([jnp.ones((s_max, MLA_NOPE), F32), cos, cos, pad], axis=1)
    sin_t = jnp.concatenate([jnp.zeros((s_max, MLA_NOPE), F32), -sin, sin, pad], axis=1)

    w_route = jnp.zeros((d, LANES), F32).at[:, :N_EXPERTS].set(w_router)
    w_route = w_route.at[:, N_EXPERTS:N_EXPERTS + N_GROUPS].set(w_group)
    b_route = jnp.zeros((1, LANES), F32).at[0, :N_EXPERTS].set(b_router)
    b_route = b_route.at[0, N_EXPERTS:N_EXPERTS + N_GROUPS].set(b_group)

    return dict(
        g1=g_norm1.reshape(1, d), w_in=w_in_ext, w_gk=w_gk, b_gk=b_gk,
        g_qa=g_q_a.reshape(1, -1), w_q=w_q, g_kva=g_kv_a.reshape(1, -1), w_k=w_k, w_v=w_v,
        v_ones=v_ones, bd=bd, lanevecs=lanevecs, colvecs=lanevecs.T, cos=cos_t, sin=sin_t,
        g_gla=g_gla_out.reshape(1, -1), w_gla_o=w_gla_o.astype(BF16), w_mla_o=w_mla_o.astype(BF16),
        w_out=w_out.astype(BF16), g2=g_norm2.reshape(1, d), w_route=w_route, b_route=b_route,
        w_e_gate=w_e_gate.astype(BF16), w_e_up=w_e_up.astype(BF16), w_e_down=w_e_down.astype(BF16))


def _layer(x, c, w_ada, b_ada, wp):
    mod = _ada(c, w_ada, b_ada)
    q, k, v, zg, zga, zgm, lgf, lgb, qm, km, vm = _inproj(x, mod, wp)
    o_f, o_b = _gla(q, k, v, lgf, lgb)
    o_mla = _mla(qm, km, vm)
    x1, h2, route, counts = _merge(o_f, o_b, zg, zga, zgm, o_mla, x, mod, wp)
    return _moe(x1, h2, route, counts, mod, wp)


def kernel(x_prompt, x_sample, c_prompt, c_sample, w_ada, b_ada, g_norm1, w_in, w_gk_fwd, b_gk_fwd, w_gk_bwd, b_gk_bwd, g_gla_out, w_gla_o, g_q_a, w_q_b, g_kv_a, w_kv_b, g_q_nope, g_k_nope, g_q_rope, g_k_rope, w_mla_o, w_out, g_norm2, w_group, b_group, w_router, b_router, w_e_gate, w_e_up, w_e_down):
    assert w_ada.shape[0] == 1, "single-layer trunk"
    s_max = max(x_prompt.shape[1], x_sample.shape[1])
    wp = _prep_weights(s_max, *[p[0] for p in (
        g_norm1, w_in, w_gk_fwd, b_gk_fwd, w_gk_bwd, b_gk_bwd, g_gla_out, w_gla_o, g_q_a, w_q_b, g_kv_a,
        w_kv_b, g_q_nope, g_k_nope, g_q_rope, g_k_rope, w_mla_o, w_out, g_norm2, w_group, b_group,
        w_router, b_router, w_e_gate, w_e_up, w_e_down)])
    y_prompt = _layer(x_prompt, c_prompt, w_ada[0], b_ada[0], wp)
    y_sample = _layer(x_sample, c_sample, w_ada[0], b_ada[0], wp)
    return (y_prompt, y_sample)
```

```python
import functools
import math

import numpy as np
import jax
import jax.numpy as jnp
from jax import lax
from jax.experimental import pallas as pl
from jax.experimental.pallas import tpu as pltpu

F32 = jnp.float32
BF16 = jnp.bfloat16

D_MODEL = 1024
GLA_HEADS, GLA_DK, GLA_DV, GLA_RANK = 4, 128, 256, 16
GLA_GATE_NORM = 16.0
MLA_HEADS, MLA_Q_RANK, MLA_KV_RANK = 8, 256, 128
MLA_NOPE, MLA_ROPE, MLA_V = 64, 32, 64
ROPE_BASE = 10000.0
N_GROUPS, EXPERTS_PER_GROUP, TOP_K, D_EXPERT = 4, 8, 2, 512
N_EXPERTS = N_GROUPS * EXPERTS_PER_GROUP
NORM_EPS = 1e-6

LANES = 128
ROW_CHUNKS = D_MODEL // LANES
SLOT = LANES
VMEM_LIMIT = 56 * 1024 * 1024

ROW_TILE = 256
GLA_CHUNK = 64
GLA_BLOCK = 256
GLA_EXP_CLAMP = 80.0
MLA_TQ = 512
MLA_VROWS = MLA_V + 16
MLA_UNROLL = 2
MOE_TILE = 256
MISC_COLS = MLA_Q_RANK + MLA_KV_RANK + 3 * SLOT


def _cparams(sem):
    return pltpu.CompilerParams(dimension_semantics=sem, vmem_limit_bytes=VMEM_LIMIT)


def _dot(a, b):
    return jnp.dot(a, b, preferred_element_type=F32)


def _dot_nt(a, b):
    return lax.dot_general(a, b, (((1,), (1,)), ((), ())), preferred_element_type=F32)


def _split_bf16(x):
    hi = x.astype(BF16)
    lo = (x - hi.astype(F32)).astype(BF16)
    return hi, lo


def _sigmoid(x):
    return jax.nn.sigmoid(x)


def _silu(x):
    return x * _sigmoid(x)


def _ada_kernel(c_ref, w_ref, b_ref, o_ref):
    a = _silu(c_ref[...])
    ahi, alo = _split_bf16(a)
    w = w_ref[...]
    whi, wlo = _split_bf16(w)
    o_ref[...] = _dot(ahi, whi) + _dot(ahi, wlo) + _dot(alo, whi) + b_ref[...]


def _ada(c, w_ada, b_ada):
    nb = c.shape[0]
    cp = jnp.zeros((8, D_MODEL), F32).at[:nb].set(c)
    tn = 1536
    out = pl.pallas_call(
        _ada_kernel,
        out_shape=jax.ShapeDtypeStruct((8, 6 * D_MODEL), F32),
        grid=(6 * D_MODEL // tn,),
        in_specs=[pl.BlockSpec((8, D_MODEL), lambda j: (0, 0)),
                  pl.BlockSpec((D_MODEL, tn), lambda j: (0, j)),
                  pl.BlockSpec((1, tn), lambda j: (0, j))],
        out_specs=pl.BlockSpec((8, tn), lambda j: (0, j)),
        compiler_params=_cparams(("arbitrary",)),
        name="ada",
    )(cp, w_ada, b_ada.reshape(1, -1))
    mod = out[:nb].reshape(nb, 6, D_MODEL)
    return jnp.concatenate([mod, jnp.zeros((nb, 2, D_MODEL), F32)], axis=1)


_C_Q, _C_K, _C_V, _C_G, _C_GA, _C_GM = 0, 512, 1024, 2048, 3072, 4096
_C_MISC = 5120
IN_EXT = _C_MISC + MISC_COLS


def _segmean(xsq, bd):
    return _dot(xsq.astype(BF16), bd)


def _inproj_kernel(x_ref, mod_ref, g1_ref, win_ref, wgk_ref, bgk_ref, gqa_ref, wq_ref, gkva_ref,
                   wk_ref, wv_ref, vone_ref, bd_ref, lv_ref, cv_ref, cos_ref, sin_ref, cosc_ref, sinc_ref,
                   q_o, k_o, v_o, zg_o, zga_o, zgm_o, lgf_o, lgb_o, qm_o, km_o, vm_o, h_scr):
    x = x_ref[0]
    ms = jnp.mean(x * x, axis=-1, keepdims=True)
    h = (x * lax.rsqrt(ms + NORM_EPS) * g1_ref[...]) * (1.0 + mod_ref[0, 1:2, :]) + mod_ref[0, 0:1, :]
    h_scr[...] = h.astype(BF16)

    def proj(c0, n):
        return _dot(h_scr[...], win_ref[:, c0:c0 + n])

    q_o[0] = (proj(_C_Q, 512) * (GLA_DK ** -0.5)).astype(BF16)
    k_o[0] = proj(_C_K, 512).astype(BF16)
    v_o[0] = proj(_C_V, 1024).astype(BF16)
    zg_o[0] = proj(_C_G, 1024).astype(BF16)
    zga_o[0] = proj(_C_GA, 1024).astype(BF16)
    zgm_o[0] = proj(_C_GM, 1024).astype(BF16)
    misc = proj(_C_MISC, MISC_COLS)

    bd = bd_ref[...]
    cos_t = cos_ref[...]
    sin_t = sin_ref[...]
    g_kn, g_kr, g_krs = lv_ref[2:3, :], lv_ref[3:4, :], lv_ref[4:5, :]
    scale = (MLA_NOPE + MLA_ROPE) ** -0.5 * math.log2(math.e)

    zqa = misc[:, 0:MLA_Q_RANK]
    qa = zqa * lax.rsqrt(jnp.mean(zqa * zqa, axis=-1, keepdims=True) + NORM_EPS) * gqa_ref[...]
    qq = _dot_nt(wq_ref[...], qa.astype(BF16))
    cos_c, sin_c = cosc_ref[...], sinc_ref[...]
    gc_q, gc_qs = cv_ref[:, 0:1], cv_ref[:, 1:2]
    for hh in range(MLA_HEADS):
        xq = qq[hh * SLOT:(hh + 1) * SLOT, :]
        xs = qq[(MLA_HEADS + hh) * SLOT:(MLA_HEADS + hh + 1) * SLOT, :]
        r = lax.rsqrt(_dot(bd, (xq * xq).astype(BF16)) + NORM_EPS)
        qm = (xq * gc_q * cos_c + xs * gc_qs * sin_c) * (r * scale)
        qm_o[0, hh * SLOT:(hh + 1) * SLOT, :] = qm.astype(BF16)

    c0 = MLA_Q_RANK
    zkva = misc[:, c0:c0 + MLA_KV_RANK]
    kva = zkva * lax.rsqrt(jnp.mean(zkva * zkva, axis=-1, keepdims=True) + NORM_EPS) * gkva_ref[...]
    kvab = kva.astype(BF16)
    kk = _dot(kvab, wk_ref[...])
    vm_o[0, 0] = (_dot_nt(wv_ref[...], kvab) + vone_ref[...]).astype(BF16)
    c0 += MLA_KV_RANK
    kr = misc[:, c0:c0 + SLOT]
    krs = misc[:, c0 + SLOT:c0 + 2 * SLOT]
    r_kr = lax.rsqrt(_segmean(kr * kr, bd) + NORM_EPS)
    kpe = (kr * g_kr * cos_t + krs * g_krs * sin_t) * r_kr
    for hh in range(MLA_HEADS):
        xk = kk[:, hh * SLOT:(hh + 1) * SLOT]
        r = lax.rsqrt(_segmean(xk * xk, bd) + NORM_EPS)
        km_o[0, :, hh * SLOT:(hh + 1) * SLOT] = (xk * r * g_kn + kpe).astype(BF16)

    c0 += 2 * SLOT
    zgate = misc[:, c0:c0 + SLOT].astype(BF16)
    pre = _dot(zgate, wgk_ref[...]) + bgk_ref[...]
    lg = (jnp.minimum(pre, 0.0) - jnp.log(1.0 + jnp.exp(-jnp.abs(pre)))) * (1.0 / GLA_GATE_NORM)
    lgf_o[0] = lg[:, :GLA_HEADS * GLA_DK]
    lgb_o[0] = lg[:, GLA_HEADS * GLA_DK:]


def _inproj(x, mod, wp):
    nb, s, d = x.shape
    tm = min(ROW_TILE, s)
    const = lambda b, i: (0, 0)
    row = lambda b, i: (b, i, 0)

    def full(a):
        return pl.BlockSpec(a.shape, const)

    def out(n, dt):
        return jax.ShapeDtypeStruct((nb, s, n), dt), pl.BlockSpec((1, tm, n), row)

    outs = [out(512, BF16), out(512, BF16), out(1024, BF16), out(1024, BF16), out(1024, BF16),
            out(1024, BF16), out(512, F32), out(512, F32),
            (jax.ShapeDtypeStruct((nb, MLA_HEADS * SLOT, s), BF16),
             pl.BlockSpec((1, MLA_HEADS * SLOT, tm), lambda b, i: (b, 0, i))),
            out(MLA_HEADS * SLOT, BF16),
            (jax.ShapeDtypeStruct((nb, s // tm, MLA_HEADS * MLA_VROWS, tm), BF16),
             pl.BlockSpec((1, 1, MLA_HEADS * MLA_VROWS, tm), lambda b, i: (b, i, 0, 0)))]
    return pl.pallas_call(
        _inproj_kernel,
        out_shape=[o[0] for o in outs],
        grid=(nb, s // tm),
        in_specs=[pl.BlockSpec((1, tm, d), row),
                  pl.BlockSpec((1, 8, d), lambda b, i: (b, 0, 0)),
                  full(wp["g1"]), full(wp["w_in"]), full(wp["w_gk"]), full(wp["b_gk"]),
                  full(wp["g_qa"]), full(wp["w_q"]), full(wp["g_kva"]), full(wp["w_k"]), full(wp["w_v"]),
                  full(wp["v_ones"]), full(wp["bd"]), full(wp["lanevecs"]), full(wp["colvecs"]),
                  pl.BlockSpec((tm, SLOT), lambda b, i: (i, 0)),
                  pl.BlockSpec((tm, SLOT), lambda b, i: (i, 0)),
                  pl.BlockSpec((SLOT, tm), lambda b, i: (0, i)),
                  pl.BlockSpec((SLOT, tm), lambda b, i: (0, i))],
        out_specs=[o[1] for o in outs],
        scratch_shapes=[pltpu.VMEM((tm, d), BF16)],
        compiler_params=_cparams(("arbitrary", "arbitrary")),
        name="inproj",
    )(x, mod, wp["g1"], wp["w_in"], wp["w_gk"], wp["b_gk"], wp["g_qa"], wp["w_q"], wp["g_kva"],
      wp["w_k"], wp["w_v"], wp["v_ones"], wp["bd"], wp["lanevecs"], wp["colvecs"],
      wp["cos"][:s], wp["sin"][:s], wp["cos"][:s].T, wp["sin"][:s].T)


def _dot_tn(a, b):
    return lax.dot_general(a, b, (((0,), (0,)), ((), ())), preferred_element_type=F32)


def _gla_direction(q_ref, k_ref, v_ref, g_ref, o_ref, st_ref, scr, cum, mask, reverse):
    qin_s, qmid_s, kmid_s, kout_s, tot_s, u_s = scr
    c = GLA_CHUNK
    nch = q_ref.shape[1] // c
    rows = [slice(j * c, (j + 1) * c) for j in range(nch)]
    ks = [slice(h * GLA_DK, (h + 1) * GLA_DK) for h in range(GLA_HEADS)]
    vs = [slice(h * GLA_DV, (h + 1) * GLA_DV) for h in range(GLA_HEADS)]

    for j in range(nch):
        hi, lo = _split_bf16(g_ref[0, rows[j], :])
        b = _dot(cum, hi) + _dot(cum, lo)
        mid = b[c // 2:c // 2 + 1, :]
        tot = b[0:1, :] if reverse else b[c - 1:c, :]
        q = q_ref[0, rows[j], :].astype(F32)
        k = k_ref[0, rows[j], :].astype(F32)
        qin_s[rows[j], :] = (q * jnp.exp(b)).astype(BF16)
        qmid_s[rows[j], :] = (q * jnp.exp(jnp.minimum(b - mid, GLA_EXP_CLAMP))).astype(BF16)
        kmid_s[rows[j], :] = (k * jnp.exp(jnp.minimum(mid - b, GLA_EXP_CLAMP))).astype(BF16)
        kout_s[rows[j], :] = (k * jnp.exp(tot - b)).astype(BF16)
        tot_s[j:j + 1, :] = tot

    attn = {}
    for j in range(nch):
        for h in range(GLA_HEADS):
            a = _dot_nt(qmid_s[rows[j], ks[h]], kmid_s[rows[j], ks[h]])
            attn[j, h] = jnp.where(mask, a, 0.0).astype(BF16)

    for j in range(nch):
        for h in range(GLA_HEADS):
            v = v_ref[0, rows[j], vs[h]]
            o_ref[0, rows[j], vs[h]] = _dot(attn[j, h], v)
            u_s[j * GLA_HEADS + h] = _dot_tn(kout_s[rows[j], ks[h]], v)

    for j in (range(nch - 1, -1, -1) if reverse else range(nch)):
        for h in range(GLA_HEADS):
            st = st_ref[h]
            o_ref[0, rows[j], vs[h]] += _dot(qin_s[rows[j], ks[h]], st.astype(BF16))
            tot = jnp.broadcast_to(tot_s[j:j + 1, ks[h]], (GLA_DK, GLA_DK))
            dec = jnp.exp(tot.T)
            st_ref[h] = jnp.concatenate([dec] * (GLA_DV // GLA_DK), axis=1) * st + u_s[j * GLA_HEADS + h]


def _gla_kernel(qf, kf, vf, gf, qb, kb, vb, gb, of_ref, ob_ref, sf_ref, sb_ref, *scr):
    @pl.when(pl.program_id(1) == 0)
    def _():
        sf_ref[...] = jnp.zeros_like(sf_ref)
        sb_ref[...] = jnp.zeros_like(sb_ref)

    c = GLA_CHUNK
    ri = lax.broadcasted_iota(jnp.int32, (c, c), 0)
    ci = lax.broadcasted_iota(jnp.int32, (c, c), 1)
    cum_f = jnp.where(ri >= ci, 1.0, 0.0).astype(BF16)
    cum_b = jnp.where(ci >= ri, 1.0, 0.0).astype(BF16)
    n = len(scr) // 2
    _gla_direction(qf, kf, vf, gf, of_ref, sf_ref, scr[:n], cum_f, ri >= ci, False)
    _gla_direction(qb, kb, vb, gb, ob_ref, sb_ref, scr[n:], cum_b, ci > ri, True)


def _gla(q, k, v, lgf, lgb):
    nb, s, _ = q.shape
    cb = min(GLA_BLOCK, s)
    ns = s // cb
    nch = cb // GLA_CHUNK
    fwd = lambda b, i: (b, i, 0)
    bwd = lambda b, i: (b, ns - 1 - i, 0)
    hk, hv = GLA_HEADS * GLA_DK, GLA_HEADS * GLA_DV

    def specs(im):
        return [pl.BlockSpec((1, cb, hk), im), pl.BlockSpec((1, cb, hk), im),
                pl.BlockSpec((1, cb, hv), im), pl.BlockSpec((1, cb, hk), im)]

    per_dir = [pltpu.VMEM((cb, hk), BF16)] * 4 + [pltpu.VMEM((8, hk), F32),
                                                  pltpu.VMEM((nch * GLA_HEADS, GLA_DK, GLA_DV), F32)]
    return pl.pallas_call(
        _gla_kernel,
        out_shape=[jax.ShapeDtypeStruct((nb, s, hv), F32)] * 2,
        grid=(nb, ns),
        in_specs=specs(fwd) + specs(bwd),
        out_specs=[pl.BlockSpec((1, cb, hv), fwd), pl.BlockSpec((1, cb, hv), bwd)],
        scratch_shapes=[pltpu.VMEM((GLA_HEADS, GLA_DK, GLA_DV), F32)] * 2 + per_dir * 2,
        compiler_params=_cparams(("arbitrary", "arbitrary")),
        name="gla",
    )(q, k, v, lgf, q, k, v, lgb)


def _mla_kernel(q_ref, k_ref, vt_ref, o_ref, sc0, sc1, p0, p1):
    tq = q_ref.shape[2]
    nk = vt_ref.shape[1]
    tk = vt_ref.shape[3]
    gu = MLA_UNROLL
    ng = nk // gu
    sc, pp = (sc0, sc1), (p0, p1)
    qs = [q_ref[0, hh * SLOT:(hh + 1) * SLOT, :] for hh in range(2)]
    tiles = [(u, hh) for u in range(gu) for hh in range(2)]

    def score_tile(g, slot, u, hh):
        r0 = pl.multiple_of((g * gu + u) * tk, tk)
        kt = k_ref[0, pl.ds(r0, tk), hh * SLOT:(hh + 1) * SLOT]
        st = _dot(kt, qs[hh])
        sc[slot][2 * u + hh] = st.astype(BF16)
        return jnp.max(st, axis=0, keepdims=True).astype(BF16).astype(F32)

    def softmax_tile(slot, u, hh, m, tmax):
        m_new = jnp.maximum(m, tmax)
        pp[slot][2 * u + hh] = jnp.exp2(sc[slot][2 * u + hh] - m_new.astype(BF16))
        return m_new, jnp.exp2(m - m_new)

    def value_tile(g, slot, u, hh, alpha, acc):
        vt = vt_ref[0, g * gu + u, hh * MLA_VROWS:(hh + 1) * MLA_VROWS, :]
        return alpha * acc + _dot(vt, pp[slot][2 * u + hh])

    def issue_scores(g, slot):
        return tuple(score_tile(g, slot, u, hh) for u, hh in tiles)

    def softmax(slot, ms, tmax):
        ms, alphas = list(ms), []
        for u, hh in tiles:
            ms[hh], alpha = softmax_tile(slot, u, hh, ms[hh], tmax[2 * u + hh])
            alphas.append(alpha)
        return tuple(ms), tuple(alphas)

    def values(g, slot, alphas, accs):
        accs = list(accs)
        for u, hh in tiles:
            accs[hh] = value_tile(g, slot, u, hh, alphas[2 * u + hh], accs[hh])
        return tuple(accs)

    def step(g, slot, carry, more_scores):
        ms, alphas, accs, tmax = carry
        ms, accs, new_alphas, new_tmax = list(ms), list(accs), [], []
        for u, hh in tiles:
            idx = 2 * u + hh
            accs[hh] = value_tile(g - 1, 1 - slot, u, hh, alphas[idx], accs[hh])
            ms[hh], alpha = softmax_tile(slot, u, hh, ms[hh], tmax[idx])
            new_alphas.append(alpha)
            if more_scores:
                new_tmax.append(score_tile(g + 1, 1 - slot, u, hh))
        return tuple(ms), tuple(new_alphas), tuple(accs), tuple(new_tmax)

    def double_step(i, carry):
        g = 2 * i + 1
        carry = step(g, 1, carry, True)
        return step(g + 1, 0, carry, True)

    m0 = jnp.full((1, tq), -jnp.inf, F32)
    acc0 = jnp.zeros((MLA_VROWS, tq), F32)
    tmax = issue_scores(0, 0)
    ms, alphas = softmax(0, (m0, m0), tmax)
    tmax = issue_scores(1, 1)
    carry = (ms, alphas, (acc0, acc0), tmax)
    for i in range((ng - 2) // 2):
        carry = double_step(i, carry)
    ms, alphas, accs, _ = step(ng - 1, 1, carry, False)
    accs = values(ng - 1, 1, alphas, accs)
    outs = [(a[:MLA_V] / a[MLA_V:MLA_V + 1]).T for a in accs]
    o_ref[0] = jnp.concatenate(outs, axis=-1).astype(BF16)


def _mla(qm, km, vmt):
    nb, s, _ = km.shape
    nk, tk = vmt.shape[1], vmt.shape[3]
    tq = min(MLA_TQ, s)
    assert (nk // MLA_UNROLL) % 2 == 0 and nk % MLA_UNROLL == 0
    slot_tiles = 2 * MLA_UNROLL
    return pl.pallas_call(
        _mla_kernel,
        out_shape=jax.ShapeDtypeStruct((nb, s, MLA_HEADS * MLA_V), BF16),
        grid=(nb, MLA_HEADS // 2, s // tq),
        in_specs=[pl.BlockSpec((1, 2 * SLOT, tq), lambda b, p, i: (b, p, i)),
                  pl.BlockSpec((1, s, 2 * SLOT), lambda b, p, i: (b, 0, p)),
                  pl.BlockSpec((1, nk, 2 * MLA_VROWS, tk), lambda b, p, i: (b, 0, p, 0))],
        out_specs=pl.BlockSpec((1, tq, 2 * MLA_V), lambda b, p, i: (b, i, p)),
        scratch_shapes=[pltpu.VMEM((slot_tiles, tk, tq), BF16)] * 4,
        compiler_params=_cparams(("arbitrary", "arbitrary", "arbitrary")),
        name="mla",
    )(qm, km, vmt)


def _merge_kernel(of_ref, ob_ref, zg_ref, zga_ref, zgm_ref, om_ref, x_ref, mod_ref, ggla_ref, wglo_ref,
                  wmo_ref, wout_ref, g2_ref, wr_ref, br_ref, ltri_ref,
                  x1_o, h2_o, route_o, cnt_o, cnt_scr):
    first = (pl.program_id(0) == 0) & (pl.program_id(1) == 0)

    @pl.when(first)
    def _():
        cnt_scr[...] = jnp.zeros_like(cnt_scr)

    tm = x_ref.shape[1]
    o = of_ref[0] + ob_ref[0]
    zg = zg_ref[0].astype(F32)
    parts = []
    for hh in range(GLA_HEADS):
        seg = o[:, hh * GLA_DV:(hh + 1) * GLA_DV]
        r = lax.rsqrt(jnp.mean(seg * seg, axis=-1, keepdims=True) + NORM_EPS)
        parts.append((seg * r * ggla_ref[...]) * _silu(zg[:, hh * GLA_DV:(hh + 1) * GLA_DV]))
    og = jnp.concatenate(parts, axis=-1).astype(BF16)
    y_gla = _dot(og, wglo_ref[...])
    y_mla = _dot(om_ref[0], wmo_ref[...])
    merged = _sigmoid(zga_ref[0].astype(F32)) * y_gla + _sigmoid(zgm_ref[0].astype(F32)) * y_mla
    mix = _dot(merged.astype(BF16), wout_ref[...])
    x1 = x_ref[0] + mod_ref[0, 2:3, :] * mix
    x1_o[0] = x1
    ms = jnp.mean(x1 * x1, axis=-1, keepdims=True)
    h2 = (x1 * lax.rsqrt(ms + NORM_EPS) * g2_ref[...]) * (1.0 + mod_ref[0, 4:5, :]) + mod_ref[0, 3:4, :]
    for c in range(ROW_CHUNKS):
        h2_o[pl.ds(c, tm, stride=ROW_CHUNKS), :] = h2[:, c * LANES:(c + 1) * LANES]

    hhi, hlo = _split_bf16(h2)
    w = wr_ref[...]
    whi, wlo = _split_bf16(w)
    logits = _dot(hhi, whi) + _dot(hhi, wlo) + _dot(hlo, whi) + br_ref[...]
    lane = lax.broadcasted_iota(jnp.int32, (tm, LANES), 1)
    neg = -jnp.inf
    is_g = (lane >= N_EXPERTS) & (lane < N_EXPERTS + N_GROUPS)
    lgm = jnp.where(is_g, logits, neg)
    mg = jnp.max(lgm, axis=-1, keepdims=True)
    sg = jnp.sum(jnp.exp(lgm - mg), axis=-1, keepdims=True)
    p_top = 1.0 / sg
    g_idx = jnp.min(jnp.where(lgm == mg, lane, 2 * LANES), axis=-1, keepdims=True) - N_EXPERTS
    in_grp = (lane >= g_idx * EXPERTS_PER_GROUP) & (lane < (g_idx + 1) * EXPERTS_PER_GROUP)
    el = jnp.where(in_grp, logits, neg)
    m1 = jnp.max(el, axis=-1, keepdims=True)
    e1 = jnp.min(jnp.where(el == m1, lane, 2 * LANES), axis=-1, keepdims=True)
    el2 = jnp.where(lane == e1, neg, el)
    m2 = jnp.max(el2, axis=-1, keepdims=True)
    e2 = jnp.min(jnp.where(el2 == m2, lane, 2 * LANES), axis=-1, keepdims=True)
    se = jnp.sum(jnp.exp(el - m1), axis=-1, keepdims=True)
    pv1 = 1.0 / se
    pv2 = jnp.exp(m2 - m1) / se
    w1 = p_top * pv1 / (pv1 + pv2)
    w2 = p_top * pv2 / (pv1 + pv2)

    oh1 = lane == e1
    oh2 = lane == e2
    oh = jnp.where(oh1 | oh2, 1.0, 0.0)
    prefix = _dot(ltri_ref[...], oh.astype(BF16)) + cnt_scr[...]
    r1 = jnp.sum(jnp.where(oh1, prefix, 0.0), axis=-1, keepdims=True)
    r2 = jnp.sum(jnp.where(oh2, prefix, 0.0), axis=-1, keepdims=True)
    cnt_scr[...] = cnt_scr[...] + jnp.sum(oh, axis=0, keepdims=True)
    cnt_o[...] = jnp.broadcast_to(cnt_scr[...], cnt_o.shape)

    route = jnp.where(lane == 0, e1.astype(F32), 0.0)
    route = jnp.where(lane == 1, e2.astype(F32), route)
    route = jnp.where(lane == 2, r1, route)
    route = jnp.where(lane == 3, r2, route)
    route = jnp.where(lane == 4, w1, route)
    route = jnp.where(lane == 5, w2, route)
    route_o[0] = route


def _merge(o_f, o_b, zg, zga, zgm, o_mla, x, mod, wp):
    nb, s, d = x.shape
    tm = min(ROW_TILE, s)
    const = lambda b, i: (0, 0)
    row = lambda b, i: (b, i, 0)

    def full(a):
        return pl.BlockSpec(a.shape, const)

    def rows(n):
        return pl.BlockSpec((1, tm, n), row)

    ltri = jnp.asarray(np.tril(np.ones((tm, tm), np.float32), -1), BF16)
    return pl.pallas_call(
        _merge_kernel,
        out_shape=[jax.ShapeDtypeStruct((nb, s, d), F32), jax.ShapeDtypeStruct((nb * s * ROW_CHUNKS, LANES), F32),
                   jax.ShapeDtypeStruct((nb, s, LANES), F32), jax.ShapeDtypeStruct((8, LANES), F32)],
        grid=(nb, s // tm),
        in_specs=[rows(1024), rows(1024), rows(1024), rows(1024), rows(1024), rows(MLA_HEADS * MLA_V),
                  rows(d), pl.BlockSpec((1, 8, d), lambda b, i: (b, 0, 0)),
                  full(wp["g_gla"]), full(wp["w_gla_o"]), full(wp["w_mla_o"]), full(wp["w_out"]),
                  full(wp["g2"]), full(wp["w_route"]), full(wp["b_route"]), full(ltri)],
        out_specs=[rows(d), pl.BlockSpec((tm * ROW_CHUNKS, LANES), lambda b, i: (b * (s // tm) + i, 0)),
                   rows(LANES), pl.BlockSpec((8, LANES), const)],
        scratch_shapes=[pltpu.VMEM((1, LANES), F32)],
        compiler_params=_cparams(("arbitrary", "arbitrary")),
        name="merge",
    )(o_f, o_b, zg, zga, zgm, o_mla, x, mod, wp["g_gla"], wp["w_gla_o"], wp["w_mla_o"], wp["w_out"],
      wp["g2"], wp["w_route"], wp["b_route"], ltri)


def _dispatch_kernel(dest_ref, h_ref, xs_hbm, xs_out, sem):
    del xs_hbm
    tt = h_ref.shape[0] // ROW_CHUNKS

    def copy(r, slot):
        d0 = pl.multiple_of(dest_ref[0, 0, 2 * r + slot] * ROW_CHUNKS, ROW_CHUNKS)
        return pltpu.make_async_copy(h_ref.at[pl.ds(pl.multiple_of(r * ROW_CHUNKS, ROW_CHUNKS), ROW_CHUNKS)],
                                     xs_out.at[pl.ds(d0, ROW_CHUNKS)], sem.at[0])

    def start(r, c):
        copy(r, 0).start(priority=0)
        copy(r, 1).start(priority=1)
        return c

    def wait(r, c):
        copy(r, 0).wait()
        copy(r, 1).wait()
        return c

    lax.fori_loop(0, tt, start, 0, unroll=8)
    lax.fori_loop(0, tt, wait, 0, unroll=8)


def _dispatch(h2, dest, n_rows):
    t = h2.shape[0] // ROW_CHUNKS
    tt = min(ROW_TILE, t)
    nt = t // tt
    xs0 = jnp.zeros((n_rows * ROW_CHUNKS, LANES), F32)
    return pl.pallas_call(
        _dispatch_kernel,
        out_shape=jax.ShapeDtypeStruct((n_rows * ROW_CHUNKS, LANES), F32),
        grid=(nt,),
        in_specs=[pl.BlockSpec((1, 1, 2 * tt), lambda i: (i, 0, 0), memory_space=pltpu.SMEM),
                  pl.BlockSpec((tt * ROW_CHUNKS, LANES), lambda i: (i, 0)),
                  pl.BlockSpec(memory_space=pl.ANY)],
        out_specs=pl.BlockSpec(memory_space=pl.ANY),
        scratch_shapes=[pltpu.SemaphoreType.DMA((1,))],
        input_output_aliases={2: 0},
        compiler_params=_cparams(("arbitrary",)),
        name="dispatch",
    )(dest.reshape(nt, 1, 2 * tt), h2, xs0)


def _expert_kernel(te_ref, na_ref, xs_ref, wg_ref, wu_ref, wd_ref, ys_ref, xb_scr):
    active = pl.program_id(0) < na_ref[0]

    @pl.when(active)
    def _():
        for c in range(ROW_CHUNKS):
            xb_scr[:, c * LANES:(c + 1) * LANES] = xs_ref[pl.ds(c, MOE_TILE, stride=ROW_CHUNKS), :].astype(BF16)
        xb = xb_scr[...]
        a = _silu(_dot(xb, wg_ref[0].astype(BF16))) * _dot(xb, wu_ref[0].astype(BF16))
        y = _dot(a.astype(BF16), wd_ref[0].astype(BF16))
        for c in range(ROW_CHUNKS):
            ys_ref[pl.ds(c, MOE_TILE, stride=ROW_CHUNKS), :] = y[:, c * LANES:(c + 1) * LANES]

    @pl.when(jnp.logical_not(active))
    def _():
        ys_ref[...] = jnp.zeros_like(ys_ref)


def _experts(xs, tile_expert, n_active, wp):
    d = D_MODEL
    n_rows = xs.shape[0] // ROW_CHUNKS
    nt = n_rows // MOE_TILE
    rows = MOE_TILE * ROW_CHUNKS

    def tile(i, te, na):
        return (jnp.minimum(i, na[0] - 1), 0)

    def wmap(i, te, na):
        return (te[jnp.minimum(i, na[0] - 1)], 0, 0)

    return pl.pallas_call(
        _expert_kernel,
        out_shape=jax.ShapeDtypeStruct((n_rows * ROW_CHUNKS, LANES), F32),
        grid_spec=pltpu.PrefetchScalarGridSpec(
            num_scalar_prefetch=2,
            grid=(nt,),
            in_specs=[pl.BlockSpec((rows, LANES), tile),
                      pl.BlockSpec((1, d, D_EXPERT), wmap),
                      pl.BlockSpec((1, d, D_EXPERT), wmap),
                      pl.BlockSpec((1, D_EXPERT, d), wmap)],
            out_specs=pl.BlockSpec((rows, LANES), lambda i, te, na: (i, 0)),
            scratch_shapes=[pltpu.VMEM((MOE_TILE, d), BF16)]),
        compiler_params=_cparams(("arbitrary",)),
        name="experts",
    )(tile_expert, n_active, xs, wp["w_e_gate"], wp["w_e_up"], wp["w_e_down"])


def _combine_kernel(dest_ref, x1_ref, rw_ref, ga_ref, ys_hbm, o_ref, buf, sem):
    tt = x1_ref.shape[0]
    i = pl.program_id(0)
    nt = pl.num_programs(0) - 1
    par = lax.rem(i, 2)

    def copy(p, r, slot, d0):
        return pltpu.make_async_copy(ys_hbm.at[pl.ds(d0, ROW_CHUNKS)],
                                     buf.at[p, slot, pl.ds(pl.multiple_of(r * ROW_CHUNKS, ROW_CHUNKS), ROW_CHUNKS)],
                                     sem.at[p])

    @pl.when(i < nt)
    def _():
        def start(r, c):
            for slot in range(TOP_K):
                d0 = pl.multiple_of(dest_ref[0, 0, 2 * r + slot] * ROW_CHUNKS, ROW_CHUNKS)
                copy(par, r, slot, d0).start(priority=slot)
            return c

        lax.fori_loop(0, tt, start, 0, unroll=8)

    @pl.when(i > 0)
    def _():
        def wait(r, c):
            for slot in range(TOP_K):
                copy(1 - par, r, slot, 0).wait()
            return c

        lax.fori_loop(0, tt, wait, 0, unroll=8)
        rw = rw_ref[...]
        for c in range(ROW_CHUNKS):
            cs = slice(c * LANES, (c + 1) * LANES)
            rows = pl.ds(c, tt, stride=ROW_CHUNKS)
            moe = buf[1 - par, 0, rows, :] * rw[:, 4:5] + buf[1 - par, 1, rows, :] * rw[:, 5:6]
            o_ref[:, cs] = x1_ref[:, cs] + ga_ref[0, 5:6, cs] * moe


def _combine(x1, route, mod, ys, dest, s):
    t, d = x1.shape
    tt = min(ROW_TILE, s)
    nt = t // tt
    per_b = s // tt
    lag = lambda i: jnp.maximum(i - 1, 0)
    return pl.pallas_call(
        _combine_kernel,
        out_shape=jax.ShapeDtypeStruct((t, d), F32),
        grid=(nt + 1,),
        in_specs=[pl.BlockSpec((1, 1, 2 * tt), lambda i: (jnp.minimum(i, nt - 1), 0, 0), memory_space=pltpu.SMEM),
                  pl.BlockSpec((tt, d), lambda i: (lag(i), 0)),
                  pl.BlockSpec((tt, LANES), lambda i: (lag(i), 0)),
                  pl.BlockSpec((1, 8, d), lambda i: (lag(i) // per_b, 0, 0)),
                  pl.BlockSpec(memory_space=pl.ANY)],
        out_specs=pl.BlockSpec((tt, d), lambda i: (lag(i), 0)),
        scratch_shapes=[pltpu.VMEM((2, TOP_K, tt * ROW_CHUNKS, LANES), F32), pltpu.SemaphoreType.DMA((2,))],
        compiler_params=_cparams(("arbitrary",)),
        name="combine",
    )(dest.reshape(nt, 1, 2 * tt), x1, route, mod, ys)


def _moe(x1, h2, route, counts, mod, wp):
    nb, s, d = x1.shape
    t = nb * s
    route2 = route.reshape(t, LANES)
    e_flat = route2[:, 0:2].astype(jnp.int32)
    rank = route2[:, 2:4].astype(jnp.int32)
    cnt = counts[0, :N_EXPERTS].astype(jnp.int32)
    padded = (cnt + MOE_TILE - 1) // MOE_TILE * MOE_TILE
    pad_end = jnp.cumsum(padded)
    pad_start = pad_end - padded
    dest = (pad_start[e_flat] + rank).reshape(t * TOP_K)
    n_rows = t * TOP_K + N_EXPERTS * MOE_TILE
    nt = n_rows // MOE_TILE
    tile_row = jnp.arange(nt, dtype=jnp.int32) * MOE_TILE
    tile_expert = jnp.minimum(jnp.sum((pad_end[None, :] <= tile_row[:, None]).astype(jnp.int32), axis=1),
                              N_EXPERTS - 1)
    n_active = (pad_end[-1:] // MOE_TILE).astype(jnp.int32)
    xs = _dispatch(h2, dest, n_rows)
    ys = _experts(xs, tile_expert, n_active, wp)
    out = _combine(x1.reshape(t, d), route2, mod, ys, dest, s)
    return out.reshape(nb, s, d)


def _swap_halves(a, axis=-1):
    n = a.shape[axis] // 2
    lo, hi = jnp.split(a, [n], axis=axis)
    return jnp.concatenate([hi, lo], axis=axis)


def _slot(parts, n_lead):
    out = jnp.zeros((n_lead, SLOT), F32)
    for off, a in parts:
        out = out.at[:, off:off + a.shape[-1]].set(a)
    return out


def _prep_weights(s_max, g_norm1, w_in, w_gk_fwd, b_gk_fwd, w_gk_bwd, b_gk_bwd, g_gla_out, w_gla_o, g_q_a,
                  w_q_b, g_kv_a, w_kv_b, g_q_nope, g_k_nope, g_q_rope, g_k_rope, w_mla_o, w_out, g_norm2,
                  w_group, b_group, w_router, b_router, w_e_gate, w_e_up, w_e_down):
    d = D_MODEL
    hk, hv = GLA_HEADS * GLA_DK, GLA_HEADS * GLA_DV
    offs = np.cumsum([0, hk, hk, hv, hv, GLA_RANK, GLA_RANK, MLA_Q_RANK, MLA_KV_RANK, MLA_ROPE, d, d])
    cols = [w_in[:, offs[i]:offs[i + 1]] for i in range(11)]
    wq_, wk_, wv_, wg_, wgf, wgb, wqa, wkva, wkr, wga, wgm = cols
    kr_slot = _slot([(MLA_NOPE, wkr)], d)
    krs_slot = _slot([(MLA_NOPE, _swap_halves(wkr))], d)
    gate_slot = _slot([(0, wgf), (GLA_RANK, wgb)], d)
    w_in_ext = jnp.concatenate([wq_, wk_, wv_, wg_, wga, wgm, wqa, wkva, kr_slot, krs_slot, gate_slot],
                               axis=1).astype(BF16)
    w_gk = jnp.zeros((SLOT, 2 * hk), F32)
    w_gk = w_gk.at[0:GLA_RANK, 0:hk].set(w_gk_fwd).at[GLA_RANK:2 * GLA_RANK, hk:].set(w_gk_bwd).astype(BF16)
    b_gk = jnp.concatenate([b_gk_fwd, b_gk_bwd]).reshape(1, -1)

    wq3 = w_q_b.reshape(MLA_Q_RANK, MLA_HEADS, MLA_NOPE + MLA_ROPE)
    zq = jnp.zeros((MLA_Q_RANK, MLA_HEADS, SLOT - MLA_NOPE - MLA_ROPE), F32)
    main = jnp.concatenate([wq3, zq], axis=-1)
    swp = jnp.concatenate([jnp.zeros((MLA_Q_RANK, MLA_HEADS, MLA_NOPE), F32),
                           _swap_halves(wq3[..., MLA_NOPE:]), zq], axis=-1)
    w_q = jnp.concatenate([main.reshape(MLA_Q_RANK, -1), swp.reshape(MLA_Q_RANK, -1)], axis=1).T.astype(BF16)
    wkv3 = w_kv_b.reshape(MLA_KV_RANK, MLA_HEADS, MLA_NOPE + MLA_V)
    w_k = jnp.concatenate([wkv3[..., :MLA_NOPE], jnp.zeros((MLA_KV_RANK, MLA_HEADS, SLOT - MLA_NOPE), F32)],
                          axis=-1).reshape(MLA_KV_RANK, -1).astype(BF16)
    w_v = jnp.concatenate([wkv3[..., MLA_NOPE:], jnp.zeros((MLA_KV_RANK, MLA_HEADS, MLA_VROWS - MLA_V), F32)],
                          axis=-1).reshape(MLA_KV_RANK, -1).T.astype(BF16)
    v_ones = jnp.zeros((MLA_HEADS, MLA_VROWS), F32).at[:, MLA_V].set(1.0).reshape(-1, 1)

    lanevecs = jnp.concatenate([
        _slot([(0, g_q_nope[None]), (MLA_NOPE, g_q_rope[None])], 1),
        _slot([(MLA_NOPE, _swap_halves(g_q_rope)[None])], 1),
        _slot([(0, g_k_nope[None])], 1),
        _slot([(MLA_NOPE, g_k_rope[None])], 1),
        _slot([(MLA_NOPE, _swap_halves(g_k_rope)[None])], 1),
        jnp.zeros((3, SLOT), F32)], axis=0)
    seg = np.zeros((SLOT, SLOT), np.float32)
    seg[:MLA_NOPE, :MLA_NOPE] = 1.0 / MLA_NOPE
    seg[MLA_NOPE:MLA_NOPE + MLA_ROPE, MLA_NOPE:MLA_NOPE + MLA_ROPE] = 1.0 / MLA_ROPE
    bd = jnp.asarray(seg, BF16)

    half = MLA_ROPE // 2
    inv = ROPE_BASE ** (-jnp.arange(half, dtype=F32) / half)
    ang = jnp.arange(s_max, dtype=F32)[:, None] * inv[None, :]
    cos, sin = jnp.cos(ang), jnp.sin(ang)
    pad = jnp.zeros((s_max, SLOT - MLA_NOPE - MLA_ROPE), F32)
    cos_t = jnp.concatenate([jnp.ones((s_max, MLA_NOPE), F32), cos, cos, pad], axis=1)
    sin_t = jnp.concatenate([jnp.zeros((s_max, MLA_NOPE), F32), -sin, sin, pad], axis=1)

    w_route = jnp.zeros((d, LANES), F32).at[:, :N_EXPERTS].set(w_router)
    w_route = w_route.at[:, N_EXPERTS:N_EXPERTS + N_GROUPS].set(w_group)
    b_route = jnp.zeros((1, LANES), F32).at[0, :N_EXPERTS].set(b_router)
    b_route = b_route.at[0, N_EXPERTS:N_EXPERTS + N_GROUPS].set(b_group)

    return dict(
        g1=g_norm1.reshape(1, d), w_in=w_in_ext, w_gk=w_gk, b_gk=b_gk,
        g_qa=g_q_a.reshape(1, -1), w_q=w_q, g_kva=g_kv_a.reshape(1, -1), w_k=w_k, w_v=w_v,
        v_ones=v_ones, bd=bd, lanevecs=lanevecs, colvecs=lanevecs.T, cos=cos_t, sin=sin_t,
        g_gla=g_gla_out.reshape(1, -1), w_gla_o=w_gla_o.astype(BF16), w_mla_o=w_mla_o.astype(BF16),
        w_out=w_out.astype(BF16), g2=g_norm2.reshape(1, d), w_route=w_route, b_route=b_route,
        w_e_gate=w_e_gate, w_e_up=w_e_up, w_e_down=w_e_down)


def _layer(x, c, w_ada, b_ada, wp):
    mod = _ada(c, w_ada, b_ada)
    q, k, v, zg, zga, zgm, lgf, lgb, qm, km, vm = _inproj(x, mod, wp)
    o_f, o_b = _gla(q, k, v, lgf, lgb)
    o_mla = _mla(qm, km, vm)
    x1, h2, route, counts = _merge(o_f, o_b, zg, zga, zgm, o_mla, x, mod, wp)
    return _moe(x1, h2, route, counts, mod, wp)


def kernel(x_prompt, x_sample, c_prompt, c_sample, w_ada, b_ada, g_norm1, w_in, w_gk_fwd, b_gk_fwd, w_gk_bwd, b_gk_bwd, g_gla_out, w_gla_o, g_q_a, w_q_b, g_kv_a, w_kv_b, g_q_nope, g_k_nope, g_q_rope, g_k_rope, w_mla_o, w_out, g_norm2, w_group, b_group, w_router, b_router, w_e_gate, w_e_up, w_e_down):
    assert w_ada.shape[0] == 1, "single-layer trunk"
    s_max = max(x_prompt.shape[1], x_sample.shape[1])
    wp = _prep_weights(s_max, *[p[0] for p in (
        g_norm1, w_in, w_gk_fwd, b_gk_fwd, w_gk_bwd, b_gk_bwd, g_gla_out, w_gla_o, g_q_a, w_q_b, g_kv_a,
        w_kv_b, g_q_nope, g_k_nope, g_q_rope, g_k_rope, w_mla_o, w_out, g_norm2, w_group, b_group,
        w_router, b_router, w_e_gate, w_e_up, w_e_down)])
    y_prompt = _layer(x_prompt, c_prompt, w_ada[0], b_ada[0], wp)
    y_sample = _layer(x_sample, c_sample, w_ada[0], b_ada[0], wp)
    return (y_prompt, y_sample)
```

```python
import functools
import math

import numpy as np
import jax
import jax.numpy as jnp
from jax import lax
from jax.experimental import pallas as pl
from jax.experimental.pallas import tpu as pltpu

F32 = jnp.float32
BF16 = jnp.bfloat16

D_MODEL = 1024
GLA_HEADS, GLA_DK, GLA_DV, GLA_RANK = 4, 128, 256, 16
GLA_GATE_NORM = 16.0
MLA_HEADS, MLA_Q_RANK, MLA_KV_RANK = 8, 256, 128
MLA_NOPE, MLA_ROPE, MLA_V = 64, 32, 64
ROPE_BASE = 10000.0
N_GROUPS, EXPERTS_PER_GROUP, TOP_K, D_EXPERT = 4, 8, 2, 512
N_EXPERTS = N_GROUPS * EXPERTS_PER_GROUP
NORM_EPS = 1e-6

LANES = 128
ROW_CHUNKS = D_MODEL // LANES
SLOT = LANES
VMEM_LIMIT = 56 * 1024 * 1024

ROW_TILE = 256
GLA_CHUNK = 64
GLA_BLOCK = 256
GLA_EXP_CLAMP = 80.0
MLA_TQ = 512
MLA_VROWS = MLA_V + 16
MLA_UNROLL = 2
MOE_TILE = 256
MISC_COLS = MLA_Q_RANK + MLA_KV_RANK + 3 * SLOT


def _cparams(sem):
    return pltpu.CompilerParams(dimension_semantics=sem, vmem_limit_bytes=VMEM_LIMIT)


def _dot(a, b):
    return jnp.dot(a, b, preferred_element_type=F32)


def _dot_nt(a, b):
    return lax.dot_general(a, b, (((1,), (1,)), ((), ())), preferred_element_type=F32)


def _split_bf16(x):
    hi = x.astype(BF16)
    lo = (x - hi.astype(F32)).astype(BF16)
    return hi, lo


def _sigmoid(x):
    return jax.nn.sigmoid(x)


def _silu(x):
    return x * _sigmoid(x)


def _ada_kernel(c_ref, w_ref, b_ref, o_ref):
    a = _silu(c_ref[...])
    ahi, alo = _split_bf16(a)
    w = w_ref[...]
    whi, wlo = _split_bf16(w)
    o_ref[...] = _dot(ahi, whi) + _dot(ahi, wlo) + _dot(alo, whi) + b_ref[...]


def _ada(c, w_ada, b_ada):
    nb = c.shape[0]
    cp = jnp.zeros((8, D_MODEL), F32).at[:nb].set(c)
    tn = 1536
    out = pl.pallas_call(
        _ada_kernel,
        out_shape=jax.ShapeDtypeStruct((8, 6 * D_MODEL), F32),
        grid=(6 * D_MODEL // tn,),
        in_specs=[pl.BlockSpec((8, D_MODEL), lambda j: (0, 0)),
                  pl.BlockSpec((D_MODEL, tn), lambda j: (0, j)),
                  pl.BlockSpec((1, tn), lambda j: (0, j))],
        out_specs=pl.BlockSpec((8, tn), lambda j: (0, j)),
        compiler_params=_cparams(("arbitrary",)),
        name="ada",
    )(cp, w_ada, b_ada.reshape(1, -1))
    mod = out[:nb].reshape(nb, 6, D_MODEL)
    return jnp.concatenate([mod, jnp.zeros((nb, 2, D_MODEL), F32)], axis=1)


_C_Q, _C_K, _C_V, _C_G, _C_GA, _C_GM = 0, 512, 1024, 2048, 3072, 4096
_C_MISC = 5120
IN_EXT = _C_MISC + MISC_COLS


def _segmean(xsq, bd):
    return _dot(xsq.astype(BF16), bd)


def _inproj_kernel(x_ref, mod_ref, g1_ref, win_ref, wgk_ref, bgk_ref, gqa_ref, wq_ref, gkva_ref,
                   wk_ref, wv_ref, vone_ref, bd_ref, lv_ref, cv_ref, cos_ref, sin_ref, cosc_ref, sinc_ref,
                   q_o, k_o, v_o, zg_o, zga_o, zgm_o, lgf_o, lgb_o, qm_o, km_o, vm_o, h_scr):
    x = x_ref[0]
    ms = jnp.mean(x * x, axis=-1, keepdims=True)
    h = (x * lax.rsqrt(ms + NORM_EPS) * g1_ref[...]) * (1.0 + mod_ref[0, 1:2, :]) + mod_ref[0, 0:1, :]
    h_scr[...] = h.astype(BF16)

    def proj(c0, n):
        return _dot(h_scr[...], win_ref[:, c0:c0 + n])

    q_o[0] = (proj(_C_Q, 512) * (GLA_DK ** -0.5)).astype(BF16)
    k_o[0] = proj(_C_K, 512).astype(BF16)
    v_o[0] = proj(_C_V, 1024).astype(BF16)
    zg_o[0] = proj(_C_G, 1024).astype(BF16)
    zga_o[0] = proj(_C_GA, 1024).astype(BF16)
    zgm_o[0] = proj(_C_GM, 1024).astype(BF16)
    misc = proj(_C_MISC, MISC_COLS)

    bd = bd_ref[...]
    cos_t = cos_ref[...]
    sin_t = sin_ref[...]
    g_kn, g_kr, g_krs = lv_ref[2:3, :], lv_ref[3:4, :], lv_ref[4:5, :]
    scale = (MLA_NOPE + MLA_ROPE) ** -0.5 * math.log2(math.e)

    zqa = misc[:, 0:MLA_Q_RANK]
    qa = zqa * lax.rsqrt(jnp.mean(zqa * zqa, axis=-1, keepdims=True) + NORM_EPS) * gqa_ref[...]
    qq = _dot_nt(wq_ref[...], qa.astype(BF16))
    cos_c, sin_c = cosc_ref[...], sinc_ref[...]
    gc_q, gc_qs = cv_ref[:, 0:1], cv_ref[:, 1:2]
    for hh in range(MLA_HEADS):
        xq = qq[hh * SLOT:(hh + 1) * SLOT, :]
        xs = qq[(MLA_HEADS + hh) * SLOT:(MLA_HEADS + hh + 1) * SLOT, :]
        r = lax.rsqrt(_dot(bd, (xq * xq).astype(BF16)) + NORM_EPS)
        qm = (xq * gc_q * cos_c + xs * gc_qs * sin_c) * (r * scale)
        qm_o[0, hh * SLOT:(hh + 1) * SLOT, :] = qm.astype(BF16)

    c0 = MLA_Q_RANK
    zkva = misc[:, c0:c0 + MLA_KV_RANK]
    kva = zkva * lax.rsqrt(jnp.mean(zkva * zkva, axis=-1, keepdims=True) + NORM_EPS) * gkva_ref[...]
    kvab = kva.astype(BF16)
    kk = _dot(kvab, wk_ref[...])
    vm_o[0, 0] = (_dot_nt(wv_ref[...], kvab) + vone_ref[...]).astype(BF16)
    c0 += MLA_KV_RANK
    kr = misc[:, c0:c0 + SLOT]
    krs = misc[:, c0 + SLOT:c0 + 2 * SLOT]
    r_kr = lax.rsqrt(_segmean(kr * kr, bd) + NORM_EPS)
    kpe = (kr * g_kr * cos_t + krs * g_krs * sin_t) * r_kr
    for hh in range(MLA_HEADS):
        xk = kk[:, hh * SLOT:(hh + 1) * SLOT]
        r = lax.rsqrt(_segmean(xk * xk, bd) + NORM_EPS)
        km_o[0, :, hh * SLOT:(hh + 1) * SLOT] = (xk * r * g_kn + kpe).astype(BF16)

    c0 += 2 * SLOT
    zgate = misc[:, c0:c0 + SLOT].astype(BF16)
    pre = _dot(zgate, wgk_ref[...]) + bgk_ref[...]
    lg = (jnp.minimum(pre, 0.0) - jnp.log(1.0 + jnp.exp(-jnp.abs(pre)))) * (1.0 / GLA_GATE_NORM)
    lgf_o[0] = lg[:, :GLA_HEADS * GLA_DK]
    lgb_o[0] = lg[:, GLA_HEADS * GLA_DK:]


def _inproj(x, mod, wp):
    nb, s, d = x.shape
    tm = min(ROW_TILE, s)
    const = lambda b, i: (0, 0)
    row = lambda b, i: (b, i, 0)

    def full(a):
        return pl.BlockSpec(a.shape, const)

    def out(n, dt):
        return jax.ShapeDtypeStruct((nb, s, n), dt), pl.BlockSpec((1, tm, n), row)

    outs = [out(512, BF16), out(512, BF16), out(1024, BF16), out(1024, BF16), out(1024, BF16),
            out(1024, BF16), out(512, F32), out(512, F32),
            (jax.ShapeDtypeStruct((nb, MLA_HEADS * SLOT, s), BF16),
             pl.BlockSpec((1, MLA_HEADS * SLOT, tm), lambda b, i: (b, 0, i))),
            out(MLA_HEADS * SLOT, BF16),
            (jax.ShapeDtypeStruct((nb, s // tm, MLA_HEADS * MLA_VROWS, tm), BF16),
             pl.BlockSpec((1, 1, MLA_HEADS * MLA_VROWS, tm), lambda b, i: (b, i, 0, 0)))]
    return pl.pallas_call(
        _inproj_kernel,
        out_shape=[o[0] for o in outs],
        grid=(nb, s // tm),
        in_specs=[pl.BlockSpec((1, tm, d), row),
                  pl.BlockSpec((1, 8, d), lambda b, i: (b, 0, 0)),
                  full(wp["g1"]), full(wp["w_in"]), full(wp["w_gk"]), full(wp["b_gk"]),
                  full(wp["g_qa"]), full(wp["w_q"]), full(wp["g_kva"]), full(wp["w_k"]), full(wp["w_v"]),
                  full(wp["v_ones"]), full(wp["bd"]), full(wp["lanevecs"]), full(wp["colvecs"]),
                  pl.BlockSpec((tm, SLOT), lambda b, i: (i, 0)),
                  pl.BlockSpec((tm, SLOT), lambda b, i: (i, 0)),
                  pl.BlockSpec((SLOT, tm), lambda b, i: (0, i)),
                  pl.BlockSpec((SLOT, tm), lambda b, i: (0, i))],
        out_specs=[o[1] for o in outs],
        scratch_shapes=[pltpu.VMEM((tm, d), BF16)],
        compiler_params=_cparams(("arbitrary", "arbitrary")),
        name="inproj",
    )(x, mod, wp["g1"], wp["w_in"], wp["w_gk"], wp["b_gk"], wp["g_qa"], wp["w_q"], wp["g_kva"],
      wp["w_k"], wp["w_v"], wp["v_ones"], wp["bd"], wp["lanevecs"], wp["colvecs"],
      wp["cos"][:s], wp["sin"][:s], wp["cos"][:s].T, wp["sin"][:s].T)


def _dot_tn(a, b):
    return lax.dot_general(a, b, (((0,), (0,)), ((), ())), preferred_element_type=F32)


def _gla_direction(q_ref, k_ref, v_ref, g_ref, o_ref, st_ref, scr, cum, mask, reverse):
    qin_s, qmid_s, kmid_s, kout_s, tot_s, u_s = scr
    c = GLA_CHUNK
    nch = q_ref.shape[1] // c
    rows = [slice(j * c, (j + 1) * c) for j in range(nch)]
    ks = [slice(h * GLA_DK, (h + 1) * GLA_DK) for h in range(GLA_HEADS)]
    vs = [slice(h * GLA_DV, (h + 1) * GLA_DV) for h in range(GLA_HEADS)]

    for j in range(nch):
        hi, lo = _split_bf16(g_ref[0, rows[j], :])
        b = _dot(cum, hi) + _dot(cum, lo)
        mid = b[c // 2:c // 2 + 1, :]
        tot = b[0:1, :] if reverse else b[c - 1:c, :]
        q = q_ref[0, rows[j], :].astype(F32)
        k = k_ref[0, rows[j], :].astype(F32)
        qin_s[rows[j], :] = (q * jnp.exp(b)).astype(BF16)
        qmid_s[rows[j], :] = (q * jnp.exp(jnp.minimum(b - mid, GLA_EXP_CLAMP))).astype(BF16)
        kmid_s[rows[j], :] = (k * jnp.exp(jnp.minimum(mid - b, GLA_EXP_CLAMP))).astype(BF16)
        kout_s[rows[j], :] = (k * jnp.exp(tot - b)).astype(BF16)
        tot_s[j:j + 1, :] = tot

    attn = {}
    for j in range(nch):
        for h in range(GLA_HEADS):
            a = _dot_nt(qmid_s[rows[j], ks[h]], kmid_s[rows[j], ks[h]])
            attn[j, h] = jnp.where(mask, a, 0.0).astype(BF16)

    for j in range(nch):
        for h in range(GLA_HEADS):
            v = v_ref[0, rows[j], vs[h]]
            o_ref[0, rows[j], vs[h]] = _dot(attn[j, h], v)
            u_s[j * GLA_HEADS + h] = _dot_tn(kout_s[rows[j], ks[h]], v)

    for j in (range(nch - 1, -1, -1) if reverse else range(nch)):
        for h in range(GLA_HEADS):
            st = st_ref[h]
            o_ref[0, rows[j], vs[h]] += _dot(qin_s[rows[j], ks[h]], st.astype(BF16))
            tot = jnp.broadcast_to(tot_s[j:j + 1, ks[h]], (GLA_DK, GLA_DK))
            dec = jnp.exp(tot.T)
            st_ref[h] = jnp.concatenate([dec] * (GLA_DV // GLA_DK), axis=1) * st + u_s[j * GLA_HEADS + h]


def _gla_kernel(qf, kf, vf, gf, qb, kb, vb, gb, of_ref, ob_ref, sf_ref, sb_ref, *scr):
    @pl.when(pl.program_id(1) == 0)
    def _():
        sf_ref[...] = jnp.zeros_like(sf_ref)
        sb_ref[...] = jnp.zeros_like(sb_ref)

    c = GLA_CHUNK
    ri = lax.broadcasted_iota(jnp.int32, (c, c), 0)
    ci = lax.broadcasted_iota(jnp.int32, (c, c), 1)
    cum_f = jnp.where(ri >= ci, 1.0, 0.0).astype(BF16)
    cum_b = jnp.where(ci >= ri, 1.0, 0.0).astype(BF16)
    n = len(scr) // 2
    _gla_direction(qf, kf, vf, gf, of_ref, sf_ref, scr[:n], cum_f, ri >= ci, False)
    _gla_direction(qb, kb, vb, gb, ob_ref, sb_ref, scr[n:], cum_b, ci > ri, True)


def _gla(q, k, v, lgf, lgb):
    nb, s, _ = q.shape
    cb = min(GLA_BLOCK, s)
    ns = s // cb
    nch = cb // GLA_CHUNK
    fwd = lambda b, i: (b, i, 0)
    bwd = lambda b, i: (b, ns - 1 - i, 0)
    hk, hv = GLA_HEADS * GLA_DK, GLA_HEADS * GLA_DV

    def specs(im):
        return [pl.BlockSpec((1, cb, hk), im), pl.BlockSpec((1, cb, hk), im),
                pl.BlockSpec((1, cb, hv), im), pl.BlockSpec((1, cb, hk), im)]

    per_dir = [pltpu.VMEM((cb, hk), BF16)] * 4 + [pltpu.VMEM((8, hk), F32),
                                                  pltpu.VMEM((nch * GLA_HEADS, GLA_DK, GLA_DV), F32)]
    return pl.pallas_call(
        _gla_kernel,
        out_shape=[jax.ShapeDtypeStruct((nb, s, hv), F32)] * 2,
        grid=(nb, ns),
        in_specs=specs(fwd) + specs(bwd),
        out_specs=[pl.BlockSpec((1, cb, hv), fwd), pl.BlockSpec((1, cb, hv), bwd)],
        scratch_shapes=[pltpu.VMEM((GLA_HEADS, GLA_DK, GLA_DV), F32)] * 2 + per_dir * 2,
        compiler_params=_cparams(("arbitrary", "arbitrary")),
        name="gla",
    )(q, k, v, lgf, q, k, v, lgb)


def _mla_kernel(q_ref, k_ref, vt_ref, o_ref, sc0, sc1, p0, p1):
    tq = q_ref.shape[2]
    nk = vt_ref.shape[1]
    tk = vt_ref.shape[3]
    gu = MLA_UNROLL
    ng = nk // gu
    sc, pp = (sc0, sc1), (p0, p1)
    qs = [q_ref[0, hh * SLOT:(hh + 1) * SLOT, :] for hh in range(2)]
    tiles = [(u, hh) for u in range(gu) for hh in range(2)]

    def score_tile(g, slot, u, hh):
        r0 = pl.multiple_of((g * gu + u) * tk, tk)
        kt = k_ref[0, pl.ds(r0, tk), hh * SLOT:(hh + 1) * SLOT]
        st = _dot(kt, qs[hh])
        sc[slot][2 * u + hh] = st
        return jnp.max(st, axis=0, keepdims=True)

    def softmax_tile(slot, u, hh, m, tmax):
        m_new = jnp.maximum(m, tmax)
        pp[slot][2 * u + hh] = jnp.exp2(sc[slot][2 * u + hh] - m_new).astype(BF16)
        return m_new, jnp.exp2(m - m_new)

    def value_tile(g, slot, u, hh, alpha, acc):
        vt = vt_ref[0, g * gu + u, hh * MLA_VROWS:(hh + 1) * MLA_VROWS, :]
        return alpha * acc + _dot(vt, pp[slot][2 * u + hh])

    def issue_scores(g, slot):
        return tuple(score_tile(g, slot, u, hh) for u, hh in tiles)

    def softmax(slot, ms, tmax):
        ms, alphas = list(ms), []
        for u, hh in tiles:
            ms[hh], alpha = softmax_tile(slot, u, hh, ms[hh], tmax[2 * u + hh])
            alphas.append(alpha)
        return tuple(ms), tuple(alphas)

    def values(g, slot, alphas, accs):
        accs = list(accs)
        for u, hh in tiles:
            accs[hh] = value_tile(g, slot, u, hh, alphas[2 * u + hh], accs[hh])
        return tuple(accs)

    def step(g, slot, carry, more_scores):
        ms, alphas, accs, tmax = carry
        ms, accs, new_alphas, new_tmax = list(ms), list(accs), [], []
        for u, hh in tiles:
            idx = 2 * u + hh
            accs[hh] = value_tile(g - 1, 1 - slot, u, hh, alphas[idx], accs[hh])
            ms[hh], alpha = softmax_tile(slot, u, hh, ms[hh], tmax[idx])
            new_alphas.append(alpha)
            if more_scores:
                new_tmax.append(score_tile(g + 1, 1 - slot, u, hh))
        return tuple(ms), tuple(new_alphas), tuple(accs), tuple(new_tmax)

    def double_step(i, carry):
        g = 2 * i + 1
        carry = step(g, 1, carry, True)
        return step(g + 1, 0, carry, True)

    m0 = jnp.full((1, tq), -jnp.inf, F32)
    acc0 = jnp.zeros((MLA_VROWS, tq), F32)
    tmax = issue_scores(0, 0)
    ms, alphas = softmax(0, (m0, m0), tmax)
    tmax = issue_scores(1, 1)
    carry = (ms, alphas, (acc0, acc0), tmax)
    for i in range((ng - 2) // 2):
        carry = double_step(i, carry)
    ms, alphas, accs, _ = step(ng - 1, 1, carry, False)
    accs = values(ng - 1, 1, alphas, accs)
    outs = [(a[:MLA_V] / a[MLA_V:MLA_V + 1]).T for a in accs]
    o_ref[0] = jnp.concatenate(outs, axis=-1).astype(BF16)


def _mla(qm, km, vmt):
    nb, s, _ = km.shape
    nk, tk = vmt.shape[1], vmt.shape[3]
    tq = min(MLA_TQ, s)
    assert (nk // MLA_UNROLL) % 2 == 0 and nk % MLA_UNROLL == 0
    slot_tiles = 2 * MLA_UNROLL
    return pl.pallas_call(
        _mla_kernel,
        out_shape=jax.ShapeDtypeStruct((nb, s, MLA_HEADS * MLA_V), BF16),
        grid=(nb, MLA_HEADS // 2, s // tq),
        in_specs=[pl.BlockSpec((1, 2 * SLOT, tq), lambda b, p, i: (b, p, i)),
                  pl.BlockSpec((1, s, 2 * SLOT), lambda b, p, i: (b, 0, p)),
                  pl.BlockSpec((1, nk, 2 * MLA_VROWS, tk), lambda b, p, i: (b, 0, p, 0))],
        out_specs=pl.BlockSpec((1, tq, 2 * MLA_V), lambda b, p, i: (b, i, p)),
        scratch_shapes=[pltpu.VMEM((slot_tiles, tk, tq), F32)] * 2 + [pltpu.VMEM((slot_tiles, tk, tq), BF16)] * 2,
        compiler_params=_cparams(("arbitrary", "arbitrary", "arbitrary")),
        name="mla",
    )(qm, km, vmt)


def _merge_kernel(of_ref, ob_ref, zg_ref, zga_ref, zgm_ref, om_ref, x_ref, mod_ref, ggla_ref, wglo_ref,
                  wmo_ref, wout_ref, g2_ref, wr_ref, br_ref, ltri_ref,
                  x1_o, h2_o, route_o, cnt_o, cnt_scr):
    first = (pl.program_id(0) == 0) & (pl.program_id(1) == 0)

    @pl.when(first)
    def _():
        cnt_scr[...] = jnp.zeros_like(cnt_scr)

    tm = x_ref.shape[1]
    o = of_ref[0] + ob_ref[0]
    zg = zg_ref[0].astype(F32)
    parts = []
    for hh in range(GLA_HEADS):
        seg = o[:, hh * GLA_DV:(hh + 1) * GLA_DV]
        r = lax.rsqrt(jnp.mean(seg * seg, axis=-1, keepdims=True) + NORM_EPS)
        parts.append((seg * r * ggla_ref[...]) * _silu(zg[:, hh * GLA_DV:(hh + 1) * GLA_DV]))
    og = jnp.concatenate(parts, axis=-1).astype(BF16)
    y_gla = _dot(og, wglo_ref[...])
    y_mla = _dot(om_ref[0], wmo_ref[...])
    merged = _sigmoid(zga_ref[0].astype(F32)) * y_gla + _sigmoid(zgm_ref[0].astype(F32)) * y_mla
    mix = _dot(merged.astype(BF16), wout_ref[...])
    x1 = x_ref[0] + mod_ref[0, 2:3, :] * mix
    x1_o[0] = x1
    ms = jnp.mean(x1 * x1, axis=-1, keepdims=True)
    h2 = (x1 * lax.rsqrt(ms + NORM_EPS) * g2_ref[...]) * (1.0 + mod_ref[0, 4:5, :]) + mod_ref[0, 3:4, :]
    for c in range(ROW_CHUNKS):
        h2_o[pl.ds(c, tm, stride=ROW_CHUNKS), :] = h2[:, c * LANES:(c + 1) * LANES]

    hhi, hlo = _split_bf16(h2)
    w = wr_ref[...]
    whi, wlo = _split_bf16(w)
    logits = _dot(hhi, whi) + _dot(hhi, wlo) + _dot(hlo, whi) + br_ref[...]
    lane = lax.broadcasted_iota(jnp.int32, (tm, LANES), 1)
    neg = -jnp.inf
    is_g = (lane >= N_EXPERTS) & (lane < N_EXPERTS + N_GROUPS)
    lgm = jnp.where(is_g, logits, neg)
    mg = jnp.max(lgm, axis=-1, keepdims=True)
    sg = jnp.sum(jnp.exp(lgm - mg), axis=-1, keepdims=True)
    p_top = 1.0 / sg
    g_idx = jnp.min(jnp.where(lgm == mg, lane, 2 * LANES), axis=-1, keepdims=True) - N_EXPERTS
    in_grp = (lane >= g_idx * EXPERTS_PER_GROUP) & (lane < (g_idx + 1) * EXPERTS_PER_GROUP)
    el = jnp.where(in_grp, logits, neg)
    m1 = jnp.max(el, axis=-1, keepdims=True)
    e1 = jnp.min(jnp.where(el == m1, lane, 2 * LANES), axis=-1, keepdims=True)
    el2 = jnp.where(lane == e1, neg, el)
    m2 = jnp.max(el2, axis=-1, keepdims=True)
    e2 = jnp.min(jnp.where(el2 == m2, lane, 2 * LANES), axis=-1, keepdims=True)
    se = jnp.sum(jnp.exp(el - m1), axis=-1, keepdims=True)
    pv1 = 1.0 / se
    pv2 = jnp.exp(m2 - m1) / se
    w1 = p_top * pv1 / (pv1 + pv2)
    w2 = p_top * pv2 / (pv1 + pv2)

    oh1 = lane == e1
    oh2 = lane == e2
    oh = jnp.where(oh1 | oh2, 1.0, 0.0)
    prefix = _dot(ltri_ref[...], oh.astype(BF16)) + cnt_scr[...]
    r1 = jnp.sum(jnp.where(oh1, prefix, 0.0), axis=-1, keepdims=True)
    r2 = jnp.sum(jnp.where(oh2, prefix, 0.0), axis=-1, keepdims=True)
    cnt_scr[...] = cnt_scr[...] + jnp.sum(oh, axis=0, keepdims=True)
    cnt_o[...] = jnp.broadcast_to(cnt_scr[...], cnt_o.shape)

    route = jnp.where(lane == 0, e1.astype(F32), 0.0)
    route = jnp.where(lane == 1, e2.astype(F32), route)
    route = jnp.where(lane == 2, r1, route)
    route = jnp.where(lane == 3, r2, route)
    route = jnp.where(lane == 4, w1, route)
    route = jnp.where(lane == 5, w2, route)
    route_o[0] = route


def _merge(o_f, o_b, zg, zga, zgm, o_mla, x, mod, wp):
    nb, s, d = x.shape
    tm = min(ROW_TILE, s)
    const = lambda b, i: (0, 0)
    row = lambda b, i: (b, i, 0)

    def full(a):
        return pl.BlockSpec(a.shape, const)

    def rows(n):
        return pl.BlockSpec((1, tm, n), row)

    ltri = jnp.asarray(np.tril(np.ones((tm, tm), np.float32), -1), BF16)
    return pl.pallas_call(
        _merge_kernel,
        out_shape=[jax.ShapeDtypeStruct((nb, s, d), F32), jax.ShapeDtypeStruct((nb * s * ROW_CHUNKS, LANES), F32),
                   jax.ShapeDtypeStruct((nb, s, LANES), F32), jax.ShapeDtypeStruct((8, LANES), F32)],
        grid=(nb, s // tm),
        in_specs=[rows(1024), rows(1024), rows(1024), rows(1024), rows(1024), rows(MLA_HEADS * MLA_V),
                  rows(d), pl.BlockSpec((1, 8, d), lambda b, i: (b, 0, 0)),
                  full(wp["g_gla"]), full(wp["w_gla_o"]), full(wp["w_mla_o"]), full(wp["w_out"]),
                  full(wp["g2"]), full(wp["w_route"]), full(wp["b_route"]), full(ltri)],
        out_specs=[rows(d), pl.BlockSpec((tm * ROW_CHUNKS, LANES), lambda b, i: (b * (s // tm) + i, 0)),
                   rows(LANES), pl.BlockSpec((8, LANES), const)],
        scratch_shapes=[pltpu.VMEM((1, LANES), F32)],
        compiler_params=_cparams(("arbitrary", "arbitrary")),
        name="merge",
    )(o_f, o_b, zg, zga, zgm, o_mla, x, mod, wp["g_gla"], wp["w_gla_o"], wp["w_mla_o"], wp["w_out"],
      wp["g2"], wp["w_route"], wp["b_route"], ltri)


def _dispatch_kernel(seg_ref, dest_ref, h_ref, xs_out, zero_scr, sem):
    tt = h_ref.shape[0] // ROW_CHUNKS
    tile_rows = MOE_TILE * ROW_CHUNKS

    @pl.when(pl.program_id(0) == 0)
    def _():
        zero_scr[...] = jnp.zeros_like(zero_scr)

        def zero_copy(e):
            r0 = pl.multiple_of((seg_ref[0, e] - MOE_TILE) * ROW_CHUNKS, tile_rows)
            return pltpu.make_async_copy(zero_scr, xs_out.at[pl.ds(r0, tile_rows)], sem.at[1])

        for e in range(N_EXPERTS):
            @pl.when(seg_ref[1, e] > 0)
            def _():
                zero_copy(e).start()
        for e in range(N_EXPERTS):
            @pl.when(seg_ref[1, e] > 0)
            def _():
                zero_copy(e).wait()

        def tail_copy(ti):
            return pltpu.make_async_copy(zero_scr, xs_out.at[pl.ds(pl.multiple_of(ti * tile_rows, tile_rows), tile_rows)],
                                         sem.at[1])

        first_tail = seg_ref[0, N_EXPERTS - 1] // MOE_TILE
        n_tiles = xs_out.shape[0] // tile_rows
        lax.fori_loop(first_tail, n_tiles, lambda ti, c: (tail_copy(ti).start(), c)[1], 0)
        lax.fori_loop(first_tail, n_tiles, lambda ti, c: (tail_copy(ti).wait(), c)[1], 0)

    def copy(r, slot):
        d0 = pl.multiple_of(dest_ref[0, 0, 2 * r + slot] * ROW_CHUNKS, ROW_CHUNKS)
        return pltpu.make_async_copy(h_ref.at[pl.ds(pl.multiple_of(r * ROW_CHUNKS, ROW_CHUNKS), ROW_CHUNKS)],
                                     xs_out.at[pl.ds(d0, ROW_CHUNKS)], sem.at[0])

    def start(r, c):
        copy(r, 0).start(priority=0)
        copy(r, 1).start(priority=1)
        return c

    def wait(r, c):
        copy(r, 0).wait()
        copy(r, 1).wait()
        return c

    lax.fori_loop(0, tt, start, 0, unroll=8)
    lax.fori_loop(0, tt, wait, 0, unroll=8)


def _dispatch(h2, dest, segments, n_rows):
    t = h2.shape[0] // ROW_CHUNKS
    tt = min(ROW_TILE, t)
    nt = t // tt
    return pl.pallas_call(
        _dispatch_kernel,
        out_shape=jax.ShapeDtypeStruct((n_rows * ROW_CHUNKS, LANES), F32),
        grid_spec=pltpu.PrefetchScalarGridSpec(
            num_scalar_prefetch=1,
            grid=(nt,),
            in_specs=[pl.BlockSpec((1, 1, 2 * tt), lambda i, seg: (i, 0, 0), memory_space=pltpu.SMEM),
                      pl.BlockSpec((tt * ROW_CHUNKS, LANES), lambda i, seg: (i, 0))],
            out_specs=pl.BlockSpec(memory_space=pl.ANY),
            scratch_shapes=[pltpu.VMEM((MOE_TILE * ROW_CHUNKS, LANES), F32), pltpu.SemaphoreType.DMA((2,))]),
        compiler_params=_cparams(("arbitrary",)),
        name="dispatch",
    )(segments, dest.reshape(nt, 1, 2 * tt), h2)


def _expert_kernel(te_ref, na_ref, xs_ref, wg_ref, wu_ref, wd_ref, ys_ref, xb_scr):
    active = pl.program_id(0) < na_ref[0]

    @pl.when(active)
    def _():
        for c in range(ROW_CHUNKS):
            xb_scr[:, c * LANES:(c + 1) * LANES] = xs_ref[pl.ds(c, MOE_TILE, stride=ROW_CHUNKS), :].astype(BF16)
        xb = xb_scr[...]
        a = _silu(_dot(xb, wg_ref[0].astype(BF16))) * _dot(xb, wu_ref[0].astype(BF16))
        y = _dot(a.astype(BF16), wd_ref[0].astype(BF16))
        for c in range(ROW_CHUNKS):
            ys_ref[pl.ds(c, MOE_TILE, stride=ROW_CHUNKS), :] = y[:, c * LANES:(c + 1) * LANES]

    @pl.when(jnp.logical_not(active))
    def _():
        ys_ref[...] = jnp.zeros_like(ys_ref)


def _experts(xs, tile_expert, n_active, wp):
    d = D_MODEL
    n_rows = xs.shape[0] // ROW_CHUNKS
    nt = n_rows // MOE_TILE
    rows = MOE_TILE * ROW_CHUNKS

    def tile(i, te, na):
        return (jnp.minimum(i, na[0] - 1), 0)

    def wmap(i, te, na):
        return (te[jnp.minimum(i, na[0] - 1)], 0, 0)

    return pl.pallas_call(
        _expert_kernel,
        out_shape=jax.ShapeDtypeStruct((n_rows * ROW_CHUNKS, LANES), F32),
        grid_spec=pltpu.PrefetchScalarGridSpec(
            num_scalar_prefetch=2,
            grid=(nt,),
            in_specs=[pl.BlockSpec((rows, LANES), tile),
                      pl.BlockSpec((1, d, D_EXPERT), wmap),
                      pl.BlockSpec((1, d, D_EXPERT), wmap),
                      pl.BlockSpec((1, D_EXPERT, d), wmap)],
            out_specs=pl.BlockSpec((rows, LANES), lambda i, te, na: (i, 0)),
            scratch_shapes=[pltpu.VMEM((MOE_TILE, d), BF16)]),
        compiler_params=_cparams(("arbitrary",)),
        name="experts",
    )(tile_expert, n_active, xs, wp["w_e_gate"], wp["w_e_up"], wp["w_e_down"])


def _combine_kernel(dest_ref, x1_ref, rw_ref, ga_ref, ys_hbm, o_ref, buf, sem):
    tt = x1_ref.shape[0]
    i = pl.program_id(0)
    nt = pl.num_programs(0) - 1
    par = lax.rem(i, 2)

    def copy(p, r, slot, d0):
        return pltpu.make_async_copy(ys_hbm.at[pl.ds(d0, ROW_CHUNKS)],
                                     buf.at[p, slot, pl.ds(pl.multiple_of(r * ROW_CHUNKS, ROW_CHUNKS), ROW_CHUNKS)],
                                     sem.at[p])

    @pl.when(i < nt)
    def _():
        def start(r, c):
            for slot in range(TOP_K):
                d0 = pl.multiple_of(dest_ref[0, 0, 2 * r + slot] * ROW_CHUNKS, ROW_CHUNKS)
                copy(par, r, slot, d0).start(priority=slot)
            return c

        lax.fori_loop(0, tt, start, 0, unroll=8)

    @pl.when(i > 0)
    def _():
        def wait(r, c):
            for slot in range(TOP_K):
                copy(1 - par, r, slot, 0).wait()
            return c

        lax.fori_loop(0, tt, wait, 0, unroll=8)
        rw = rw_ref[...]
        for c in range(ROW_CHUNKS):
            cs = slice(c * LANES, (c + 1) * LANES)
            rows = pl.ds(c, tt, stride=ROW_CHUNKS)
            moe = buf[1 - par, 0, rows, :] * rw[:, 4:5] + buf[1 - par, 1, rows, :] * rw[:, 5:6]
            o_ref[:, cs] = x1_ref[:, cs] + ga_ref[0, 5:6, cs] * moe


def _combine(x1, route, mod, ys, dest, s):
    t, d = x1.shape
    tt = min(ROW_TILE, s)
    nt = t // tt
    per_b = s // tt
    lag = lambda i: jnp.maximum(i - 1, 0)
    return pl.pallas_call(
        _combine_kernel,
        out_shape=jax.ShapeDtypeStruct((t, d), F32),
        grid=(nt + 1,),
        in_specs=[pl.BlockSpec((1, 1, 2 * tt), lambda i: (jnp.minimum(i, nt - 1), 0, 0), memory_space=pltpu.SMEM),
                  pl.BlockSpec((tt, d), lambda i: (lag(i), 0)),
                  pl.BlockSpec((tt, LANES), lambda i: (lag(i), 0)),
                  pl.BlockSpec((1, 8, d), lambda i: (lag(i) // per_b, 0, 0)),
                  pl.BlockSpec(memory_space=pl.ANY)],
        out_specs=pl.BlockSpec((tt, d), lambda i: (lag(i), 0)),
        scratch_shapes=[pltpu.VMEM((2, TOP_K, tt * ROW_CHUNKS, LANES), F32), pltpu.SemaphoreType.DMA((2,))],
        compiler_params=_cparams(("arbitrary",)),
        name="combine",
    )(dest.reshape(nt, 1, 2 * tt), x1, route, mod, ys)


def _moe(x1, h2, route, counts, mod, wp):
    nb, s, d = x1.shape
    t = nb * s
    route2 = route.reshape(t, LANES)
    e_flat = route2[:, 0:2].astype(jnp.int32)
    rank = route2[:, 2:4].astype(jnp.int32)
    cnt = counts[0, :N_EXPERTS].astype(jnp.int32)
    padded = (cnt + MOE_TILE - 1) // MOE_TILE * MOE_TILE
    pad_end = jnp.cumsum(padded)
    pad_start = pad_end - padded
    dest = (pad_start[e_flat] + rank).reshape(t * TOP_K)
    n_rows = t * TOP_K + N_EXPERTS * MOE_TILE
    nt = n_rows // MOE_TILE
    tile_row = jnp.arange(nt, dtype=jnp.int32) * MOE_TILE
    tile_expert = jnp.minimum(jnp.sum((pad_end[None, :] <= tile_row[:, None]).astype(jnp.int32), axis=1),
                              N_EXPERTS - 1)
    n_active = (pad_end[-1:] // MOE_TILE).astype(jnp.int32)
    xs = _dispatch(h2, dest, jnp.stack([pad_end, padded]).astype(jnp.int32), n_rows)
    ys = _experts(xs, tile_expert, n_active, wp)
    out = _combine(x1.reshape(t, d), route2, mod, ys, dest, s)
    return out.reshape(nb, s, d)


def _swap_halves(a, axis=-1):
    n = a.shape[axis] // 2
    lo, hi = jnp.split(a, [n], axis=axis)
    return jnp.concatenate([hi, lo], axis=axis)


def _slot(parts, n_lead):
    out = jnp.zeros((n_lead, SLOT), F32)
    for off, a in parts:
        out = out.at[:, off:off + a.shape[-1]].set(a)
    return out


def _prep_weights(s_max, g_norm1, w_in, w_gk_fwd, b_gk_fwd, w_gk_bwd, b_gk_bwd, g_gla_out, w_gla_o, g_q_a,
                  w_q_b, g_kv_a, w_kv_b, g_q_nope, g_k_nope, g_q_rope, g_k_rope, w_mla_o, w_out, g_norm2,
                  w_group, b_group, w_router, b_router, w_e_gate, w_e_up, w_e_down):
    d = D_MODEL
    hk, hv = GLA_HEADS * GLA_DK, GLA_HEADS * GLA_DV
    offs = np.cumsum([0, hk, hk, hv, hv, GLA_RANK, GLA_RANK, MLA_Q_RANK, MLA_KV_RANK, MLA_ROPE, d, d])
    cols = [w_in[:, offs[i]:offs[i + 1]] for i in range(11)]
    wq_, wk_, wv_, wg_, wgf, wgb, wqa, wkva, wkr, wga, wgm = cols
    kr_slot = _slot([(MLA_NOPE, wkr)], d)
    krs_slot = _slot([(MLA_NOPE, _swap_halves(wkr))], d)
    gate_slot = _slot([(0, wgf), (GLA_RANK, wgb)], d)
    w_in_ext = jnp.concatenate([wq_, wk_, wv_, wg_, wga, wgm, wqa, wkva, kr_slot, krs_slot, gate_slot],
                               axis=1).astype(BF16)
    w_gk = jnp.zeros((SLOT, 2 * hk), F32)
    w_gk = w_gk.at[0:GLA_RANK, 0:hk].set(w_gk_fwd).at[GLA_RANK:2 * GLA_RANK, hk:].set(w_gk_bwd).astype(BF16)
    b_gk = jnp.concatenate([b_gk_fwd, b_gk_bwd]).reshape(1, -1)

    wq3 = w_q_b.reshape(MLA_Q_RANK, MLA_HEADS, MLA_NOPE + MLA_ROPE)
    zq = jnp.zeros((MLA_Q_RANK, MLA_HEADS, SLOT - MLA_NOPE - MLA_ROPE), F32)
    main = jnp.concatenate([wq3, zq], axis=-1)
    swp = jnp.concatenate([jnp.zeros((MLA_Q_RANK, MLA_HEADS, MLA_NOPE), F32),
                           _swap_halves(wq3[..., MLA_NOPE:]), zq], axis=-1)
    w_q = jnp.concatenate([main.reshape(MLA_Q_RANK, -1), swp.reshape(MLA_Q_RANK, -1)], axis=1).T.astype(BF16)
    wkv3 = w_kv_b.reshape(MLA_KV_RANK, MLA_HEADS, MLA_NOPE + MLA_V)
    w_k = jnp.concatenate([wkv3[..., :MLA_NOPE], jnp.zeros((MLA_KV_RANK, MLA_HEADS, SLOT - MLA_NOPE), F32)],
                          axis=-1).reshape(MLA_KV_RANK, -1).astype(BF16)
    w_v = jnp.concatenate([wkv3[..., MLA_NOPE:], jnp.zeros((MLA_KV_RANK, MLA_HEADS, MLA_VROWS - MLA_V), F32)],
                          axis=-1).reshape(MLA_KV_RANK, -1).T.astype(BF16)
    v_ones = jnp.zeros((MLA_HEADS, MLA_VROWS), F32).at[:, MLA_V].set(1.0).reshape(-1, 1)

    lanevecs = jnp.concatenate([
        _slot([(0, g_q_nope[None]), (MLA_NOPE, g_q_rope[None])], 1),
        _slot([(MLA_NOPE, _swap_halves(g_q_rope)[None])], 1),
        _slot([(0, g_k_nope[None])], 1),
        _slot([(MLA_NOPE, g_k_rope[None])], 1),
        _slot([(MLA_NOPE, _swap_halves(g_k_rope)[None])], 1),
        jnp.zeros((3, SLOT), F32)], axis=0)
    seg = np.zeros((SLOT, SLOT), np.float32)
    seg[:MLA_NOPE, :MLA_NOPE] = 1.0 / MLA_NOPE
    seg[MLA_NOPE:MLA_NOPE + MLA_ROPE, MLA_NOPE:MLA_NOPE + MLA_ROPE] = 1.0 / MLA_ROPE
    bd = jnp.asarray(seg, BF16)

    half = MLA_ROPE // 2
    inv = ROPE_BASE ** (-jnp.arange(half, dtype=F32) / half)
    ang = jnp.arange(s_max, dtype=F32)[:, None] * inv[None, :]
    cos, sin = jnp.cos(ang), jnp.sin(ang)
    pad = jnp.zeros((s_max, SLOT - MLA_NOPE - MLA_ROPE), F32)
    cos_t = jnp.concatenate([jnp.ones((s_max, MLA_NOPE), F32), cos, cos, pad], axis=1)
    sin_t = jnp.concatenate([jnp.zeros((s_max, MLA_NOPE), F32), -sin, sin, pad], axis=1)

    w_route = jnp.zeros((d, LANES), F32).at[:, :N_EXPERTS].set(w_router)
    w_route = w_route.at[:, N_EXPERTS:N_EXPERTS + N_GROUPS].set(w_group)
    b_route = jnp.zeros((1, LANES), F32).at[0, :N_EXPERTS].set(b_router)
    b_route = b_route.at[0, N_EXPERTS:N_EXPERTS + N_GROUPS].set(b_group)

    return dict(
        g1=g_norm1.reshape(1, d), w_in=w_in_ext, w_gk=w_gk, b_gk=b_gk,
        g_qa=g_q_a.reshape(1, -1), w_q=w_q, g_kva=g_kv_a.reshape(1, -1), w_k=w_k, w_v=w_v,
        v_ones=v_ones, bd=bd, lanevecs=lanevecs, colvecs=lanevecs.T, cos=cos_t, sin=sin_t,
        g_gla=g_gla_out.reshape(1, -1), w_gla_o=w_gla_o.astype(BF16), w_mla_o=w_mla_o.astype(BF16),
        w_out=w_out.astype(BF16), g2=g_norm2.reshape(1, d), w_route=w_route, b_route=b_route,
        w_e_gate=w_e_gate, w_e_up=w_e_up, w_e_down=w_e_down)


def _layer(x, c, w_ada, b_ada, wp):
    mod = _ada(c, w_ada, b_ada)
    q, k, v, zg, zga, zgm, lgf, lgb, qm, km, vm = _inproj(x, mod, wp)
    o_f, o_b = _gla(q, k, v, lgf, lgb)
    o_mla = _mla(qm, km, vm)
    x1, h2, route, counts = _merge(o_f, o_b, zg, zga, zgm, o_mla, x, mod, wp)
    return _moe(x1, h2, route, counts, mod, wp)


def kernel(x_prompt, x_sample, c_prompt, c_sample, w_ada, b_ada, g_norm1, w_in, w_gk_fwd, b_gk_fwd, w_gk_bwd, b_gk_bwd, g_gla_out, w_gla_o, g_q_a, w_q_b, g_kv_a, w_kv_b, g_q_nope, g_k_nope, g_q_rope, g_k_rope, w_mla_o, w_out, g_norm2, w_group, b_group, w_router, b_router, w_e_gate, w_e_up, w_e_down):
    assert w_ada.shape[0] == 1, "single-layer trunk"
    s_max = max(x_prompt.shape[1], x_sample.shape[1])
    wp = _prep_weights(s_max, *[p[0] for p in (
        g_norm1, w_in, w_gk_fwd, b_gk_fwd, w_gk_bwd, b_gk_bwd, g_gla_out, w_gla_o, g_q_a, w_q_b, g_kv_a,
        w_kv_b, g_q_nope, g_k_nope, g_q_rope, g_k_rope, w_mla_o, w_out, g_norm2, w_group, b_group,
        w_router, b_router, w_e_gate, w_e_up, w_e_down)])
    y_prompt = _layer(x_prompt, c_prompt, w_ada[0], b_ada[0], wp)
    y_sample = _layer(x_sample, c_sample, w_ada[0], b_ada[0], wp)
    return (y_prompt, y_sample)
```

```python
import functools
import math

import numpy as np
import jax
import jax.numpy as jnp
from jax import lax
from jax.experimental import pallas as pl
from jax.experimental.pallas import tpu as pltpu

F32 = jnp.float32
BF16 = jnp.bfloat16

D_MODEL = 1024
GLA_HEADS, GLA_DK, GLA_DV, GLA_RANK = 4, 128, 256, 16
GLA_GATE_NORM = 16.0
MLA_HEADS, MLA_Q_RANK, MLA_KV_RANK = 8, 256, 128
MLA_NOPE, MLA_ROPE, MLA_V = 64, 32, 64
ROPE_BASE = 10000.0
N_GROUPS, EXPERTS_PER_GROUP, TOP_K, D_EXPERT = 4, 8, 2, 512
N_EXPERTS = N_GROUPS * EXPERTS_PER_GROUP
NORM_EPS = 1e-6

LANES = 128
ROW_CHUNKS = D_MODEL // LANES
SLOT = LANES
VMEM_LIMIT = 56 * 1024 * 1024

ROW_TILE = 256
GLA_CHUNK = 64
GLA_BLOCK = 256
GLA_EXP_CLAMP = 80.0
MLA_TQ = 512
MLA_VROWS = MLA_V + 16
MLA_UNROLL = 2
MERGE_SPLIT = 2
MOE_TILE = 512
MISC_COLS = MLA_Q_RANK + MLA_KV_RANK + 3 * SLOT


def _cparams(sem):
    return pltpu.CompilerParams(dimension_semantics=sem, vmem_limit_bytes=VMEM_LIMIT)


def _dot(a, b):
    return jnp.dot(a, b, preferred_element_type=F32)


def _dot_nt(a, b):
    return lax.dot_general(a, b, (((1,), (1,)), ((), ())), preferred_element_type=F32)


def _split_bf16(x):
    hi = x.astype(BF16)
    lo = (x - hi.astype(F32)).astype(BF16)
    return hi, lo


def _sigmoid(x):
    return jax.nn.sigmoid(x)


def _silu(x):
    return x * _sigmoid(x)


def _ada_kernel(c_ref, w_ref, b_ref, o_ref):
    a = _silu(c_ref[...])
    ahi, alo = _split_bf16(a)
    w = w_ref[...]
    whi, wlo = _split_bf16(w)
    o_ref[...] = _dot(ahi, whi) + _dot(ahi, wlo) + _dot(alo, whi) + b_ref[...]


def _ada(c, w_ada, b_ada):
    nb = c.shape[0]
    cp = jnp.zeros((8, D_MODEL), F32).at[:nb].set(c)
    tn = 1536
    out = pl.pallas_call(
        _ada_kernel,
        out_shape=jax.ShapeDtypeStruct((8, 6 * D_MODEL), F32),
        grid=(6 * D_MODEL // tn,),
        in_specs=[pl.BlockSpec((8, D_MODEL), lambda j: (0, 0)),
                  pl.BlockSpec((D_MODEL, tn), lambda j: (0, j)),
                  pl.BlockSpec((1, tn), lambda j: (0, j))],
        out_specs=pl.BlockSpec((8, tn), lambda j: (0, j)),
        compiler_params=_cparams(("arbitrary",)),
        name="ada",
    )(cp, w_ada, b_ada.reshape(1, -1))
    mod = out[:nb].reshape(nb, 6, D_MODEL)
    return jnp.concatenate([mod, jnp.zeros((nb, 2, D_MODEL), F32)], axis=1)


_C_Q, _C_K, _C_V, _C_G, _C_GA, _C_GM = 0, 512, 1024, 2048, 3072, 4096
_C_MISC = 5120
IN_EXT = _C_MISC + MISC_COLS


def _segmean(xsq, bd):
    return _dot(xsq.astype(BF16), bd)


def _inproj_kernel(x_ref, mod_ref, g1_ref, win_ref, wgk_ref, bgk_ref, gqa_ref, wq_ref, gkva_ref,
                   wk_ref, wv_ref, vone_ref, bd_ref, lv_ref, cv_ref, cos_ref, sin_ref, cosc_ref, sinc_ref,
                   q_o, k_o, v_o, zg_o, zga_o, zgm_o, lgf_o, lgb_o, qm_o, km_o, vm_o, h_scr):
    x = x_ref[0]
    ms = jnp.mean(x * x, axis=-1, keepdims=True)
    h = (x * lax.rsqrt(ms + NORM_EPS) * g1_ref[...]) * (1.0 + mod_ref[0, 1:2, :]) + mod_ref[0, 0:1, :]
    h_scr[...] = h.astype(BF16)

    def proj(c0, n):
        return _dot(h_scr[...], win_ref[:, c0:c0 + n])

    q_o[0] = (proj(_C_Q, 512) * (GLA_DK ** -0.5)).astype(BF16)
    k_o[0] = proj(_C_K, 512).astype(BF16)
    v_o[0] = proj(_C_V, 1024).astype(BF16)
    zg_o[0] = proj(_C_G, 1024).astype(BF16)
    zga_o[0] = proj(_C_GA, 1024).astype(BF16)
    zgm_o[0] = proj(_C_GM, 1024).astype(BF16)
    misc = proj(_C_MISC, MISC_COLS)

    bd = bd_ref[...]
    cos_t = cos_ref[...]
    sin_t = sin_ref[...]
    g_kn, g_kr, g_krs = lv_ref[2:3, :], lv_ref[3:4, :], lv_ref[4:5, :]
    scale = (MLA_NOPE + MLA_ROPE) ** -0.5 * math.log2(math.e)

    zqa = misc[:, 0:MLA_Q_RANK]
    qa = zqa * lax.rsqrt(jnp.mean(zqa * zqa, axis=-1, keepdims=True) + NORM_EPS) * gqa_ref[...]
    qq = _dot_nt(wq_ref[...], qa.astype(BF16))
    cos_c, sin_c = cosc_ref[...], sinc_ref[...]
    gc_q, gc_qs = cv_ref[:, 0:1], cv_ref[:, 1:2]
    for hh in range(MLA_HEADS):
        xq = qq[hh * SLOT:(hh + 1) * SLOT, :]
        xs = qq[(MLA_HEADS + hh) * SLOT:(MLA_HEADS + hh + 1) * SLOT, :]
        r = lax.rsqrt(_dot(bd, (xq * xq).astype(BF16)) + NORM_EPS)
        qm = (xq * gc_q * cos_c + xs * gc_qs * sin_c) * (r * scale)
        qm_o[0, hh * SLOT:(hh + 1) * SLOT, :] = qm.astype(BF16)

    c0 = MLA_Q_RANK
    zkva = misc[:, c0:c0 + MLA_KV_RANK]
    kva = zkva * lax.rsqrt(jnp.mean(zkva * zkva, axis=-1, keepdims=True) + NORM_EPS) * gkva_ref[...]
    kvab = kva.astype(BF16)
    kk = _dot(kvab, wk_ref[...])
    vm_o[0, 0] = (_dot_nt(wv_ref[...], kvab) + vone_ref[...]).astype(BF16)
    c0 += MLA_KV_RANK
    kr = misc[:, c0:c0 + SLOT]
    krs = misc[:, c0 + SLOT:c0 + 2 * SLOT]
    r_kr = lax.rsqrt(_segmean(kr * kr, bd) + NORM_EPS)
    kpe = (kr * g_kr * cos_t + krs * g_krs * sin_t) * r_kr
    for hh in range(MLA_HEADS):
        xk = kk[:, hh * SLOT:(hh + 1) * SLOT]
        r = lax.rsqrt(_segmean(xk * xk, bd) + NORM_EPS)
        km_o[0, :, hh * SLOT:(hh + 1) * SLOT] = (xk * r * g_kn + kpe).astype(BF16)

    c0 += 2 * SLOT
    zgate = misc[:, c0:c0 + SLOT].astype(BF16)
    pre = _dot(zgate, wgk_ref[...]) + bgk_ref[...]
    lg = (jnp.minimum(pre, 0.0) - jnp.log(1.0 + jnp.exp(-jnp.abs(pre)))) * (1.0 / GLA_GATE_NORM)
    lgf_o[0] = lg[:, :GLA_HEADS * GLA_DK]
    lgb_o[0] = lg[:, GLA_HEADS * GLA_DK:]


def _inproj(x, mod, wp):
    nb, s, d = x.shape
    tm = min(ROW_TILE, s)
    const = lambda b, i: (0, 0)
    row = lambda b, i: (b, i, 0)

    def full(a):
        return pl.BlockSpec(a.shape, const)

    def out(n, dt):
        return jax.ShapeDtypeStruct((nb, s, n), dt), pl.BlockSpec((1, tm, n), row)

    outs = [out(512, BF16), out(512, BF16), out(1024, BF16), out(1024, BF16), out(1024, BF16),
            out(1024, BF16), out(512, F32), out(512, F32),
            (jax.ShapeDtypeStruct((nb, MLA_HEADS * SLOT, s), BF16),
             pl.BlockSpec((1, MLA_HEADS * SLOT, tm), lambda b, i: (b, 0, i))),
            out(MLA_HEADS * SLOT, BF16),
            (jax.ShapeDtypeStruct((nb, s // tm, MLA_HEADS * MLA_VROWS, tm), BF16),
             pl.BlockSpec((1, 1, MLA_HEADS * MLA_VROWS, tm), lambda b, i: (b, i, 0, 0)))]
    return pl.pallas_call(
        _inproj_kernel,
        out_shape=[o[0] for o in outs],
        grid=(nb, s // tm),
        in_specs=[pl.BlockSpec((1, tm, d), row),
                  pl.BlockSpec((1, 8, d), lambda b, i: (b, 0, 0)),
                  full(wp["g1"]), full(wp["w_in"]), full(wp["w_gk"]), full(wp["b_gk"]),
                  full(wp["g_qa"]), full(wp["w_q"]), full(wp["g_kva"]), full(wp["w_k"]), full(wp["w_v"]),
                  full(wp["v_ones"]), full(wp["bd"]), full(wp["lanevecs"]), full(wp["colvecs"]),
                  pl.BlockSpec((tm, SLOT), lambda b, i: (i, 0)),
                  pl.BlockSpec((tm, SLOT), lambda b, i: (i, 0)),
                  pl.BlockSpec((SLOT, tm), lambda b, i: (0, i)),
                  pl.BlockSpec((SLOT, tm), lambda b, i: (0, i))],
        out_specs=[o[1] for o in outs],
        scratch_shapes=[pltpu.VMEM((tm, d), BF16)],
        compiler_params=_cparams(("arbitrary", "arbitrary")),
        name="inproj",
    )(x, mod, wp["g1"], wp["w_in"], wp["w_gk"], wp["b_gk"], wp["g_qa"], wp["w_q"], wp["g_kva"],
      wp["w_k"], wp["w_v"], wp["v_ones"], wp["bd"], wp["lanevecs"], wp["colvecs"],
      wp["cos"][:s], wp["sin"][:s], wp["cos"][:s].T, wp["sin"][:s].T)


def _dot_tn(a, b):
    return lax.dot_general(a, b, (((0,), (0,)), ((), ())), preferred_element_type=F32)


def _gla_direction(q_ref, k_ref, v_ref, g_ref, o_ref, st_ref, scr, cum, mask, reverse):
    qin_s, qmid_s, kmid_s, kout_s, tot_s, u_s = scr
    c = GLA_CHUNK
    nch = q_ref.shape[1] // c
    rows = [slice(j * c, (j + 1) * c) for j in range(nch)]
    ks = [slice(h * GLA_DK, (h + 1) * GLA_DK) for h in range(GLA_HEADS)]
    vs = [slice(h * GLA_DV, (h + 1) * GLA_DV) for h in range(GLA_HEADS)]

    for j in range(nch):
        hi, lo = _split_bf16(g_ref[0, rows[j], :])
        b = _dot(cum, hi) + _dot(cum, lo)
        mid = b[c // 2:c // 2 + 1, :]
        tot = b[0:1, :] if reverse else b[c - 1:c, :]
        q = q_ref[0, rows[j], :].astype(F32)
        k = k_ref[0, rows[j], :].astype(F32)
        qin_s[rows[j], :] = (q * jnp.exp(b)).astype(BF16)
        qmid_s[rows[j], :] = (q * jnp.exp(jnp.minimum(b - mid, GLA_EXP_CLAMP))).astype(BF16)
        kmid_s[rows[j], :] = (k * jnp.exp(jnp.minimum(mid - b, GLA_EXP_CLAMP))).astype(BF16)
        kout_s[rows[j], :] = (k * jnp.exp(tot - b)).astype(BF16)
        tot_s[j:j + 1, :] = tot

    attn = {}
    for j in range(nch):
        for h in range(GLA_HEADS):
            a = _dot_nt(qmid_s[rows[j], ks[h]], kmid_s[rows[j], ks[h]])
            attn[j, h] = jnp.where(mask, a, 0.0).astype(BF16)

    for j in range(nch):
        for h in range(GLA_HEADS):
            v = v_ref[0, rows[j], vs[h]]
            o_ref[0, rows[j], vs[h]] = _dot(attn[j, h], v)
            u_s[j * GLA_HEADS + h] = _dot_tn(kout_s[rows[j], ks[h]], v)

    for j in (range(nch - 1, -1, -1) if reverse else range(nch)):
        for h in range(GLA_HEADS):
            st = st_ref[h]
            o_ref[0, rows[j], vs[h]] += _dot(qin_s[rows[j], ks[h]], st.astype(BF16))
            tot = jnp.broadcast_to(tot_s[j:j + 1, ks[h]], (GLA_DK, GLA_DK))
            dec = jnp.exp(tot.T)
            st_ref[h] = jnp.concatenate([dec] * (GLA_DV // GLA_DK), axis=1) * st + u_s[j * GLA_HEADS + h]


def _gla_kernel(qf, kf, vf, gf, qb, kb, vb, gb, of_ref, ob_ref, sf_ref, sb_ref, *scr):
    @pl.when(pl.program_id(1) == 0)
    def _():
        sf_ref[...] = jnp.zeros_like(sf_ref)
        sb_ref[...] = jnp.zeros_like(sb_ref)

    c = GLA_CHUNK
    ri = lax.broadcasted_iota(jnp.int32, (c, c), 0)
    ci = lax.broadcasted_iota(jnp.int32, (c, c), 1)
    cum_f = jnp.where(ri >= ci, 1.0, 0.0).astype(BF16)
    cum_b = jnp.where(ci >= ri, 1.0, 0.0).astype(BF16)
    n = len(scr) // 2
    _gla_direction(qf, kf, vf, gf, of_ref, sf_ref, scr[:n], cum_f, ri >= ci, False)
    _gla_direction(qb, kb, vb, gb, ob_ref, sb_ref, scr[n:], cum_b, ci > ri, True)


def _gla(q, k, v, lgf, lgb):
    nb, s, _ = q.shape
    cb = min(GLA_BLOCK, s)
    ns = s // cb
    nch = cb // GLA_CHUNK
    fwd = lambda b, i: (b, i, 0)
    bwd = lambda b, i: (b, ns - 1 - i, 0)
    hk, hv = GLA_HEADS * GLA_DK, GLA_HEADS * GLA_DV

    def specs(im):
        return [pl.BlockSpec((1, cb, hk), im), pl.BlockSpec((1, cb, hk), im),
                pl.BlockSpec((1, cb, hv), im), pl.BlockSpec((1, cb, hk), im)]

    per_dir = [pltpu.VMEM((cb, hk), BF16)] * 4 + [pltpu.VMEM((8, hk), F32),
                                                  pltpu.VMEM((nch * GLA_HEADS, GLA_DK, GLA_DV), F32)]
    return pl.pallas_call(
        _gla_kernel,
        out_shape=[jax.ShapeDtypeStruct((nb, s, hv), F32)] * 2,
        grid=(nb, ns),
        in_specs=specs(fwd) + specs(bwd),
        out_specs=[pl.BlockSpec((1, cb, hv), fwd), pl.BlockSpec((1, cb, hv), bwd)],
        scratch_shapes=[pltpu.VMEM((GLA_HEADS, GLA_DK, GLA_DV), F32)] * 2 + per_dir * 2,
        compiler_params=_cparams(("arbitrary", "arbitrary")),
        name="gla",
    )(q, k, v, lgf, q, k, v, lgb)


def _mla_kernel(q_ref, k_ref, vt_ref, o_ref, sc0, sc1, p0, p1):
    tq = q_ref.shape[2]
    nk = vt_ref.shape[1]
    tk = vt_ref.shape[3]
    gu = MLA_UNROLL
    ng = nk // gu
    sc, pp = (sc0, sc1), (p0, p1)
    qs = [q_ref[0, hh * SLOT:(hh + 1) * SLOT, :] for hh in range(2)]
    tiles = [(u, hh) for u in range(gu) for hh in range(2)]

    def score_tile(g, slot, u, hh):
        r0 = pl.multiple_of((g * gu + u) * tk, tk)
        kt = k_ref[0, pl.ds(r0, tk), hh * SLOT:(hh + 1) * SLOT]
        st = _dot(kt, qs[hh])
        sc[slot][2 * u + hh] = st
        return jnp.max(st, axis=0, keepdims=True)

    def softmax_tile(slot, u, hh, m, tmax):
        m_new = jnp.maximum(m, tmax)
        pp[slot][2 * u + hh] = jnp.exp2(sc[slot][2 * u + hh] - m_new).astype(BF16)
        return m_new, jnp.exp2(m - m_new)

    def value_tile(g, slot, u, hh, alpha, acc):
        vt = vt_ref[0, g * gu + u, hh * MLA_VROWS:(hh + 1) * MLA_VROWS, :]
        return alpha * acc + _dot(vt, pp[slot][2 * u + hh])

    def issue_scores(g, slot):
        return tuple(score_tile(g, slot, u, hh) for u, hh in tiles)

    def softmax(slot, ms, tmax):
        ms, alphas = list(ms), []
        for u, hh in tiles:
            ms[hh], alpha = softmax_tile(slot, u, hh, ms[hh], tmax[2 * u + hh])
            alphas.append(alpha)
        return tuple(ms), tuple(alphas)

    def values(g, slot, alphas, accs):
        accs = list(accs)
        for u, hh in tiles:
            accs[hh] = value_tile(g, slot, u, hh, alphas[2 * u + hh], accs[hh])
        return tuple(accs)

    def step(g, slot, carry, more_scores):
        ms, alphas, accs, tmax = carry
        ms, accs, new_alphas, new_tmax = list(ms), list(accs), [], []
        for u, hh in tiles:
            idx = 2 * u + hh
            accs[hh] = value_tile(g - 1, 1 - slot, u, hh, alphas[idx], accs[hh])
            ms[hh], alpha = softmax_tile(slot, u, hh, ms[hh], tmax[idx])
            new_alphas.append(alpha)
            if more_scores:
                new_tmax.append(score_tile(g + 1, 1 - slot, u, hh))
        return tuple(ms), tuple(new_alphas), tuple(accs), tuple(new_tmax)

    def double_step(i, carry):
        g = 2 * i + 1
        carry = step(g, 1, carry, True)
        return step(g + 1, 0, carry, True)

    m0 = jnp.full((1, tq), -jnp.inf, F32)
    acc0 = jnp.zeros((MLA_VROWS, tq), F32)
    tmax = issue_scores(0, 0)
    ms, alphas = softmax(0, (m0, m0), tmax)
    tmax = issue_scores(1, 1)
    carry = (ms, alphas, (acc0, acc0), tmax)
    for i in range((ng - 2) // 2):
        carry = double_step(i, carry)
    ms, alphas, accs, _ = step(ng - 1, 1, carry, False)
    accs = values(ng - 1, 1, alphas, accs)
    outs = [(a[:MLA_V] / a[MLA_V:MLA_V + 1]).T for a in accs]
    o_ref[0] = jnp.concatenate(outs, axis=-1).astype(BF16)


def _mla(qm, km, vmt):
    nb, s, _ = km.shape
    nk, tk = vmt.shape[1], vmt.shape[3]
    tq = min(MLA_TQ, s)
    assert (nk // MLA_UNROLL) % 2 == 0 and nk % MLA_UNROLL == 0
    slot_tiles = 2 * MLA_UNROLL
    return pl.pallas_call(
        _mla_kernel,
        out_shape=jax.ShapeDtypeStruct((nb, s, MLA_HEADS * MLA_V), BF16),
        grid=(nb, MLA_HEADS // 2, s // tq),
        in_specs=[pl.BlockSpec((1, 2 * SLOT, tq), lambda b, p, i: (b, p, i)),
                  pl.BlockSpec((1, s, 2 * SLOT), lambda b, p, i: (b, 0, p)),
                  pl.BlockSpec((1, nk, 2 * MLA_VROWS, tk), lambda b, p, i: (b, 0, p, 0))],
        out_specs=pl.BlockSpec((1, tq, 2 * MLA_V), lambda b, p, i: (b, i, p)),
        scratch_shapes=[pltpu.VMEM((slot_tiles, tk, tq), F32)] * 2 + [pltpu.VMEM((slot_tiles, tk, tq), BF16)] * 2,
        compiler_params=_cparams(("arbitrary", "arbitrary", "arbitrary")),
        name="mla",
    )(qm, km, vmt)


def _merge_kernel(of_ref, ob_ref, zg_ref, zga_ref, zgm_ref, om_ref, x_ref, mod_ref, ggla_ref, wglo_ref,
                  wmo_ref, wout_ref, g2_ref, wr_ref, br_ref, ltri_ref,
                  x1_o, h2_o, route_o, cnt_o, cnt_scr):
    first = (pl.program_id(0) == 0) & (pl.program_id(1) == 0)

    @pl.when(first)
    def _():
        cnt_scr[...] = jnp.zeros_like(cnt_scr)

    tm = x_ref.shape[1]
    hm = tm // MERGE_SPLIT
    subs = [slice(i * hm, (i + 1) * hm) for i in range(MERGE_SPLIT)]

    def gla_gate(rs):
        o = of_ref[0, rs, :] + ob_ref[0, rs, :]
        zg = zg_ref[0, rs, :].astype(F32)
        parts = []
        for hh in range(GLA_HEADS):
            seg = o[:, hh * GLA_DV:(hh + 1) * GLA_DV]
            r = lax.rsqrt(jnp.mean(seg * seg, axis=-1, keepdims=True) + NORM_EPS)
            parts.append((seg * r * ggla_ref[...]) * _silu(zg[:, hh * GLA_DV:(hh + 1) * GLA_DV]))
        return jnp.concatenate(parts, axis=-1).astype(BF16)

    ogs = [gla_gate(rs) for rs in subs]
    ys = [(_dot(og, wglo_ref[...]), _dot(om_ref[0, rs, :], wmo_ref[...])) for og, rs in zip(ogs, subs)]
    merged = [(_sigmoid(zga_ref[0, rs, :].astype(F32)) * y_gla
               + _sigmoid(zgm_ref[0, rs, :].astype(F32)) * y_mla).astype(BF16) for (y_gla, y_mla), rs in zip(ys, subs)]
    mixes = [_dot(m, wout_ref[...]) for m in merged]

    w = wr_ref[...]
    whi, wlo = _split_bf16(w)
    splits = []
    for mix, rs in zip(mixes, subs):
        x1 = x_ref[0, rs, :] + mod_ref[0, 2:3, :] * mix
        x1_o[0, rs, :] = x1
        ms = jnp.mean(x1 * x1, axis=-1, keepdims=True)
        h2 = (x1 * lax.rsqrt(ms + NORM_EPS) * g2_ref[...]) * (1.0 + mod_ref[0, 4:5, :]) + mod_ref[0, 3:4, :]
        for c in range(ROW_CHUNKS):
            h2_o[pl.ds(rs.start * ROW_CHUNKS + c, hm, stride=ROW_CHUNKS), :] = h2[:, c * LANES:(c + 1) * LANES]
        splits.append(_split_bf16(h2))

    logit_parts = [_dot(hhi, whi) + _dot(hhi, wlo) + _dot(hlo, whi) + br_ref[...] for hhi, hlo in splits]
    logits = jnp.concatenate(logit_parts, axis=0)
    lane = lax.broadcasted_iota(jnp.int32, (tm, LANES), 1)
    neg = -jnp.inf
    is_g = (lane >= N_EXPERTS) & (lane < N_EXPERTS + N_GROUPS)
    lgm = jnp.where(is_g, logits, neg)
    mg = jnp.max(lgm, axis=-1, keepdims=True)
    sg = jnp.sum(jnp.exp(lgm - mg), axis=-1, keepdims=True)
    p_top = 1.0 / sg
    g_idx = jnp.min(jnp.where(lgm == mg, lane, 2 * LANES), axis=-1, keepdims=True) - N_EXPERTS
    in_grp = (lane >= g_idx * EXPERTS_PER_GROUP) & (lane < (g_idx + 1) * EXPERTS_PER_GROUP)
    el = jnp.where(in_grp, logits, neg)
    m1 = jnp.max(el, axis=-1, keepdims=True)
    e1 = jnp.min(jnp.where(el == m1, lane, 2 * LANES), axis=-1, keepdims=True)
    el2 = jnp.where(lane == e1, neg, el)
    m2 = jnp.max(el2, axis=-1, keepdims=True)
    e2 = jnp.min(jnp.where(el2 == m2, lane, 2 * LANES), axis=-1, keepdims=True)
    se = jnp.sum(jnp.exp(el - m1), axis=-1, keepdims=True)
    pv1 = 1.0 / se
    pv2 = jnp.exp(m2 - m1) / se
    w1 = p_top * pv1 / (pv1 + pv2)
    w2 = p_top * pv2 / (pv1 + pv2)

    oh1 = lane == e1
    oh2 = lane == e2
    oh = jnp.where(oh1 | oh2, 1.0, 0.0)
    prefix = _dot(ltri_ref[...], oh.astype(BF16)) + cnt_scr[...]
    r1 = jnp.sum(jnp.where(oh1, prefix, 0.0), axis=-1, keepdims=True)
    r2 = jnp.sum(jnp.where(oh2, prefix, 0.0), axis=-1, keepdims=True)
    cnt_scr[...] = cnt_scr[...] + jnp.sum(oh, axis=0, keepdims=True)
    cnt_o[...] = jnp.broadcast_to(cnt_scr[...], cnt_o.shape)

    route = jnp.where(lane == 0, e1.astype(F32), 0.0)
    route = jnp.where(lane == 1, e2.astype(F32), route)
    route = jnp.where(lane == 2, r1, route)
    route = jnp.where(lane == 3, r2, route)
    route = jnp.where(lane == 4, w1, route)
    route = jnp.where(lane == 5, w2, route)
    route_o[0] = route


def _merge(o_f, o_b, zg, zga, zgm, o_mla, x, mod, wp):
    nb, s, d = x.shape
    tm = min(ROW_TILE, s)
    const = lambda b, i: (0, 0)
    row = lambda b, i: (b, i, 0)

    def full(a):
        return pl.BlockSpec(a.shape, const)

    def rows(n):
        return pl.BlockSpec((1, tm, n), row)

    ltri = jnp.asarray(np.tril(np.ones((tm, tm), np.float32), -1), BF16)
    return pl.pallas_call(
        _merge_kernel,
        out_shape=[jax.ShapeDtypeStruct((nb, s, d), F32), jax.ShapeDtypeStruct((nb * s * ROW_CHUNKS, LANES), F32),
                   jax.ShapeDtypeStruct((nb, s, LANES), F32), jax.ShapeDtypeStruct((8, LANES), F32)],
        grid=(nb, s // tm),
        in_specs=[rows(1024), rows(1024), rows(1024), rows(1024), rows(1024), rows(MLA_HEADS * MLA_V),
                  rows(d), pl.BlockSpec((1, 8, d), lambda b, i: (b, 0, 0)),
                  full(wp["g_gla"]), full(wp["w_gla_o"]), full(wp["w_mla_o"]), full(wp["w_out"]),
                  full(wp["g2"]), full(wp["w_route"]), full(wp["b_route"]), full(ltri)],
        out_specs=[rows(d), pl.BlockSpec((tm * ROW_CHUNKS, LANES), lambda b, i: (b * (s // tm) + i, 0)),
                   rows(LANES), pl.BlockSpec((8, LANES), const)],
        scratch_shapes=[pltpu.VMEM((1, LANES), F32)],
        compiler_params=_cparams(("arbitrary", "arbitrary")),
        name="merge",
    )(o_f, o_b, zg, zga, zgm, o_mla, x, mod, wp["g_gla"], wp["w_gla_o"], wp["w_mla_o"], wp["w_out"],
      wp["g2"], wp["w_route"], wp["b_route"], ltri)


def _dispatch_kernel(seg_ref, dest_ref, h_ref, xs_out, zero_scr, sem):
    tt = h_ref.shape[0] // ROW_CHUNKS
    tile_rows = MOE_TILE * ROW_CHUNKS

    @pl.when(pl.program_id(0) == 0)
    def _():
        zero_scr[...] = jnp.zeros_like(zero_scr)

        def zero_copy(e):
            r0 = pl.multiple_of((seg_ref[0, e] - MOE_TILE) * ROW_CHUNKS, tile_rows)
            return pltpu.make_async_copy(zero_scr, xs_out.at[pl.ds(r0, tile_rows)], sem.at[1])

        for e in range(N_EXPERTS):
            @pl.when(seg_ref[1, e] > 0)
            def _():
                zero_copy(e).start()
        for e in range(N_EXPERTS):
            @pl.when(seg_ref[1, e] > 0)
            def _():
                zero_copy(e).wait()

        def tail_copy(ti):
            return pltpu.make_async_copy(zero_scr, xs_out.at[pl.ds(pl.multiple_of(ti * tile_rows, tile_rows), tile_rows)],
                                         sem.at[1])

        first_tail = seg_ref[0, N_EXPERTS - 1] // MOE_TILE
        n_tiles = xs_out.shape[0] // tile_rows
        lax.fori_loop(first_tail, n_tiles, lambda ti, c: (tail_copy(ti).start(), c)[1], 0)
        lax.fori_loop(first_tail, n_tiles, lambda ti, c: (tail_copy(ti).wait(), c)[1], 0)

    def copy(r, slot):
        d0 = pl.multiple_of(dest_ref[0, 0, 2 * r + slot] * ROW_CHUNKS, ROW_CHUNKS)
        return pltpu.make_async_copy(h_ref.at[pl.ds(pl.multiple_of(r * ROW_CHUNKS, ROW_CHUNKS), ROW_CHUNKS)],
                                     xs_out.at[pl.ds(d0, ROW_CHUNKS)], sem.at[0])

    def start(r, c):
        copy(r, 0).start(priority=0)
        copy(r, 1).start(priority=1)
        return c

    def wait(r, c):
        copy(r, 0).wait()
        copy(r, 1).wait()
        return c

    lax.fori_loop(0, tt, start, 0, unroll=8)
    lax.fori_loop(0, tt, wait, 0, unroll=8)


def _dispatch(h2, dest, segments, n_rows):
    t = h2.shape[0] // ROW_CHUNKS
    tt = min(ROW_TILE, t)
    nt = t // tt
    return pl.pallas_call(
        _dispatch_kernel,
        out_shape=jax.ShapeDtypeStruct((n_rows * ROW_CHUNKS, LANES), F32),
        grid_spec=pltpu.PrefetchScalarGridSpec(
            num_scalar_prefetch=1,
            grid=(nt,),
            in_specs=[pl.BlockSpec((1, 1, 2 * tt), lambda i, seg: (i, 0, 0), memory_space=pltpu.SMEM),
                      pl.BlockSpec((tt * ROW_CHUNKS, LANES), lambda i, seg: (i, 0))],
            out_specs=pl.BlockSpec(memory_space=pl.ANY),
            scratch_shapes=[pltpu.VMEM((MOE_TILE * ROW_CHUNKS, LANES), F32), pltpu.SemaphoreType.DMA((2,))]),
        compiler_params=_cparams(("arbitrary",)),
        name="dispatch",
    )(segments, dest.reshape(nt, 1, 2 * tt), h2)


def _expert_kernel(te_ref, na_ref, xs_ref, wg_ref, wu_ref, wd_ref, ys_ref, xb_scr):
    active = pl.program_id(0) < na_ref[0]

    @pl.when(active)
    def _():
        for c in range(ROW_CHUNKS):
            xb_scr[:, c * LANES:(c + 1) * LANES] = xs_ref[pl.ds(c, MOE_TILE, stride=ROW_CHUNKS), :].astype(BF16)
        xb = xb_scr[...]
        a = _silu(_dot(xb, wg_ref[0].astype(BF16))) * _dot(xb, wu_ref[0].astype(BF16))
        y = _dot(a.astype(BF16), wd_ref[0].astype(BF16))
        for c in range(ROW_CHUNKS):
            ys_ref[pl.ds(c, MOE_TILE, stride=ROW_CHUNKS), :] = y[:, c * LANES:(c + 1) * LANES]

    @pl.when(jnp.logical_not(active))
    def _():
        ys_ref[...] = jnp.zeros_like(ys_ref)


def _experts(xs, tile_expert, n_active, wp):
    d = D_MODEL
    n_rows = xs.shape[0] // ROW_CHUNKS
    nt = n_rows // MOE_TILE
    rows = MOE_TILE * ROW_CHUNKS

    def tile(i, te, na):
        return (jnp.minimum(i, na[0] - 1), 0)

    def wmap(i, te, na):
        return (te[jnp.minimum(i, na[0] - 1)], 0, 0)

    return pl.pallas_call(
        _expert_kernel,
        out_shape=jax.ShapeDtypeStruct((n_rows * ROW_CHUNKS, LANES), F32),
        grid_spec=pltpu.PrefetchScalarGridSpec(
            num_scalar_prefetch=2,
            grid=(nt,),
            in_specs=[pl.BlockSpec((rows, LANES), tile),
                      pl.BlockSpec((1, d, D_EXPERT), wmap),
                      pl.BlockSpec((1, d, D_EXPERT), wmap),
                      pl.BlockSpec((1, D_EXPERT, d), wmap)],
            out_specs=pl.BlockSpec((rows, LANES), lambda i, te, na: (i, 0)),
            scratch_shapes=[pltpu.VMEM((MOE_TILE, d), BF16)]),
        compiler_params=_cparams(("arbitrary",)),
        name="experts",
    )(tile_expert, n_active, xs, wp["w_e_gate"], wp["w_e_up"], wp["w_e_down"])


def _combine_kernel(dest_ref, x1_ref, rw_ref, ga_ref, ys_hbm, o_ref, buf, sem):
    tt = x1_ref.shape[0]
    i = pl.program_id(0)
    nt = pl.num_programs(0) - 1
    par = lax.rem(i, 2)

    def copy(p, r, slot, d0):
        return pltpu.make_async_copy(ys_hbm.at[pl.ds(d0, ROW_CHUNKS)],
                                     buf.at[p, slot, pl.ds(pl.multiple_of(r * ROW_CHUNKS, ROW_CHUNKS), ROW_CHUNKS)],
                                     sem.at[p])

    @pl.when(i < nt)
    def _():
        def start(r, c):
            for slot in range(TOP_K):
                d0 = pl.multiple_of(dest_ref[0, 0, 2 * r + slot] * ROW_CHUNKS, ROW_CHUNKS)
                copy(par, r, slot, d0).start(priority=slot)
            return c

        lax.fori_loop(0, tt, start, 0, unroll=8)

    @pl.when(i > 0)
    def _():
        def wait(r, c):
            for slot in range(TOP_K):
                copy(1 - par, r, slot, 0).wait()
            return c

        lax.fori_loop(0, tt, wait, 0, unroll=8)
        rw = rw_ref[...]
        for c in range(ROW_CHUNKS):
            cs = slice(c * LANES, (c + 1) * LANES)
            rows = pl.ds(c, tt, stride=ROW_CHUNKS)
            moe = buf[1 - par, 0, rows, :] * rw[:, 4:5] + buf[1 - par, 1, rows, :] * rw[:, 5:6]
            o_ref[:, cs] = x1_ref[:, cs] + ga_ref[0, 5:6, cs] * moe


def _combine(x1, route, mod, ys, dest, s):
    t, d = x1.shape
    tt = min(ROW_TILE, s)
    nt = t // tt
    per_b = s // tt
    lag = lambda i: jnp.maximum(i - 1, 0)
    return pl.pallas_call(
        _combine_kernel,
        out_shape=jax.ShapeDtypeStruct((t, d), F32),
        grid=(nt + 1,),
        in_specs=[pl.BlockSpec((1, 1, 2 * tt), lambda i: (jnp.minimum(i, nt - 1), 0, 0), memory_space=pltpu.SMEM),
                  pl.BlockSpec((tt, d), lambda i: (lag(i), 0)),
                  pl.BlockSpec((tt, LANES), lambda i: (lag(i), 0)),
                  pl.BlockSpec((1, 8, d), lambda i: (lag(i) // per_b, 0, 0)),
                  pl.BlockSpec(memory_space=pl.ANY)],
        out_specs=pl.BlockSpec((tt, d), lambda i: (lag(i), 0)),
        scratch_shapes=[pltpu.VMEM((2, TOP_K, tt * ROW_CHUNKS, LANES), F32), pltpu.SemaphoreType.DMA((2,))],
        compiler_params=_cparams(("arbitrary",)),
        name="combine",
    )(dest.reshape(nt, 1, 2 * tt), x1, route, mod, ys)


def _moe(x1, h2, route, counts, mod, wp):
    nb, s, d = x1.shape
    t = nb * s
    route2 = route.reshape(t, LANES)
    e_flat = route2[:, 0:2].astype(jnp.int32)
    rank = route2[:, 2:4].astype(jnp.int32)
    cnt = counts[0, :N_EXPERTS].astype(jnp.int32)
    padded = (cnt + MOE_TILE - 1) // MOE_TILE * MOE_TILE
    pad_end = jnp.cumsum(padded)
    pad_start = pad_end - padded
    is_e = e_flat[..., None] == jnp.arange(N_EXPERTS, dtype=jnp.int32)
    dest = (jnp.sum(jnp.where(is_e, pad_start, 0), axis=-1) + rank).reshape(t * TOP_K)
    n_rows = t * TOP_K + N_EXPERTS * MOE_TILE
    nt = n_rows // MOE_TILE
    tile_row = jnp.arange(nt, dtype=jnp.int32) * MOE_TILE
    tile_expert = jnp.minimum(jnp.sum((pad_end[None, :] <= tile_row[:, None]).astype(jnp.int32), axis=1),
                              N_EXPERTS - 1)
    n_active = (pad_end[-1:] // MOE_TILE).astype(jnp.int32)
    xs = _dispatch(h2, dest, jnp.stack([pad_end, padded]).astype(jnp.int32), n_rows)
    ys = _experts(xs, tile_expert, n_active, wp)
    out = _combine(x1.reshape(t, d), route2, mod, ys, dest, s)
    return out.reshape(nb, s, d)


def _swap_halves(a, axis=-1):
    n = a.shape[axis] // 2
    lo, hi = jnp.split(a, [n], axis=axis)
    return jnp.concatenate([hi, lo], axis=axis)


def _slot(parts, n_lead):
    out = jnp.zeros((n_lead, SLOT), F32)
    for off, a in parts:
        out = out.at[:, off:off + a.shape[-1]].set(a)
    return out


def _prep_weights(s_max, g_norm1, w_in, w_gk_fwd, b_gk_fwd, w_gk_bwd, b_gk_bwd, g_gla_out, w_gla_o, g_q_a,
                  w_q_b, g_kv_a, w_kv_b, g_q_nope, g_k_nope, g_q_rope, g_k_rope, w_mla_o, w_out, g_norm2,
                  w_group, b_group, w_router, b_router, w_e_gate, w_e_up, w_e_down):
    d = D_MODEL
    hk, hv = GLA_HEADS * GLA_DK, GLA_HEADS * GLA_DV
    offs = np.cumsum([0, hk, hk, hv, hv, GLA_RANK, GLA_RANK, MLA_Q_RANK, MLA_KV_RANK, MLA_ROPE, d, d])
    cols = [w_in[:, offs[i]:offs[i + 1]] for i in range(11)]
    wq_, wk_, wv_, wg_, wgf, wgb, wqa, wkva, wkr, wga, wgm = cols
    kr_slot = _slot([(MLA_NOPE, wkr)], d)
    krs_slot = _slot([(MLA_NOPE, _swap_halves(wkr))], d)
    gate_slot = _slot([(0, wgf), (GLA_RANK, wgb)], d)
    w_in_ext = jnp.concatenate([wq_, wk_, wv_, wg_, wga, wgm, wqa, wkva, kr_slot, krs_slot, gate_slot],
                               axis=1).astype(BF16)
    w_gk = jnp.zeros((SLOT, 2 * hk), F32)
    w_gk = w_gk.at[0:GLA_RANK, 0:hk].set(w_gk_fwd).at[GLA_RANK:2 * GLA_RANK, hk:].set(w_gk_bwd).astype(BF16)
    b_gk = jnp.concatenate([b_gk_fwd, b_gk_bwd]).reshape(1, -1)

    wq3 = w_q_b.reshape(MLA_Q_RANK, MLA_HEADS, MLA_NOPE + MLA_ROPE)
    zq = jnp.zeros((MLA_Q_RANK, MLA_HEADS, SLOT - MLA_NOPE - MLA_ROPE), F32)
    main = jnp.concatenate([wq3, zq], axis=-1)
    swp = jnp.concatenate([jnp.zeros((MLA_Q_RANK, MLA_HEADS, MLA_NOPE), F32),
                           _swap_halves(wq3[..., MLA_NOPE:]), zq], axis=-1)
    w_q = jnp.concatenate([main.reshape(MLA_Q_RANK, -1), swp.reshape(MLA_Q_RANK, -1)], axis=1).T.astype(BF16)
    wkv3 = w_kv_b.reshape(MLA_KV_RANK, MLA_HEADS, MLA_NOPE + MLA_V)
    w_k = jnp.concatenate([wkv3[..., :MLA_NOPE], jnp.zeros((MLA_KV_RANK, MLA_HEADS, SLOT - MLA_NOPE), F32)],
                          axis=-1).reshape(MLA_KV_RANK, -1).astype(BF16)
    w_v = jnp.concatenate([wkv3[..., MLA_NOPE:], jnp.zeros((MLA_KV_RANK, MLA_HEADS, MLA_VROWS - MLA_V), F32)],
                          axis=-1).reshape(MLA_KV_RANK, -1).T.astype(BF16)
    v_ones = jnp.zeros((MLA_HEADS, MLA_VROWS), F32).at[:, MLA_V].set(1.0).reshape(-1, 1)

    lanevecs = jnp.concatenate([
        _slot([(0, g_q_nope[None]), (MLA_NOPE, g_q_rope[None])], 1),
        _slot([(MLA_NOPE, _swap_halves(g_q_rope)[None])], 1),
        _slot([(0, g_k_nope[None])], 1),
        _slot([(MLA_NOPE, g_k_rope[None])], 1),
        _slot([(MLA_NOPE, _swap_halves(g_k_rope)[None])], 1),
        jnp.zeros((3, SLOT), F32)], axis=0)
    seg = np.zeros((SLOT, SLOT), np.float32)
    seg[:MLA_NOPE, :MLA_NOPE] = 1.0 / MLA_NOPE
    seg[MLA_NOPE:MLA_NOPE + MLA_ROPE, MLA_NOPE:MLA_NOPE + MLA_ROPE] = 1.0 / MLA_ROPE
    bd = jnp.asarray(seg, BF16)

    half = MLA_ROPE // 2
    inv = ROPE_BASE ** (-jnp.arange(half, dtype=F32) / half)
    ang = jnp.arange(s_max, dtype=F32)[:, None] * inv[None, :]
    cos, sin = jnp.cos(ang), jnp.sin(ang)
    pad = jnp.zeros((s_max, SLOT - MLA_NOPE - MLA_ROPE), F32)
    cos_t = jnp.concatenate([jnp.ones((s_max, MLA_NOPE), F32), cos, cos, pad], axis=1)
    sin_t = jnp.concatenate([jnp.zeros((s_max, MLA_NOPE), F32), -sin, sin, pad], axis=1)

    w_route = jnp.zeros((d, LANES), F32).at[:, :N_EXPERTS].set(w_router)
    w_route = w_route.at[:, N_EXPERTS:N_EXPERTS + N_GROUPS].set(w_group)
    b_route = jnp.zeros((1, LANES), F32).at[0, :N_EXPERTS].set(b_router)
    b_route = b_route.at[0, N_EXPERTS:N_EXPERTS + N_GROUPS].set(b_group)

    return dict(
        g1=g_norm1.reshape(1, d), w_in=w_in_ext, w_gk=w_gk, b_gk=b_gk,
        g_qa=g_q_a.reshape(1, -1), w_q=w_q, g_kva=g_kv_a.reshape(1, -1), w_k=w_k, w_v=w_v,
        v_ones=v_ones, bd=bd, lanevecs=lanevecs, colvecs=lanevecs.T, cos=cos_t, sin=sin_t,
        g_gla=g_gla_out.reshape(1, -1), w_gla_o=w_gla_o.astype(BF16), w_mla_o=w_mla_o.astype(BF16),
        w_out=w_out.astype(BF16), g2=g_norm2.reshape(1, d), w_route=w_route, b_route=b_route,
        w_e_gate=w_e_gate, w_e_up=w_e_up, w_e_down=w_e_down)


def _layer(x, mod, wp):
    q, k, v, zg, zga, zgm, lgf, lgb, qm, km, vm = _inproj(x, mod, wp)
    o_f, o_b = _gla(q, k, v, lgf, lgb)
    o_mla = _mla(qm, km, vm)
    x1, h2, route, counts = _merge(o_f, o_b, zg, zga, zgm, o_mla, x, mod, wp)
    return _moe(x1, h2, route, counts, mod, wp)


def kernel(x_prompt, x_sample, c_prompt, c_sample, w_ada, b_ada, g_norm1, w_in, w_gk_fwd, b_gk_fwd, w_gk_bwd, b_gk_bwd, g_gla_out, w_gla_o, g_q_a, w_q_b, g_kv_a, w_kv_b, g_q_nope, g_k_nope, g_q_rope, g_k_rope, w_mla_o, w_out, g_norm2, w_group, b_group, w_router, b_router, w_e_gate, w_e_up, w_e_down):
    assert w_ada.shape[0] == 1, "single-layer trunk"
    s_max = max(x_prompt.shape[1], x_sample.shape[1])
    wp = _prep_weights(s_max, *[p[0] for p in (
        g_norm1, w_in, w_gk_fwd, b_gk_fwd, w_gk_bwd, b_gk_bwd, g_gla_out, w_gla_o, g_q_a, w_q_b, g_kv_a,
        w_kv_b, g_q_nope, g_k_nope, g_q_rope, g_k_rope, w_mla_o, w_out, g_norm2, w_group, b_group,
        w_router, b_router, w_e_gate, w_e_up, w_e_down)])
    n_prompt = x_prompt.shape[0]
    mod = _ada(jnp.concatenate([c_prompt, c_sample], axis=0), w_ada[0], b_ada[0])
    y_prompt = _layer(x_prompt, mod[:n_prompt], wp)
    y_sample = _layer(x_sample, mod[n_prompt:], wp)
    return (y_prompt, y_sample)
```

```python
import functools
import math

import numpy as np
import jax
import jax.numpy as jnp
from jax import lax
from jax.experimental import pallas as pl
from jax.experimental.pallas import tpu as pltpu

F32 = jnp.float32
BF16 = jnp.bfloat16

D_MODEL = 1024
GLA_HEADS, GLA_DK, GLA_DV, GLA_RANK = 4, 128, 256, 16
GLA_GATE_NORM = 16.0
MLA_HEADS, MLA_Q_RANK, MLA_KV_RANK = 8, 256, 128
MLA_NOPE, MLA_ROPE, MLA_V = 64, 32, 64
ROPE_BASE = 10000.0
N_GROUPS, EXPERTS_PER_GROUP, TOP_K, D_EXPERT = 4, 8, 2, 512
N_EXPERTS = N_GROUPS * EXPERTS_PER_GROUP
NORM_EPS = 1e-6

LANES = 128
ROW_CHUNKS = D_MODEL // LANES
SLOT = LANES
VMEM_LIMIT = 56 * 1024 * 1024

INPROJ_TILE = 512
ROW_TILE = 256
GLA_CHUNK = 64
GLA_BLOCK = 256
GLA_EXP_CLAMP = 80.0
MLA_TQ = 512
MLA_VROWS = MLA_V + 16
MLA_UNROLL = 2
MERGE_SPLIT = 2
MOE_TILE = 512
MISC_COLS = MLA_Q_RANK + MLA_KV_RANK + 3 * SLOT


def _cparams(sem):
    return pltpu.CompilerParams(dimension_semantics=sem, vmem_limit_bytes=VMEM_LIMIT)


def _dot(a, b):
    return jnp.dot(a, b, preferred_element_type=F32)


def _dot_nt(a, b):
    return lax.dot_general(a, b, (((1,), (1,)), ((), ())), preferred_element_type=F32)


def _split_bf16(x):
    hi = x.astype(BF16)
    lo = (x - hi.astype(F32)).astype(BF16)
    return hi, lo


def _sigmoid(x):
    return jax.nn.sigmoid(x)


def _silu(x):
    return x * _sigmoid(x)


def _ada_kernel(c_ref, w_ref, b_ref, o_ref):
    a = _silu(c_ref[...])
    ahi, alo = _split_bf16(a)
    w = w_ref[...]
    whi, wlo = _split_bf16(w)
    o_ref[...] = _dot(ahi, whi) + _dot(ahi, wlo) + _dot(alo, whi) + b_ref[...]


def _ada(c, w_ada, b_ada):
    nb = c.shape[0]
    cp = jnp.zeros((8, D_MODEL), F32).at[:nb].set(c)
    tn = 1536
    out = pl.pallas_call(
        _ada_kernel,
        out_shape=jax.ShapeDtypeStruct((8, 6 * D_MODEL), F32),
        grid=(6 * D_MODEL // tn,),
        in_specs=[pl.BlockSpec((8, D_MODEL), lambda j: (0, 0)),
                  pl.BlockSpec((D_MODEL, tn), lambda j: (0, j)),
                  pl.BlockSpec((1, tn), lambda j: (0, j))],
        out_specs=pl.BlockSpec((8, tn), lambda j: (0, j)),
        compiler_params=_cparams(("arbitrary",)),
        name="ada",
    )(cp, w_ada, b_ada.reshape(1, -1))
    mod = out[:nb].reshape(nb, 6, D_MODEL)
    return jnp.concatenate([mod, jnp.zeros((nb, 2, D_MODEL), F32)], axis=1)


_C_Q, _C_K, _C_V, _C_G, _C_GA, _C_GM = 0, 512, 1024, 2048, 3072, 4096
_C_MISC = 5120
IN_EXT = _C_MISC + MISC_COLS


def _segmean(xsq, bd):
    return _dot(xsq.astype(BF16), bd)


def _inproj_kernel(x_ref, mod_ref, g1_ref, win_ref, wgk_ref, bgk_ref, gqa_ref, wq_ref, gkva_ref,
                   wk_ref, wv_ref, vone_ref, bd_ref, lv_ref, cv_ref, cos_ref, sin_ref, cosc_ref, sinc_ref,
                   q_o, k_o, v_o, zg_o, zga_o, zgm_o, lgf_o, lgb_o, qm_o, km_o, vm_o, h_scr):
    x = x_ref[0]
    ms = jnp.mean(x * x, axis=-1, keepdims=True)
    h = (x * lax.rsqrt(ms + NORM_EPS) * g1_ref[...]) * (1.0 + mod_ref[0, 1:2, :]) + mod_ref[0, 0:1, :]
    h_scr[...] = h.astype(BF16)

    def proj(c0, n):
        return _dot(h_scr[...], win_ref[:, c0:c0 + n])

    q_o[0] = (proj(_C_Q, 512) * (GLA_DK ** -0.5)).astype(BF16)
    k_o[0] = proj(_C_K, 512).astype(BF16)
    v_o[0] = proj(_C_V, 1024).astype(BF16)
    zg_o[0] = _silu(proj(_C_G, 1024)).astype(BF16)
    zga_o[0] = _sigmoid(proj(_C_GA, 1024)).astype(BF16)
    zgm_o[0] = _sigmoid(proj(_C_GM, 1024)).astype(BF16)
    misc = proj(_C_MISC, MISC_COLS)

    bd = bd_ref[...]
    cos_t = cos_ref[...]
    sin_t = sin_ref[...]
    g_kn, g_kr, g_krs = lv_ref[2:3, :], lv_ref[3:4, :], lv_ref[4:5, :]
    scale = (MLA_NOPE + MLA_ROPE) ** -0.5 * math.log2(math.e)

    zqa = misc[:, 0:MLA_Q_RANK]
    qa = zqa * lax.rsqrt(jnp.mean(zqa * zqa, axis=-1, keepdims=True) + NORM_EPS) * gqa_ref[...]
    qq = _dot_nt(wq_ref[...], qa.astype(BF16))
    cos_c, sin_c = cosc_ref[...], sinc_ref[...]
    gc_q, gc_qs = cv_ref[:, 0:1], cv_ref[:, 1:2]
    half = MLA_ROPE // 2
    r0, r1, r2 = MLA_NOPE, MLA_NOPE + half, MLA_NOPE + MLA_ROPE
    for hh in range(MLA_HEADS):
        xq = qq[hh * SLOT:(hh + 1) * SLOT, :]
        xs = jnp.concatenate([xq[:r0], xq[r1:r2], xq[r0:r1], xq[r2:]], axis=0)
        r = lax.rsqrt(_dot(bd, (xq * xq).astype(BF16)) + NORM_EPS)
        qm = (xq * gc_q * cos_c + xs * gc_qs * sin_c) * (r * scale)
        qm_o[0, hh * SLOT:(hh + 1) * SLOT, :] = qm.astype(BF16)

    c0 = MLA_Q_RANK
    zkva = misc[:, c0:c0 + MLA_KV_RANK]
    kva = zkva * lax.rsqrt(jnp.mean(zkva * zkva, axis=-1, keepdims=True) + NORM_EPS) * gkva_ref[...]
    kvab = kva.astype(BF16)
    kk = _dot(kvab, wk_ref[...])
    for t in range(vm_o.shape[1]):
        vm_o[0, t] = (_dot_nt(wv_ref[...], kvab[t * ROW_TILE:(t + 1) * ROW_TILE]) + vone_ref[...]).astype(BF16)
    c0 += MLA_KV_RANK
    kr = misc[:, c0:c0 + SLOT]
    krs = misc[:, c0 + SLOT:c0 + 2 * SLOT]
    r_kr = lax.rsqrt(_segmean(kr * kr, bd) + NORM_EPS)
    kpe = (kr * g_kr * cos_t + krs * g_krs * sin_t) * r_kr
    for hh in range(MLA_HEADS):
        xk = kk[:, hh * SLOT:(hh + 1) * SLOT]
        r = lax.rsqrt(_segmean(xk * xk, bd) + NORM_EPS)
        km_o[0, :, hh * SLOT:(hh + 1) * SLOT] = (xk * r * g_kn + kpe).astype(BF16)

    c0 += 2 * SLOT
    zgate = misc[:, c0:c0 + SLOT].astype(BF16)
    pre = _dot(zgate, wgk_ref[...]) + bgk_ref[...]
    lg = (jnp.minimum(pre, 0.0) - jnp.log(1.0 + jnp.exp(-jnp.abs(pre)))) * (1.0 / GLA_GATE_NORM)
    lgf_o[0] = lg[:, :GLA_HEADS * GLA_DK]
    lgb_o[0] = lg[:, GLA_HEADS * GLA_DK:]


def _inproj(x, mod, wp):
    nb, s, d = x.shape
    tm = min(INPROJ_TILE, s)
    kt = min(ROW_TILE, s)
    const = lambda b, i: (0, 0)
    row = lambda b, i: (b, i, 0)

    def full(a):
        return pl.BlockSpec(a.shape, const)

    def out(n, dt):
        return jax.ShapeDtypeStruct((nb, s, n), dt), pl.BlockSpec((1, tm, n), row)

    outs = [out(512, BF16), out(512, BF16), out(1024, BF16), out(1024, BF16), out(1024, BF16),
            out(1024, BF16), out(512, F32), out(512, F32),
            (jax.ShapeDtypeStruct((nb, MLA_HEADS * SLOT, s), BF16),
             pl.BlockSpec((1, MLA_HEADS * SLOT, tm), lambda b, i: (b, 0, i))),
            out(MLA_HEADS * SLOT, BF16),
            (jax.ShapeDtypeStruct((nb, s // kt, MLA_HEADS * MLA_VROWS, kt), BF16),
             pl.BlockSpec((1, tm // kt, MLA_HEADS * MLA_VROWS, kt), lambda b, i: (b, i, 0, 0)))]
    return pl.pallas_call(
        _inproj_kernel,
        out_shape=[o[0] for o in outs],
        grid=(nb, s // tm),
        in_specs=[pl.BlockSpec((1, tm, d), row),
                  pl.BlockSpec((1, 8, d), lambda b, i: (b, 0, 0)),
                  full(wp["g1"]), full(wp["w_in"]), full(wp["w_gk"]), full(wp["b_gk"]),
                  full(wp["g_qa"]), full(wp["w_q"]), full(wp["g_kva"]), full(wp["w_k"]), full(wp["w_v"]),
                  full(wp["v_ones"]), full(wp["bd"]), full(wp["lanevecs"]), full(wp["colvecs"]),
                  pl.BlockSpec((tm, SLOT), lambda b, i: (i, 0)),
                  pl.BlockSpec((tm, SLOT), lambda b, i: (i, 0)),
                  pl.BlockSpec((SLOT, tm), lambda b, i: (0, i)),
                  pl.BlockSpec((SLOT, tm), lambda b, i: (0, i))],
        out_specs=[o[1] for o in outs],
        scratch_shapes=[pltpu.VMEM((tm, d), BF16)],
        compiler_params=_cparams(("arbitrary", "arbitrary")),
        name="inproj",
    )(x, mod, wp["g1"], wp["w_in"], wp["w_gk"], wp["b_gk"], wp["g_qa"], wp["w_q"], wp["g_kva"],
      wp["w_k"], wp["w_v"], wp["v_ones"], wp["bd"], wp["lanevecs"], wp["colvecs"],
      wp["cos"][:s], wp["sin"][:s], wp["cos"][:s].T, wp["sin"][:s].T)


def _dot_tn(a, b):
    return lax.dot_general(a, b, (((0,), (0,)), ((), ())), preferred_element_type=F32)


def _gla_direction(q_ref, k_ref, v_ref, g_ref, o_ref, st_ref, scr, cum, mask, reverse):
    qin_s, qmid_s, kmid_s, kout_s, tot_s, u_s = scr
    c = GLA_CHUNK
    nch = q_ref.shape[1] // c
    rows = [slice(j * c, (j + 1) * c) for j in range(nch)]
    ks = [slice(h * GLA_DK, (h + 1) * GLA_DK) for h in range(GLA_HEADS)]
    vs = [slice(h * GLA_DV, (h + 1) * GLA_DV) for h in range(GLA_HEADS)]

    for j in range(nch):
        hi, lo = _split_bf16(g_ref[0, rows[j], :])
        b = _dot(cum, hi) + _dot(cum, lo)
        mid = b[c // 2:c // 2 + 1, :]
        tot = b[0:1, :] if reverse else b[c - 1:c, :]
        q = q_ref[0, rows[j], :].astype(F32)
        k = k_ref[0, rows[j], :].astype(F32)
        qin_s[rows[j], :] = (q * jnp.exp(b)).astype(BF16)
        qmid_s[rows[j], :] = (q * jnp.exp(jnp.minimum(b - mid, GLA_EXP_CLAMP))).astype(BF16)
        kmid_s[rows[j], :] = (k * jnp.exp(jnp.minimum(mid - b, GLA_EXP_CLAMP))).astype(BF16)
        kout_s[rows[j], :] = (k * jnp.exp(tot - b)).astype(BF16)
        tot_s[j:j + 1, :] = tot

    attn = {}
    for j in range(nch):
        for h in range(GLA_HEADS):
            a = _dot_nt(qmid_s[rows[j], ks[h]], kmid_s[rows[j], ks[h]])
            attn[j, h] = jnp.where(mask, a, 0.0).astype(BF16)

    for j in range(nch):
        for h in range(GLA_HEADS):
            v = v_ref[0, rows[j], vs[h]]
            o_ref[0, rows[j], vs[h]] = _dot(attn[j, h], v)
            u_s[j * GLA_HEADS + h] = _dot_tn(kout_s[rows[j], ks[h]], v)

    for j in (range(nch - 1, -1, -1) if reverse else range(nch)):
        for h in range(GLA_HEADS):
            st = st_ref[h]
            o_ref[0, rows[j], vs[h]] += _dot(qin_s[rows[j], ks[h]], st.astype(BF16))
            tot = jnp.broadcast_to(tot_s[j:j + 1, ks[h]], (GLA_DK, GLA_DK))
            dec = jnp.exp(tot.T)
            st_ref[h] = jnp.concatenate([dec] * (GLA_DV // GLA_DK), axis=1) * st + u_s[j * GLA_HEADS + h]


def _gla_kernel(qf, kf, vf, gf, qb, kb, vb, gb, of_ref, ob_ref, sf_ref, sb_ref, *scr):
    @pl.when(pl.program_id(1) == 0)
    def _():
        sf_ref[...] = jnp.zeros_like(sf_ref)
        sb_ref[...] = jnp.zeros_like(sb_ref)

    c = GLA_CHUNK
    ri = lax.broadcasted_iota(jnp.int32, (c, c), 0)
    ci = lax.broadcasted_iota(jnp.int32, (c, c), 1)
    cum_f = jnp.where(ri >= ci, 1.0, 0.0).astype(BF16)
    cum_b = jnp.where(ci >= ri, 1.0, 0.0).astype(BF16)
    n = len(scr) // 2
    _gla_direction(qf, kf, vf, gf, of_ref, sf_ref, scr[:n], cum_f, ri >= ci, False)
    _gla_direction(qb, kb, vb, gb, ob_ref, sb_ref, scr[n:], cum_b, ci > ri, True)


def _gla(q, k, v, lgf, lgb):
    nb, s, _ = q.shape
    cb = min(GLA_BLOCK, s)
    ns = s // cb
    nch = cb // GLA_CHUNK
    fwd = lambda b, i: (b, i, 0)
    bwd = lambda b, i: (b, ns - 1 - i, 0)
    hk, hv = GLA_HEADS * GLA_DK, GLA_HEADS * GLA_DV

    def specs(im):
        return [pl.BlockSpec((1, cb, hk), im), pl.BlockSpec((1, cb, hk), im),
                pl.BlockSpec((1, cb, hv), im), pl.BlockSpec((1, cb, hk), im)]

    per_dir = [pltpu.VMEM((cb, hk), BF16)] * 4 + [pltpu.VMEM((8, hk), F32),
                                                  pltpu.VMEM((nch * GLA_HEADS, GLA_DK, GLA_DV), F32)]
    return pl.pallas_call(
        _gla_kernel,
        out_shape=[jax.ShapeDtypeStruct((nb, s, hv), F32)] * 2,
        grid=(nb, ns),
        in_specs=specs(fwd) + specs(bwd),
        out_specs=[pl.BlockSpec((1, cb, hv), fwd), pl.BlockSpec((1, cb, hv), bwd)],
        scratch_shapes=[pltpu.VMEM((GLA_HEADS, GLA_DK, GLA_DV), F32)] * 2 + per_dir * 2,
        compiler_params=_cparams(("arbitrary", "arbitrary")),
        name="gla",
    )(q, k, v, lgf, q, k, v, lgb)


def _mla_kernel(q_ref, k_ref, vt_ref, o_ref, sc0, sc1, p0, p1):
    tq = q_ref.shape[2]
    nk = vt_ref.shape[1]
    tk = vt_ref.shape[3]
    gu = MLA_UNROLL
    ng = nk // gu
    sc, pp = (sc0, sc1), (p0, p1)
    qs = [q_ref[0, hh * SLOT:(hh + 1) * SLOT, :] for hh in range(2)]
    tiles = [(u, hh) for u in range(gu) for hh in range(2)]

    def score_tile(g, slot, u, hh):
        r0 = pl.multiple_of((g * gu + u) * tk, tk)
        kt = k_ref[0, pl.ds(r0, tk), hh * SLOT:(hh + 1) * SLOT]
        st = _dot(kt, qs[hh])
        sc[slot][2 * u + hh] = st
        return jnp.max(st, axis=0, keepdims=True)

    def softmax_tile(slot, u, hh, m, tmax):
        m_new = jnp.maximum(m, tmax)
        pp[slot][2 * u + hh] = jnp.exp2(sc[slot][2 * u + hh] - m_new).astype(BF16)
        return m_new, jnp.exp2(m - m_new)

    def value_tile(g, slot, u, hh, alpha, acc):
        vt = vt_ref[0, g * gu + u, hh * MLA_VROWS:(hh + 1) * MLA_VROWS, :]
        return alpha * acc + _dot(vt, pp[slot][2 * u + hh])

    def issue_scores(g, slot):
        return tuple(score_tile(g, slot, u, hh) for u, hh in tiles)

    def softmax(slot, ms, tmax):
        ms, alphas = list(ms), []
        for u, hh in tiles:
            ms[hh], alpha = softmax_tile(slot, u, hh, ms[hh], tmax[2 * u + hh])
            alphas.append(alpha)
        return tuple(ms), tuple(alphas)

    def values(g, slot, alphas, accs):
        accs = list(accs)
        for u, hh in tiles:
            accs[hh] = value_tile(g, slot, u, hh, alphas[2 * u + hh], accs[hh])
        return tuple(accs)

    def step(g, slot, carry, more_scores):
        ms, alphas, accs, tmax = carry
        ms, accs, new_alphas, new_tmax = list(ms), list(accs), [], []
        for u, hh in tiles:
            idx = 2 * u + hh
            accs[hh] = value_tile(g - 1, 1 - slot, u, hh, alphas[idx], accs[hh])
            ms[hh], alpha = softmax_tile(slot, u, hh, ms[hh], tmax[idx])
            new_alphas.append(alpha)
            if more_scores:
                new_tmax.append(score_tile(g + 1, 1 - slot, u, hh))
        return tuple(ms), tuple(new_alphas), tuple(accs), tuple(new_tmax)

    def double_step(i, carry):
        g = 2 * i + 1
        carry = step(g, 1, carry, True)
        return step(g + 1, 0, carry, True)

    m0 = jnp.full((1, tq), -jnp.inf, F32)
    acc0 = jnp.zeros((MLA_VROWS, tq), F32)
    tmax = issue_scores(0, 0)
    ms, alphas = softmax(0, (m0, m0), tmax)
    tmax = issue_scores(1, 1)
    carry = (ms, alphas, (acc0, acc0), tmax)
    for i in range((ng - 2) // 2):
        carry = double_step(i, carry)
    ms, alphas, accs, _ = step(ng - 1, 1, carry, False)
    accs = values(ng - 1, 1, alphas, accs)
    outs = [(a[:MLA_V] / a[MLA_V:MLA_V + 1]).T for a in accs]
    o_ref[0] = jnp.concatenate(outs, axis=-1).astype(BF16)


def _mla(qm, km, vmt):
    nb, s, _ = km.shape
    nk, tk = vmt.shape[1], vmt.shape[3]
    tq = min(MLA_TQ, s)
    assert (nk // MLA_UNROLL) % 2 == 0 and nk % MLA_UNROLL == 0
    slot_tiles = 2 * MLA_UNROLL
    return pl.pallas_call(
        _mla_kernel,
        out_shape=jax.ShapeDtypeStruct((nb, s, MLA_HEADS * MLA_V), BF16),
        grid=(nb, MLA_HEADS // 2, s // tq),
        in_specs=[pl.BlockSpec((1, 2 * SLOT, tq), lambda b, p, i: (b, p, i)),
                  pl.BlockSpec((1, s, 2 * SLOT), lambda b, p, i: (b, 0, p)),
                  pl.BlockSpec((1, nk, 2 * MLA_VROWS, tk), lambda b, p, i: (b, 0, p, 0))],
        out_specs=pl.BlockSpec((1, tq, 2 * MLA_V), lambda b, p, i: (b, i, p)),
        scratch_shapes=[pltpu.VMEM((slot_tiles, tk, tq), F32)] * 2 + [pltpu.VMEM((slot_tiles, tk, tq), BF16)] * 2,
        compiler_params=_cparams(("arbitrary", "arbitrary", "arbitrary")),
        name="mla",
    )(qm, km, vmt)


def _merge_kernel(of_ref, ob_ref, zg_ref, zga_ref, zgm_ref, om_ref, x_ref, mod_ref, ggla_ref, wglo_ref,
                  wmo_ref, wout_ref, g2_ref, wr_ref, br_ref, ltri_ref,
                  x1_o, h2_o, route_o, cnt_o, cnt_scr):
    first = (pl.program_id(0) == 0) & (pl.program_id(1) == 0)

    @pl.when(first)
    def _():
        cnt_scr[...] = jnp.zeros_like(cnt_scr)

    tm = x_ref.shape[1]
    hm = tm // MERGE_SPLIT
    subs = [slice(i * hm, (i + 1) * hm) for i in range(MERGE_SPLIT)]

    def gla_gate(rs):
        o = of_ref[0, rs, :] + ob_ref[0, rs, :]
        zg = zg_ref[0, rs, :].astype(F32)
        parts = []
        for hh in range(GLA_HEADS):
            seg = o[:, hh * GLA_DV:(hh + 1) * GLA_DV]
            r = lax.rsqrt(jnp.mean(seg * seg, axis=-1, keepdims=True) + NORM_EPS)
            parts.append((seg * r * ggla_ref[...]) * zg[:, hh * GLA_DV:(hh + 1) * GLA_DV])
        return jnp.concatenate(parts, axis=-1).astype(BF16)

    ogs = [gla_gate(rs) for rs in subs]
    ys = [(_dot(og, wglo_ref[...]), _dot(om_ref[0, rs, :], wmo_ref[...])) for og, rs in zip(ogs, subs)]
    merged = [(zga_ref[0, rs, :].astype(F32) * y_gla
               + zgm_ref[0, rs, :].astype(F32) * y_mla).astype(BF16) for (y_gla, y_mla), rs in zip(ys, subs)]
    mixes = [_dot(m, wout_ref[...]) for m in merged]

    w = wr_ref[...]
    whi, wlo = _split_bf16(w)
    splits = []
    for mix, rs in zip(mixes, subs):
        x1 = x_ref[0, rs, :] + mod_ref[0, 2:3, :] * mix
        x1_o[0, rs, :] = x1
        ms = jnp.mean(x1 * x1, axis=-1, keepdims=True)
        h2 = (x1 * lax.rsqrt(ms + NORM_EPS) * g2_ref[...]) * (1.0 + mod_ref[0, 4:5, :]) + mod_ref[0, 3:4, :]
        for c in range(ROW_CHUNKS):
            h2_o[pl.ds(rs.start * ROW_CHUNKS + c, hm, stride=ROW_CHUNKS), :] = h2[:, c * LANES:(c + 1) * LANES]
        splits.append(_split_bf16(h2))

    logit_parts = [_dot(hhi, whi) + _dot(hhi, wlo) + _dot(hlo, whi) + br_ref[...] for hhi, hlo in splits]
    logits = jnp.concatenate(logit_parts, axis=0)
    lane = lax.broadcasted_iota(jnp.int32, (tm, LANES), 1)
    neg = -jnp.inf
    is_g = (lane >= N_EXPERTS) & (lane < N_EXPERTS + N_GROUPS)
    lgm = jnp.where(is_g, logits, neg)
    mg = jnp.max(lgm, axis=-1, keepdims=True)
    sg = jnp.sum(jnp.exp(lgm - mg), axis=-1, keepdims=True)
    p_top = 1.0 / sg
    g_idx = jnp.min(jnp.where(lgm == mg, lane, 2 * LANES), axis=-1, keepdims=True) - N_EXPERTS
    in_grp = (lane >= g_idx * EXPERTS_PER_GROUP) & (lane < (g_idx + 1) * EXPERTS_PER_GROUP)
    el = jnp.where(in_grp, logits, neg)
    m1 = jnp.max(el, axis=-1, keepdims=True)
    e1 = jnp.min(jnp.where(el == m1, lane, 2 * LANES), axis=-1, keepdims=True)
    el2 = jnp.where(lane == e1, neg, el)
    m2 = jnp.max(el2, axis=-1, keepdims=True)
    e2 = jnp.min(jnp.where(el2 == m2, lane, 2 * LANES), axis=-1, keepdims=True)
    se = jnp.sum(jnp.exp(el - m1), axis=-1, keepdims=True)
    pv1 = 1.0 / se
    pv2 = jnp.exp(m2 - m1) / se
    w1 = p_top * pv1 / (pv1 + pv2)
    w2 = p_top * pv2 / (pv1 + pv2)

    oh1 = lane == e1
    oh2 = lane == e2
    oh = jnp.where(oh1 | oh2, 1.0, 0.0)
    prefix = _dot(ltri_ref[...], oh.astype(BF16)) + cnt_scr[...]
    r1 = jnp.sum(jnp.where(oh1, prefix, 0.0), axis=-1, keepdims=True)
    r2 = jnp.sum(jnp.where(oh2, prefix, 0.0), axis=-1, keepdims=True)
    cnt_scr[...] = cnt_scr[...] + jnp.sum(oh, axis=0, keepdims=True)
    cnt_o[...] = jnp.broadcast_to(cnt_scr[...], cnt_o.shape)

    route = jnp.where(lane == 0, e1.astype(F32), 0.0)
    route = jnp.where(lane == 1, e2.astype(F32), route)
    route = jnp.where(lane == 2, r1, route)
    route = jnp.where(lane == 3, r2, route)
    route = jnp.where(lane == 4, w1, route)
    route = jnp.where(lane == 5, w2, route)
    route_o[0] = route


def _merge(o_f, o_b, zg, zga, zgm, o_mla, x, mod, wp):
    nb, s, d = x.shape
    tm = min(ROW_TILE, s)
    const = lambda b, i: (0, 0)
    row = lambda b, i: (b, i, 0)

    def full(a):
        return pl.BlockSpec(a.shape, const)

    def rows(n):
        return pl.BlockSpec((1, tm, n), row)

    ltri = jnp.asarray(np.tril(np.ones((tm, tm), np.float32), -1), BF16)
    return pl.pallas_call(
        _merge_kernel,
        out_shape=[jax.ShapeDtypeStruct((nb, s, d), F32), jax.ShapeDtypeStruct((nb * s * ROW_CHUNKS, LANES), F32),
                   jax.ShapeDtypeStruct((nb, s, LANES), F32), jax.ShapeDtypeStruct((8, LANES), F32)],
        grid=(nb, s // tm),
        in_specs=[rows(1024), rows(1024), rows(1024), rows(1024), rows(1024), rows(MLA_HEADS * MLA_V),
                  rows(d), pl.BlockSpec((1, 8, d), lambda b, i: (b, 0, 0)),
                  full(wp["g_gla"]), full(wp["w_gla_o"]), full(wp["w_mla_o"]), full(wp["w_out"]),
                  full(wp["g2"]), full(wp["w_route"]), full(wp["b_route"]), full(ltri)],
        out_specs=[rows(d), pl.BlockSpec((tm * ROW_CHUNKS, LANES), lambda b, i: (b * (s // tm) + i, 0)),
                   rows(LANES), pl.BlockSpec((8, LANES), const)],
        scratch_shapes=[pltpu.VMEM((1, LANES), F32)],
        compiler_params=_cparams(("arbitrary", "arbitrary")),
        name="merge",
    )(o_f, o_b, zg, zga, zgm, o_mla, x, mod, wp["g_gla"], wp["w_gla_o"], wp["w_mla_o"], wp["w_out"],
      wp["g2"], wp["w_route"], wp["b_route"], ltri)


def _dispatch_kernel(seg_ref, dest_ref, h_ref, xs_out, zero_scr, sem):
    tt = h_ref.shape[0] // ROW_CHUNKS
    tile_rows = MOE_TILE * ROW_CHUNKS

    @pl.when(pl.program_id(0) == 0)
    def _():
        zero_scr[...] = jnp.zeros_like(zero_scr)

        def zero_copy(e):
            r0 = pl.multiple_of((seg_ref[0, e] - MOE_TILE) * ROW_CHUNKS, tile_rows)
            return pltpu.make_async_copy(zero_scr, xs_out.at[pl.ds(r0, tile_rows)], sem.at[1])

        for e in range(N_EXPERTS):
            @pl.when(seg_ref[1, e] > 0)
            def _():
                zero_copy(e).start()
        for e in range(N_EXPERTS):
            @pl.when(seg_ref[1, e] > 0)
            def _():
                zero_copy(e).wait()

        def tail_copy(ti):
            return pltpu.make_async_copy(zero_scr, xs_out.at[pl.ds(pl.multiple_of(ti * tile_rows, tile_rows), tile_rows)],
                                         sem.at[1])

        first_tail = seg_ref[0, N_EXPERTS - 1] // MOE_TILE
        n_tiles = xs_out.shape[0] // tile_rows
        lax.fori_loop(first_tail, n_tiles, lambda ti, c: (tail_copy(ti).start(), c)[1], 0)
        lax.fori_loop(first_tail, n_tiles, lambda ti, c: (tail_copy(ti).wait(), c)[1], 0)

    def copy(r, slot):
        d0 = pl.multiple_of(dest_ref[0, 0, 2 * r + slot] * ROW_CHUNKS, ROW_CHUNKS)
        return pltpu.make_async_copy(h_ref.at[pl.ds(pl.multiple_of(r * ROW_CHUNKS, ROW_CHUNKS), ROW_CHUNKS)],
                                     xs_out.at[pl.ds(d0, ROW_CHUNKS)], sem.at[0])

    def start(r, c):
        copy(r, 0).start(priority=0)
        copy(r, 1).start(priority=1)
        return c

    def wait(r, c):
        copy(r, 0).wait()
        copy(r, 1).wait()
        return c

    lax.fori_loop(0, tt, start, 0, unroll=8)
    lax.fori_loop(0, tt, wait, 0, unroll=8)


def _dispatch(h2, dest, segments, n_rows):
    t = h2.shape[0] // ROW_CHUNKS
    tt = min(ROW_TILE, t)
    nt = t // tt
    return pl.pallas_call(
        _dispatch_kernel,
        out_shape=jax.ShapeDtypeStruct((n_rows * ROW_CHUNKS, LANES), F32),
        grid_spec=pltpu.PrefetchScalarGridSpec(
            num_scalar_prefetch=1,
            grid=(nt,),
            in_specs=[pl.BlockSpec((1, 1, 2 * tt), lambda i, seg: (i, 0, 0), memory_space=pltpu.SMEM),
                      pl.BlockSpec((tt * ROW_CHUNKS, LANES), lambda i, seg: (i, 0))],
            out_specs=pl.BlockSpec(memory_space=pl.ANY),
            scratch_shapes=[pltpu.VMEM((MOE_TILE * ROW_CHUNKS, LANES), F32), pltpu.SemaphoreType.DMA((2,))]),
        compiler_params=_cparams(("arbitrary",)),
        name="dispatch",
    )(segments, dest.reshape(nt, 1, 2 * tt), h2)


def _expert_kernel(te_ref, na_ref, xs_ref, wg_ref, wu_ref, wd_ref, ys_ref, xb_scr):
    active = pl.program_id(0) < na_ref[0]

    @pl.when(active)
    def _():
        for c in range(ROW_CHUNKS):
            xb_scr[:, c * LANES:(c + 1) * LANES] = xs_ref[pl.ds(c, MOE_TILE, stride=ROW_CHUNKS), :].astype(BF16)
        xb = xb_scr[...]
        a = _silu(_dot(xb, wg_ref[0].astype(BF16))) * _dot(xb, wu_ref[0].astype(BF16))
        y = _dot(a.astype(BF16), wd_ref[0].astype(BF16))
        for c in range(ROW_CHUNKS):
            ys_ref[pl.ds(c, MOE_TILE, stride=ROW_CHUNKS), :] = y[:, c * LANES:(c + 1) * LANES]

    @pl.when(jnp.logical_not(active))
    def _():
        ys_ref[...] = jnp.zeros_like(ys_ref)


def _experts(xs, tile_expert, n_active, wp):
    d = D_MODEL
    n_rows = xs.shape[0] // ROW_CHUNKS
    nt = n_rows // MOE_TILE
    rows = MOE_TILE * ROW_CHUNKS

    def tile(i, te, na):
        return (jnp.minimum(i, na[0] - 1), 0)

    def wmap(i, te, na):
        return (te[jnp.minimum(i, na[0] - 1)], 0, 0)

    return pl.pallas_call(
        _expert_kernel,
        out_shape=jax.ShapeDtypeStruct((n_rows * ROW_CHUNKS, LANES), F32),
        grid_spec=pltpu.PrefetchScalarGridSpec(
            num_scalar_prefetch=2,
            grid=(nt,),
            in_specs=[pl.BlockSpec((rows, LANES), tile),
                      pl.BlockSpec((1, d, D_EXPERT), wmap),
                      pl.BlockSpec((1, d, D_EXPERT), wmap),
                      pl.BlockSpec((1, D_EXPERT, d), wmap)],
            out_specs=pl.BlockSpec((rows, LANES), lambda i, te, na: (i, 0)),
            scratch_shapes=[pltpu.VMEM((MOE_TILE, d), BF16)]),
        compiler_params=_cparams(("arbitrary",)),
        name="experts",
    )(tile_expert, n_active, xs, wp["w_e_gate"], wp["w_e_up"], wp["w_e_down"])


def _combine_kernel(dest_ref, x1_ref, rw_ref, ga_ref, ys_hbm, o_ref, buf, sem):
    tt = x1_ref.shape[0]
    i = pl.program_id(0)
    nt = pl.num_programs(0) - 1
    par = lax.rem(i, 2)

    def copy(p, r, slot, d0):
        return pltpu.make_async_copy(ys_hbm.at[pl.ds(d0, ROW_CHUNKS)],
                                     buf.at[p, slot, pl.ds(pl.multiple_of(r * ROW_CHUNKS, ROW_CHUNKS), ROW_CHUNKS)],
                                     sem.at[p])

    @pl.when(i < nt)
    def _():
        def start(r, c):
            for slot in range(TOP_K):
                d0 = pl.multiple_of(dest_ref[0, 0, 2 * r + slot] * ROW_CHUNKS, ROW_CHUNKS)
                copy(par, r, slot, d0).start(priority=slot)
            return c

        lax.fori_loop(0, tt, start, 0, unroll=8)

    @pl.when(i > 0)
    def _():
        def wait(r, c):
            for slot in range(TOP_K):
                copy(1 - par, r, slot, 0).wait()
            return c

        lax.fori_loop(0, tt, wait, 0, unroll=8)
        rw = rw_ref[...]
        for c in range(ROW_CHUNKS):
            cs = slice(c * LANES, (c + 1) * LANES)
            rows = pl.ds(c, tt, stride=ROW_CHUNKS)
            moe = buf[1 - par, 0, rows, :] * rw[:, 4:5] + buf[1 - par, 1, rows, :] * rw[:, 5:6]
            o_ref[:, cs] = x1_ref[:, cs] + ga_ref[0, 5:6, cs] * moe


def _combine(x1, route, mod, ys, dest, s):
    t, d = x1.shape
    tt = min(ROW_TILE, s)
    nt = t // tt
    per_b = s // tt
    lag = lambda i: jnp.maximum(i - 1, 0)
    return pl.pallas_call(
        _combine_kernel,
        out_shape=jax.ShapeDtypeStruct((t, d), F32),
        grid=(nt + 1,),
        in_specs=[pl.BlockSpec((1, 1, 2 * tt), lambda i: (jnp.minimum(i, nt - 1), 0, 0), memory_space=pltpu.SMEM),
                  pl.BlockSpec((tt, d), lambda i: (lag(i), 0)),
                  pl.BlockSpec((tt, LANES), lambda i: (lag(i), 0)),
                  pl.BlockSpec((1, 8, d), lambda i: (lag(i) // per_b, 0, 0)),
                  pl.BlockSpec(memory_space=pl.ANY)],
        out_specs=pl.BlockSpec((tt, d), lambda i: (lag(i), 0)),
        scratch_shapes=[pltpu.VMEM((2, TOP_K, tt * ROW_CHUNKS, LANES), F32), pltpu.SemaphoreType.DMA((2,))],
        compiler_params=_cparams(("arbitrary",)),
        name="combine",
    )(dest.reshape(nt, 1, 2 * tt), x1, route, mod, ys)


def _moe(x1, h2, route, counts, mod, wp):
    nb, s, d = x1.shape
    t = nb * s
    route2 = route.reshape(t, LANES)
    e_flat = route2[:, 0:2].astype(jnp.int32)
    rank = route2[:, 2:4].astype(jnp.int32)
    cnt = counts[0, :N_EXPERTS].astype(jnp.int32)
    padded = (cnt + MOE_TILE - 1) // MOE_TILE * MOE_TILE
    pad_end = jnp.cumsum(padded)
    pad_start = pad_end - padded
    is_e = e_flat[..., None] == jnp.arange(N_EXPERTS, dtype=jnp.int32)
    dest = (jnp.sum(jnp.where(is_e, pad_start, 0), axis=-1) + rank).reshape(t * TOP_K)
    n_rows = t * TOP_K + N_EXPERTS * MOE_TILE
    nt = n_rows // MOE_TILE
    tile_row = jnp.arange(nt, dtype=jnp.int32) * MOE_TILE
    tile_expert = jnp.minimum(jnp.sum((pad_end[None, :] <= tile_row[:, None]).astype(jnp.int32), axis=1),
                              N_EXPERTS - 1)
    n_active = (pad_end[-1:] // MOE_TILE).astype(jnp.int32)
    xs = _dispatch(h2, dest, jnp.stack([pad_end, padded]).astype(jnp.int32), n_rows)
    ys = _experts(xs, tile_expert, n_active, wp)
    out = _combine(x1.reshape(t, d), route2, mod, ys, dest, s)
    return out.reshape(nb, s, d)


def _swap_halves(a, axis=-1):
    n = a.shape[axis] // 2
    lo, hi = jnp.split(a, [n], axis=axis)
    return jnp.concatenate([hi, lo], axis=axis)


def _slot(parts, n_lead):
    out = jnp.zeros((n_lead, SLOT), F32)
    for off, a in parts:
        out = out.at[:, off:off + a.shape[-1]].set(a)
    return out


def _prep_weights(s_max, g_norm1, w_in, w_gk_fwd, b_gk_fwd, w_gk_bwd, b_gk_bwd, g_gla_out, w_gla_o, g_q_a,
                  w_q_b, g_kv_a, w_kv_b, g_q_nope, g_k_nope, g_q_rope, g_k_rope, w_mla_o, w_out, g_norm2,
                  w_group, b_group, w_router, b_router, w_e_gate, w_e_up, w_e_down):
    d = D_MODEL
    hk, hv = GLA_HEADS * GLA_DK, GLA_HEADS * GLA_DV
    offs = np.cumsum([0, hk, hk, hv, hv, GLA_RANK, GLA_RANK, MLA_Q_RANK, MLA_KV_RANK, MLA_ROPE, d, d])
    cols = [w_in[:, offs[i]:offs[i + 1]] for i in range(11)]
    wq_, wk_, wv_, wg_, wgf, wgb, wqa, wkva, wkr, wga, wgm = cols
    kr_slot = _slot([(MLA_NOPE, wkr)], d)
    krs_slot = _slot([(MLA_NOPE, _swap_halves(wkr))], d)
    gate_slot = _slot([(0, wgf), (GLA_RANK, wgb)], d)
    w_in_ext = jnp.concatenate([wq_, wk_, wv_, wg_, wga, wgm, wqa, wkva, kr_slot, krs_slot, gate_slot],
                               axis=1).astype(BF16)
    w_gk = jnp.zeros((SLOT, 2 * hk), F32)
    w_gk = w_gk.at[0:GLA_RANK, 0:hk].set(w_gk_fwd).at[GLA_RANK:2 * GLA_RANK, hk:].set(w_gk_bwd).astype(BF16)
    b_gk = jnp.concatenate([b_gk_fwd, b_gk_bwd]).reshape(1, -1)

    wq3 = w_q_b.reshape(MLA_Q_RANK, MLA_HEADS, MLA_NOPE + MLA_ROPE)
    zq = jnp.zeros((MLA_Q_RANK, MLA_HEADS, SLOT - MLA_NOPE - MLA_ROPE), F32)
    w_q = jnp.concatenate([wq3, zq], axis=-1).reshape(MLA_Q_RANK, -1).T.astype(BF16)
    wkv3 = w_kv_b.reshape(MLA_KV_RANK, MLA_HEADS, MLA_NOPE + MLA_V)
    w_k = jnp.concatenate([wkv3[..., :MLA_NOPE], jnp.zeros((MLA_KV_RANK, MLA_HEADS, SLOT - MLA_NOPE), F32)],
                          axis=-1).reshape(MLA_KV_RANK, -1).astype(BF16)
    w_v = jnp.concatenate([wkv3[..., MLA_NOPE:], jnp.zeros((MLA_KV_RANK, MLA_HEADS, MLA_VROWS - MLA_V), F32)],
                          axis=-1).reshape(MLA_KV_RANK, -1).T.astype(BF16)
    v_ones = jnp.zeros((MLA_HEADS, MLA_VROWS), F32).at[:, MLA_V].set(1.0).reshape(-1, 1)

    lanevecs = jnp.concatenate([
        _slot([(0, g_q_nope[None]), (MLA_NOPE, g_q_rope[None])], 1),
        _slot([(MLA_NOPE, _swap_halves(g_q_rope)[None])], 1),
        _slot([(0, g_k_nope[None])], 1),
        _slot([(MLA_NOPE, g_k_rope[None])], 1),
        _slot([(MLA_NOPE, _swap_halves(g_k_rope)[None])], 1),
        jnp.zeros((3, SLOT), F32)], axis=0)
    seg = np.zeros((SLOT, SLOT), np.float32)
    seg[:MLA_NOPE, :MLA_NOPE] = 1.0 / MLA_NOPE
    seg[MLA_NOPE:MLA_NOPE + MLA_ROPE, MLA_NOPE:MLA_NOPE + MLA_ROPE] = 1.0 / MLA_ROPE
    bd = jnp.asarray(seg, BF16)

    half = MLA_ROPE // 2
    inv = ROPE_BASE ** (-jnp.arange(half, dtype=F32) / half)
    ang = jnp.arange(s_max, dtype=F32)[:, None] * inv[None, :]
    cos, sin = jnp.cos(ang), jnp.sin(ang)
    pad = jnp.zeros((s_max, SLOT - MLA_NOPE - MLA_ROPE), F32)
    cos_t = jnp.concatenate([jnp.ones((s_max, MLA_NOPE), F32), cos, cos, pad], axis=1)
    sin_t = jnp.concatenate([jnp.zeros((s_max, MLA_NOPE), F32), -sin, sin, pad], axis=1)

    w_route = jnp.zeros((d, LANES), F32).at[:, :N_EXPERTS].set(w_router)
    w_route = w_route.at[:, N_EXPERTS:N_EXPERTS + N_GROUPS].set(w_group)
    b_route = jnp.zeros((1, LANES), F32).at[0, :N_EXPERTS].set(b_router)
    b_route = b_route.at[0, N_EXPERTS:N_EXPERTS + N_GROUPS].set(b_group)

    return dict(
        g1=g_norm1.reshape(1, d), w_in=w_in_ext, w_gk=w_gk, b_gk=b_gk,
        g_qa=g_q_a.reshape(1, -1), w_q=w_q, g_kva=g_kv_a.reshape(1, -1), w_k=w_k, w_v=w_v,
        v_ones=v_ones, bd=bd, lanevecs=lanevecs, colvecs=lanevecs.T, cos=cos_t, sin=sin_t,
        g_gla=g_gla_out.reshape(1, -1), w_gla_o=w_gla_o.astype(BF16), w_mla_o=w_mla_o.astype(BF16),
        w_out=w_out.astype(BF16), g2=g_norm2.reshape(1, d), w_route=w_route, b_route=b_route,
        w_e_gate=w_e_gate, w_e_up=w_e_up, w_e_down=w_e_down)


def _layer(x, mod, wp):
    q, k, v, zg, zga, zgm, lgf, lgb, qm, km, vm = _inproj(x, mod, wp)
    o_f, o_b = _gla(q, k, v, lgf, lgb)
    o_mla = _mla(qm, km, vm)
    x1, h2, route, counts = _merge(o_f, o_b, zg, zga, zgm, o_mla, x, mod, wp)
    return _moe(x1, h2, route, counts, mod, wp)


def kernel(x_prompt, x_sample, c_prompt, c_sample, w_ada, b_ada, g_norm1, w_in, w_gk_fwd, b_gk_fwd, w_gk_bwd, b_gk_bwd, g_gla_out, w_gla_o, g_q_a, w_q_b, g_kv_a, w_kv_b, g_q_nope, g_k_nope, g_q_rope, g_k_rope, w_mla_o, w_out, g_norm2, w_group, b_group, w_router, b_router, w_e_gate, w_e_up, w_e_down):
    assert w_ada.shape[0] == 1, "single-layer trunk"
    s_max = max(x_prompt.shape[1], x_sample.shape[1])
    wp = _prep_weights(s_max, *[p[0] for p in (
        g_norm1, w_in, w_gk_fwd, b_gk_fwd, w_gk_bwd, b_gk_bwd, g_gla_out, w_gla_o, g_q_a, w_q_b, g_kv_a,
        w_kv_b, g_q_nope, g_k_nope, g_q_rope, g_k_rope, w_mla_o, w_out, g_norm2, w_group, b_group,
        w_router, b_router, w_e_gate, w_e_up, w_e_down)])
    n_prompt = x_prompt.shape[0]
    mod = _ada(jnp.concatenate([c_prompt, c_sample], axis=0), w_ada[0], b_ada[0])
    y_prompt = _layer(x_prompt, mod[:n_prompt], wp)
    y_sample = _layer(x_sample, mod[n_prompt:], wp)
    return (y_prompt, y_sample)
```

```python
import math

import numpy as np
import jax
import jax.numpy as jnp
from jax import lax
from jax.experimental import pallas as pl
from jax.experimental.pallas import tpu as pltpu

F32 = jnp.float32
BF16 = jnp.bfloat16

D_MODEL = 1024
GLA_HEADS, GLA_DK, GLA_DV, GLA_RANK = 4, 128, 256, 16
GLA_GATE_NORM = 16.0
MLA_HEADS, MLA_Q_RANK, MLA_KV_RANK = 8, 256, 128
MLA_NOPE, MLA_ROPE, MLA_V = 64, 32, 64
ROPE_BASE = 10000.0
N_GROUPS, EXPERTS_PER_GROUP, TOP_K, D_EXPERT = 4, 8, 2, 512
N_EXPERTS = N_GROUPS * EXPERTS_PER_GROUP
NORM_EPS = 1e-6

LANES = 128
ROW_CHUNKS = D_MODEL // LANES
SLOT = LANES
VMEM_LIMIT = 56 * 1024 * 1024

INPROJ_TILE = 512
ROW_TILE = 256
GLA_CHUNK = 64
GLA_BLOCK = 512
GLA_EXP_CLAMP = 80.0
MLA_TQ = 512
MLA_VROWS = MLA_V + 16
MLA_UNROLL = 2
MERGE_TILE = 512
MERGE_SPLIT = 2
MOE_TILE = 512
MISC_COLS = MLA_Q_RANK + MLA_KV_RANK + 3 * SLOT


def _cparams(sem):
    return pltpu.CompilerParams(dimension_semantics=sem, vmem_limit_bytes=VMEM_LIMIT)


def _dot(a, b):
    return jnp.dot(a, b, preferred_element_type=F32)


def _dot_nt(a, b):
    return lax.dot_general(a, b, (((1,), (1,)), ((), ())), preferred_element_type=F32)


def _split_bf16(x):
    hi = x.astype(BF16)
    lo = (x - hi.astype(F32)).astype(BF16)
    return hi, lo


def _sigmoid(x):
    return jax.nn.sigmoid(x)


def _silu(x):
    return x * _sigmoid(x)


def _ada_kernel(c_ref, w_ref, b_ref, o_ref):
    a = _silu(c_ref[...])
    ahi, alo = _split_bf16(a)
    w = w_ref[...]
    whi, wlo = _split_bf16(w)
    o_ref[...] = _dot(ahi, whi) + _dot(ahi, wlo) + _dot(alo, whi) + b_ref[...]


def _ada(c, w_ada, b_ada):
    nb = c.shape[0]
    cp = jnp.zeros((8, D_MODEL), F32).at[:nb].set(c)
    tn = 1536
    out = pl.pallas_call(
        _ada_kernel,
        out_shape=jax.ShapeDtypeStruct((8, 6 * D_MODEL), F32),
        grid=(6 * D_MODEL // tn,),
        in_specs=[pl.BlockSpec((8, D_MODEL), lambda j: (0, 0)),
                  pl.BlockSpec((D_MODEL, tn), lambda j: (0, j)),
                  pl.BlockSpec((1, tn), lambda j: (0, j))],
        out_specs=pl.BlockSpec((8, tn), lambda j: (0, j)),
        compiler_params=_cparams(("arbitrary",)),
        name="ada",
    )(cp, w_ada, b_ada.reshape(1, -1))
    mod = out[:nb].reshape(nb, 6, D_MODEL)
    return jnp.concatenate([mod, jnp.zeros((nb, 2, D_MODEL), F32)], axis=1)


_C_Q, _C_K, _C_V, _C_G, _C_GA, _C_GM = 0, 512, 1024, 2048, 3072, 4096
_C_MISC = 5120
IN_EXT = _C_MISC + MISC_COLS


def _segmean(xsq, bd):
    return _dot(xsq.astype(BF16), bd)


def _inproj_kernel(x_ref, mod_ref, g1_ref, win_ref, wgk_ref, bgk_ref, gqa_ref, wq_ref, gkva_ref,
                   wk_ref, wv_ref, vone_ref, bd_ref, lv_ref, cv_ref, cos_ref, sin_ref, cosc_ref, sinc_ref,
                   q_o, k_o, v_o, zg_o, zga_o, zgm_o, lgf_o, lgb_o, qm_o, km_o, vm_o, h_scr):
    x = x_ref[0]
    ms = jnp.mean(x * x, axis=-1, keepdims=True)
    h = (x * lax.rsqrt(ms + NORM_EPS) * g1_ref[...]) * (1.0 + mod_ref[0, 1:2, :]) + mod_ref[0, 0:1, :]
    h_scr[...] = h.astype(BF16)

    def proj(c0, n):
        return _dot(h_scr[...], win_ref[:, c0:c0 + n])

    q_o[0] = (proj(_C_Q, 512) * (GLA_DK ** -0.5)).astype(BF16)
    k_o[0] = proj(_C_K, 512).astype(BF16)
    v_o[0] = proj(_C_V, 1024).astype(BF16)
    zg_o[0] = _silu(proj(_C_G, 1024)).astype(BF16)
    zga_o[0] = _sigmoid(proj(_C_GA, 1024)).astype(BF16)
    zgm_o[0] = _sigmoid(proj(_C_GM, 1024)).astype(BF16)
    misc = proj(_C_MISC, MISC_COLS)

    bd = bd_ref[...]
    cos_t = cos_ref[...]
    sin_t = sin_ref[...]
    g_kn, g_kr, g_krs = lv_ref[2:3, :], lv_ref[3:4, :], lv_ref[4:5, :]
    scale = (MLA_NOPE + MLA_ROPE) ** -0.5 * math.log2(math.e)

    zqa = misc[:, 0:MLA_Q_RANK]
    qa = zqa * lax.rsqrt(jnp.mean(zqa * zqa, axis=-1, keepdims=True) + NORM_EPS) * gqa_ref[...]
    qq = _dot_nt(wq_ref[...], qa.astype(BF16))
    cos_c, sin_c = cosc_ref[...], sinc_ref[...]
    gc_q, gc_qs = cv_ref[:, 0:1], cv_ref[:, 1:2]
    half = MLA_ROPE // 2
    r0, r1, r2 = MLA_NOPE, MLA_NOPE + half, MLA_NOPE + MLA_ROPE
    for hh in range(MLA_HEADS):
        xq = qq[hh * SLOT:(hh + 1) * SLOT, :]
        xs = jnp.concatenate([xq[:r0], xq[r1:r2], xq[r0:r1], xq[r2:]], axis=0)
        r = lax.rsqrt(_dot(bd, (xq * xq).astype(BF16)) + NORM_EPS)
        qm = (xq * gc_q * cos_c + xs * gc_qs * sin_c) * (r * scale)
        qm_o[0, hh * SLOT:(hh + 1) * SLOT, :] = qm.astype(BF16)

    c0 = MLA_Q_RANK
    zkva = misc[:, c0:c0 + MLA_KV_RANK]
    kva = zkva * lax.rsqrt(jnp.mean(zkva * zkva, axis=-1, keepdims=True) + NORM_EPS) * gkva_ref[...]
    kvab = kva.astype(BF16)
    kk = _dot(kvab, wk_ref[...])
    for t in range(vm_o.shape[1]):
        vm_o[0, t] = (_dot_nt(wv_ref[...], kvab[t * ROW_TILE:(t + 1) * ROW_TILE]) + vone_ref[...]).astype(BF16)
    c0 += MLA_KV_RANK
    kr = misc[:, c0:c0 + SLOT]
    krs = misc[:, c0 + SLOT:c0 + 2 * SLOT]
    r_kr = lax.rsqrt(_segmean(kr * kr, bd) + NORM_EPS)
    kpe = (kr * g_kr * cos_t + krs * g_krs * sin_t) * r_kr
    for hh in range(MLA_HEADS):
        xk = kk[:, hh * SLOT:(hh + 1) * SLOT]
        r = lax.rsqrt(_segmean(xk * xk, bd) + NORM_EPS)
        km_o[0, :, hh * SLOT:(hh + 1) * SLOT] = (xk * r * g_kn + kpe).astype(BF16)

    c0 += 2 * SLOT
    zgate = misc[:, c0:c0 + SLOT].astype(BF16)
    pre = _dot(zgate, wgk_ref[...]) + bgk_ref[...]
    lg = (jnp.minimum(pre, 0.0) - jnp.log(1.0 + jnp.exp(-jnp.abs(pre)))) * (1.0 / GLA_GATE_NORM)
    lgf_o[0] = lg[:, :GLA_HEADS * GLA_DK]
    lgb_o[0] = lg[:, GLA_HEADS * GLA_DK:]


def _inproj(x, mod, wp):
    nb, s, d = x.shape
    tm = min(INPROJ_TILE, s)
    kt = min(ROW_TILE, s)
    const = lambda b, i: (0, 0)
    row = lambda b, i: (b, i, 0)

    def full(a):
        return pl.BlockSpec(a.shape, const)

    def out(n, dt):
        return jax.ShapeDtypeStruct((nb, s, n), dt), pl.BlockSpec((1, tm, n), row)

    outs = [out(512, BF16), out(512, BF16), out(1024, BF16), out(1024, BF16), out(1024, BF16),
            out(1024, BF16), out(512, F32), out(512, F32),
            (jax.ShapeDtypeStruct((nb, MLA_HEADS * SLOT, s), BF16),
             pl.BlockSpec((1, MLA_HEADS * SLOT, tm), lambda b, i: (b, 0, i))),
            out(MLA_HEADS * SLOT, BF16),
            (jax.ShapeDtypeStruct((nb, s // kt, MLA_HEADS * MLA_VROWS, kt), BF16),
             pl.BlockSpec((1, tm // kt, MLA_HEADS * MLA_VROWS, kt), lambda b, i: (b, i, 0, 0)))]
    return pl.pallas_call(
        _inproj_kernel,
        out_shape=[o[0] for o in outs],
        grid=(nb, s // tm),
        in_specs=[pl.BlockSpec((1, tm, d), row),
                  pl.BlockSpec((1, 8, d), lambda b, i: (b, 0, 0)),
                  full(wp["g1"]), full(wp["w_in"]), full(wp["w_gk"]), full(wp["b_gk"]),
                  full(wp["g_qa"]), full(wp["w_q"]), full(wp["g_kva"]), full(wp["w_k"]), full(wp["w_v"]),
                  full(wp["v_ones"]), full(wp["bd"]), full(wp["lanevecs"]), full(wp["colvecs"]),
                  pl.BlockSpec((tm, SLOT), lambda b, i: (i, 0)),
                  pl.BlockSpec((tm, SLOT), lambda b, i: (i, 0)),
                  pl.BlockSpec((SLOT, tm), lambda b, i: (0, i)),
                  pl.BlockSpec((SLOT, tm), lambda b, i: (0, i))],
        out_specs=[o[1] for o in outs],
        scratch_shapes=[pltpu.VMEM((tm, d), BF16)],
        compiler_params=_cparams(("arbitrary", "arbitrary")),
        name="inproj",
    )(x, mod, wp["g1"], wp["w_in"], wp["w_gk"], wp["b_gk"], wp["g_qa"], wp["w_q"], wp["g_kva"],
      wp["w_k"], wp["w_v"], wp["v_ones"], wp["bd"], wp["lanevecs"], wp["colvecs"],
      wp["cos"][:s], wp["sin"][:s], wp["cos"][:s].T, wp["sin"][:s].T)


def _dot_tn(a, b):
    return lax.dot_general(a, b, (((0,), (0,)), ((), ())), preferred_element_type=F32)


def _gla_direction(q_ref, k_ref, v_ref, g_ref, o_ref, st_ref, scr, cum, mask, reverse):
    qin_s, qmid_s, kmid_s, kout_s, tot_s, u_s = scr
    c = GLA_CHUNK
    nch = q_ref.shape[1] // c
    rows = [slice(j * c, (j + 1) * c) for j in range(nch)]
    ks = [slice(h * GLA_DK, (h + 1) * GLA_DK) for h in range(GLA_HEADS)]
    vs = [slice(h * GLA_DV, (h + 1) * GLA_DV) for h in range(GLA_HEADS)]

    for j in range(nch):
        hi, lo = _split_bf16(g_ref[0, rows[j], :])
        b = _dot(cum, hi) + _dot(cum, lo)
        mid = b[c // 2:c // 2 + 1, :]
        tot = b[0:1, :] if reverse else b[c - 1:c, :]
        q = q_ref[0, rows[j], :].astype(F32)
        k = k_ref[0, rows[j], :].astype(F32)
        qin_s[rows[j], :] = (q * jnp.exp(b)).astype(BF16)
        qmid_s[rows[j], :] = (q * jnp.exp(jnp.minimum(b - mid, GLA_EXP_CLAMP))).astype(BF16)
        kmid_s[rows[j], :] = (k * jnp.exp(jnp.minimum(mid - b, GLA_EXP_CLAMP))).astype(BF16)
        kout_s[rows[j], :] = (k * jnp.exp(tot - b)).astype(BF16)
        tot_s[j:j + 1, :] = tot

    attn = {}
    for j in range(nch):
        for h in range(GLA_HEADS):
            a = _dot_nt(qmid_s[rows[j], ks[h]], kmid_s[rows[j], ks[h]])
            attn[j, h] = jnp.where(mask, a, 0.0).astype(BF16)

    for j in range(nch):
        for h in range(GLA_HEADS):
            v = v_ref[0, rows[j], vs[h]]
            o_ref[0, rows[j], vs[h]] = _dot(attn[j, h], v)
            u_s[j * GLA_HEADS + h] = _dot_tn(kout_s[rows[j], ks[h]], v)

    for j in (range(nch - 1, -1, -1) if reverse else range(nch)):
        for h in range(GLA_HEADS):
            st = st_ref[h]
            o_ref[0, rows[j], vs[h]] += _dot(qin_s[rows[j], ks[h]], st.astype(BF16))
            tot = jnp.broadcast_to(tot_s[j:j + 1, ks[h]], (GLA_DK, GLA_DK))
            dec = jnp.exp(tot.T)
            st_ref[h] = jnp.concatenate([dec] * (GLA_DV // GLA_DK), axis=1) * st + u_s[j * GLA_HEADS + h]


def _gla_kernel(qf, kf, vf, gf, qb, kb, vb, gb, of_ref, ob_ref, sf_ref, sb_ref, *scr):
    @pl.when(pl.program_id(1) == 0)
    def _():
        sf_ref[...] = jnp.zeros_like(sf_ref)
        sb_ref[...] = jnp.zeros_like(sb_ref)

    c = GLA_CHUNK
    ri = lax.broadcasted_iota(jnp.int32, (c, c), 0)
    ci = lax.broadcasted_iota(jnp.int32, (c, c), 1)
    cum_f = jnp.where(ri >= ci, 1.0, 0.0).astype(BF16)
    cum_b = jnp.where(ci >= ri, 1.0, 0.0).astype(BF16)
    n = len(scr) // 2
    _gla_direction(qf, kf, vf, gf, of_ref, sf_ref, scr[:n], cum_f, ri >= ci, False)
    _gla_direction(qb, kb, vb, gb, ob_ref, sb_ref, scr[n:], cum_b, ci > ri, True)


def _gla(q, k, v, lgf, lgb):
    nb, s, _ = q.shape
    cb = min(GLA_BLOCK, s)
    ns = s // cb
    nch = cb // GLA_CHUNK
    fwd = lambda b, i: (b, i, 0)
    bwd = lambda b, i: (b, ns - 1 - i, 0)
    hk, hv = GLA_HEADS * GLA_DK, GLA_HEADS * GLA_DV

    def specs(im):
        return [pl.BlockSpec((1, cb, hk), im), pl.BlockSpec((1, cb, hk), im),
                pl.BlockSpec((1, cb, hv), im), pl.BlockSpec((1, cb, hk), im)]

    per_dir = [pltpu.VMEM((cb, hk), BF16)] * 4 + [pltpu.VMEM((8, hk), F32),
                                                  pltpu.VMEM((nch * GLA_HEADS, GLA_DK, GLA_DV), F32)]
    return pl.pallas_call(
        _gla_kernel,
        out_shape=[jax.ShapeDtypeStruct((nb, s, hv), F32)] * 2,
        grid=(nb, ns),
        in_specs=specs(fwd) + specs(bwd),
        out_specs=[pl.BlockSpec((1, cb, hv), fwd), pl.BlockSpec((1, cb, hv), bwd)],
        scratch_shapes=[pltpu.VMEM((GLA_HEADS, GLA_DK, GLA_DV), F32)] * 2 + per_dir * 2,
        compiler_params=_cparams(("arbitrary", "arbitrary")),
        name="gla",
    )(q, k, v, lgf, q, k, v, lgb)


def _mla_kernel(q_ref, k_ref, vt_ref, o_ref, sc0, sc1, p0, p1):
    tq = q_ref.shape[2]
    nk = vt_ref.shape[1]
    tk = vt_ref.shape[3]
    gu = MLA_UNROLL
    ng = nk // gu
    sc, pp = (sc0, sc1), (p0, p1)
    qs = [q_ref[0, hh * SLOT:(hh + 1) * SLOT, :] for hh in range(2)]
    tiles = [(u, hh) for u in range(gu) for hh in range(2)]

    def score_tile(g, slot, u, hh):
        r0 = pl.multiple_of((g * gu + u) * tk, tk)
        kt = k_ref[0, pl.ds(r0, tk), hh * SLOT:(hh + 1) * SLOT]
        st = _dot(kt, qs[hh])
        sc[slot][2 * u + hh] = st
        return jnp.max(st, axis=0, keepdims=True)

    def softmax_tile(slot, u, hh, m, tmax):
        m_new = jnp.maximum(m, tmax)
        pp[slot][2 * u + hh] = jnp.exp2(sc[slot][2 * u + hh] - m_new).astype(BF16)
        return m_new, jnp.exp2(m - m_new)

    def value_tile(g, slot, u, hh, alpha, acc):
        vt = vt_ref[0, g * gu + u, hh * MLA_VROWS:(hh + 1) * MLA_VROWS, :]
        return alpha * acc + _dot(vt, pp[slot][2 * u + hh])

    def issue_scores(g, slot):
        return tuple(score_tile(g, slot, u, hh) for u, hh in tiles)

    def softmax(slot, ms, tmax):
        ms, alphas = list(ms), []
        for u, hh in tiles:
            ms[hh], alpha = softmax_tile(slot, u, hh, ms[hh], tmax[2 * u + hh])
            alphas.append(alpha)
        return tuple(ms), tuple(alphas)

    def values(g, slot, alphas, accs):
        accs = list(accs)
        for u, hh in tiles:
            accs[hh] = value_tile(g, slot, u, hh, alphas[2 * u + hh], accs[hh])
        return tuple(accs)

    def step(g, slot, carry, more_scores):
        ms, alphas, accs, tmax = carry
        ms, accs, new_alphas, new_tmax = list(ms), list(accs), [], []
        for u, hh in tiles:
            idx = 2 * u + hh
            accs[hh] = value_tile(g - 1, 1 - slot, u, hh, alphas[idx], accs[hh])
            ms[hh], alpha = softmax_tile(slot, u, hh, ms[hh], tmax[idx])
            new_alphas.append(alpha)
            if more_scores:
                new_tmax.append(score_tile(g + 1, 1 - slot, u, hh))
        return tuple(ms), tuple(new_alphas), tuple(accs), tuple(new_tmax)

    def double_step(i, carry):
        g = 2 * i + 1
        carry = step(g, 1, carry, True)
        return step(g + 1, 0, carry, True)

    m0 = jnp.full((1, tq), -jnp.inf, F32)
    acc0 = jnp.zeros((MLA_VROWS, tq), F32)
    tmax = issue_scores(0, 0)
    ms, alphas = softmax(0, (m0, m0), tmax)
    tmax = issue_scores(1, 1)
    carry = (ms, alphas, (acc0, acc0), tmax)
    for i in range((ng - 2) // 2):
        carry = double_step(i, carry)
    ms, alphas, accs, _ = step(ng - 1, 1, carry, False)
    accs = values(ng - 1, 1, alphas, accs)
    outs = [(a[:MLA_V] / a[MLA_V:MLA_V + 1]).T for a in accs]
    o_ref[0] = jnp.concatenate(outs, axis=-1).astype(BF16)


def _mla(qm, km, vmt):
    nb, s, _ = km.shape
    nk, tk = vmt.shape[1], vmt.shape[3]
    tq = min(MLA_TQ, s)
    assert (nk // MLA_UNROLL) % 2 == 0 and nk % MLA_UNROLL == 0
    slot_tiles = 2 * MLA_UNROLL
    return pl.pallas_call(
        _mla_kernel,
        out_shape=jax.ShapeDtypeStruct((nb, s, MLA_HEADS * MLA_V), BF16),
        grid=(nb, MLA_HEADS // 2, s // tq),
        in_specs=[pl.BlockSpec((1, 2 * SLOT, tq), lambda b, p, i: (b, p, i)),
                  pl.BlockSpec((1, s, 2 * SLOT), lambda b, p, i: (b, 0, p)),
                  pl.BlockSpec((1, nk, 2 * MLA_VROWS, tk), lambda b, p, i: (b, 0, p, 0))],
        out_specs=pl.BlockSpec((1, tq, 2 * MLA_V), lambda b, p, i: (b, i, p)),
        scratch_shapes=[pltpu.VMEM((slot_tiles, tk, tq), F32)] * 2 + [pltpu.VMEM((slot_tiles, tk, tq), BF16)] * 2,
        compiler_params=_cparams(("arbitrary", "arbitrary", "arbitrary")),
        name="mla",
    )(qm, km, vmt)


def _merge_kernel(of_ref, ob_ref, zg_ref, zga_ref, zgm_ref, om_ref, x_ref, mod_ref, ggla_ref, wglo_ref,
                  wmo_ref, wout_ref, g2_ref, wr_ref, br_ref, ltri_ref,
                  x1_o, h2_o, route_o, cnt_o, cnt_scr):
    first = (pl.program_id(0) == 0) & (pl.program_id(1) == 0)

    @pl.when(first)
    def _():
        cnt_scr[...] = jnp.zeros_like(cnt_scr)

    tm = x_ref.shape[1]
    hm = tm // MERGE_SPLIT
    subs = [slice(i * hm, (i + 1) * hm) for i in range(MERGE_SPLIT)]

    def gla_gate(rs):
        o = of_ref[0, rs, :] + ob_ref[0, rs, :]
        zg = zg_ref[0, rs, :].astype(F32)
        parts = []
        for hh in range(GLA_HEADS):
            seg = o[:, hh * GLA_DV:(hh + 1) * GLA_DV]
            r = lax.rsqrt(jnp.mean(seg * seg, axis=-1, keepdims=True) + NORM_EPS)
            parts.append((seg * r * ggla_ref[...]) * zg[:, hh * GLA_DV:(hh + 1) * GLA_DV])
        return jnp.concatenate(parts, axis=-1).astype(BF16)

    ogs = [gla_gate(rs) for rs in subs]
    ys = [(_dot(og, wglo_ref[...]), _dot(om_ref[0, rs, :], wmo_ref[...])) for og, rs in zip(ogs, subs)]
    merged = [(zga_ref[0, rs, :].astype(F32) * y_gla
               + zgm_ref[0, rs, :].astype(F32) * y_mla).astype(BF16) for (y_gla, y_mla), rs in zip(ys, subs)]
    mixes = [_dot(m, wout_ref[...]) for m in merged]

    w = wr_ref[...]
    whi, wlo = _split_bf16(w)
    splits = []
    for mix, rs in zip(mixes, subs):
        x1 = x_ref[0, rs, :] + mod_ref[0, 2:3, :] * mix
        x1_o[0, rs, :] = x1
        ms = jnp.mean(x1 * x1, axis=-1, keepdims=True)
        h2 = (x1 * lax.rsqrt(ms + NORM_EPS) * g2_ref[...]) * (1.0 + mod_ref[0, 4:5, :]) + mod_ref[0, 3:4, :]
        for c in range(ROW_CHUNKS):
            h2_o[pl.ds(rs.start * ROW_CHUNKS + c, hm, stride=ROW_CHUNKS), :] = h2[:, c * LANES:(c + 1) * LANES]
        splits.append(_split_bf16(h2))

    logit_parts = [_dot(hhi, whi) + _dot(hhi, wlo) + _dot(hlo, whi) + br_ref[...] for hhi, hlo in splits]
    logits = jnp.concatenate(logit_parts, axis=0)
    lane = lax.broadcasted_iota(jnp.int32, (tm, LANES), 1)
    neg = -jnp.inf
    is_g = (lane >= N_EXPERTS) & (lane < N_EXPERTS + N_GROUPS)
    lgm = jnp.where(is_g, logits, neg)
    mg = jnp.max(lgm, axis=-1, keepdims=True)
    sg = jnp.sum(jnp.exp(lgm - mg), axis=-1, keepdims=True)
    p_top = 1.0 / sg
    g_idx = jnp.min(jnp.where(lgm == mg, lane, 2 * LANES), axis=-1, keepdims=True) - N_EXPERTS
    in_grp = (lane >= g_idx * EXPERTS_PER_GROUP) & (lane < (g_idx + 1) * EXPERTS_PER_GROUP)
    el = jnp.where(in_grp, logits, neg)
    m1 = jnp.max(el, axis=-1, keepdims=True)
    e1 = jnp.min(jnp.where(el == m1, lane, 2 * LANES), axis=-1, keepdims=True)
    el2 = jnp.where(lane == e1, neg, el)
    m2 = jnp.max(el2, axis=-1, keepdims=True)
    e2 = jnp.min(jnp.where(el2 == m2, lane, 2 * LANES), axis=-1, keepdims=True)
    se = jnp.sum(jnp.exp(el - m1), axis=-1, keepdims=True)
    pv1 = 1.0 / se
    pv2 = jnp.exp(m2 - m1) / se
    w1 = p_top * pv1 / (pv1 + pv2)
    w2 = p_top * pv2 / (pv1 + pv2)

    oh1 = lane == e1
    oh2 = lane == e2
    oh = jnp.where(oh1 | oh2, 1.0, 0.0)
    prefix = _dot(ltri_ref[...], oh.astype(BF16)) + cnt_scr[...]
    r1 = jnp.sum(jnp.where(oh1, prefix, 0.0), axis=-1, keepdims=True)
    r2 = jnp.sum(jnp.where(oh2, prefix, 0.0), axis=-1, keepdims=True)
    cnt_scr[...] = cnt_scr[...] + jnp.sum(oh, axis=0, keepdims=True)
    cnt_o[...] = jnp.broadcast_to(cnt_scr[...], cnt_o.shape)

    route = jnp.where(lane == 0, e1.astype(F32), 0.0)
    route = jnp.where(lane == 1, e2.astype(F32), route)
    route = jnp.where(lane == 2, r1, route)
    route = jnp.where(lane == 3, r2, route)
    route = jnp.where(lane == 4, w1, route)
    route = jnp.where(lane == 5, w2, route)
    route_o[0] = route


def _merge(o_f, o_b, zg, zga, zgm, o_mla, x, mod, wp):
    nb, s, d = x.shape
    tm = min(MERGE_TILE, s)
    const = lambda b, i: (0, 0)
    row = lambda b, i: (b, i, 0)

    def full(a):
        return pl.BlockSpec(a.shape, const)

    def rows(n):
        return pl.BlockSpec((1, tm, n), row)

    ltri = jnp.asarray(np.tril(np.ones((tm, tm), np.float32), -1), BF16)
    return pl.pallas_call(
        _merge_kernel,
        out_shape=[jax.ShapeDtypeStruct((nb, s, d), F32), jax.ShapeDtypeStruct((nb * s * ROW_CHUNKS, LANES), F32),
                   jax.ShapeDtypeStruct((nb, s, LANES), F32), jax.ShapeDtypeStruct((8, LANES), F32)],
        grid=(nb, s // tm),
        in_specs=[rows(1024), rows(1024), rows(1024), rows(1024), rows(1024), rows(MLA_HEADS * MLA_V),
                  rows(d), pl.BlockSpec((1, 8, d), lambda b, i: (b, 0, 0)),
                  full(wp["g_gla"]), full(wp["w_gla_o"]), full(wp["w_mla_o"]), full(wp["w_out"]),
                  full(wp["g2"]), full(wp["w_route"]), full(wp["b_route"]), full(ltri)],
        out_specs=[rows(d), pl.BlockSpec((tm * ROW_CHUNKS, LANES), lambda b, i: (b * (s // tm) + i, 0)),
                   rows(LANES), pl.BlockSpec((8, LANES), const)],
        scratch_shapes=[pltpu.VMEM((1, LANES), F32)],
        compiler_params=_cparams(("arbitrary", "arbitrary")),
        name="merge",
    )(o_f, o_b, zg, zga, zgm, o_mla, x, mod, wp["g_gla"], wp["w_gla_o"], wp["w_mla_o"], wp["w_out"],
      wp["g2"], wp["w_route"], wp["b_route"], ltri)


def _dispatch_kernel(seg_ref, dest_ref, h_ref, xs_out, zero_scr, sem):
    tt = h_ref.shape[0] // ROW_CHUNKS
    tile_rows = MOE_TILE * ROW_CHUNKS

    @pl.when(pl.program_id(0) == 0)
    def _():
        zero_scr[...] = jnp.zeros_like(zero_scr)

        def zero_copy(e):
            r0 = pl.multiple_of((seg_ref[0, e] - MOE_TILE) * ROW_CHUNKS, tile_rows)
            return pltpu.make_async_copy(zero_scr, xs_out.at[pl.ds(r0, tile_rows)], sem.at[1])

        for e in range(N_EXPERTS):
            @pl.when(seg_ref[1, e] > 0)
            def _():
                zero_copy(e).start()
        for e in range(N_EXPERTS):
            @pl.when(seg_ref[1, e] > 0)
            def _():
                zero_copy(e).wait()

        def tail_copy(ti):
            return pltpu.make_async_copy(zero_scr, xs_out.at[pl.ds(pl.multiple_of(ti * tile_rows, tile_rows), tile_rows)],
                                         sem.at[1])

        first_tail = seg_ref[0, N_EXPERTS - 1] // MOE_TILE
        n_tiles = xs_out.shape[0] // tile_rows
        lax.fori_loop(first_tail, n_tiles, lambda ti, c: (tail_copy(ti).start(), c)[1], 0)
        lax.fori_loop(first_tail, n_tiles, lambda ti, c: (tail_copy(ti).wait(), c)[1], 0)

    def copy(r, slot):
        d0 = pl.multiple_of(dest_ref[0, 0, 2 * r + slot] * ROW_CHUNKS, ROW_CHUNKS)
        return pltpu.make_async_copy(h_ref.at[pl.ds(pl.multiple_of(r * ROW_CHUNKS, ROW_CHUNKS), ROW_CHUNKS)],
                                     xs_out.at[pl.ds(d0, ROW_CHUNKS)], sem.at[0])

    def start(r, c):
        copy(r, 0).start(priority=0)
        copy(r, 1).start(priority=1)
        return c

    def wait(r, c):
        copy(r, 0).wait()
        copy(r, 1).wait()
        return c

    lax.fori_loop(0, tt, start, 0, unroll=8)
    lax.fori_loop(0, tt, wait, 0, unroll=8)


def _dispatch(h2, dest, segments, n_rows):
    t = h2.shape[0] // ROW_CHUNKS
    tt = min(ROW_TILE, t)
    nt = t // tt
    return pl.pallas_call(
        _dispatch_kernel,
        out_shape=jax.ShapeDtypeStruct((n_rows * ROW_CHUNKS, LANES), F32),
        grid_spec=pltpu.PrefetchScalarGridSpec(
            num_scalar_prefetch=1,
            grid=(nt,),
            in_specs=[pl.BlockSpec((1, 1, 2 * tt), lambda i, seg: (i, 0, 0), memory_space=pltpu.SMEM),
                      pl.BlockSpec((tt * ROW_CHUNKS, LANES), lambda i, seg: (i, 0))],
            out_specs=pl.BlockSpec(memory_space=pl.ANY),
            scratch_shapes=[pltpu.VMEM((MOE_TILE * ROW_CHUNKS, LANES), F32), pltpu.SemaphoreType.DMA((2,))]),
        compiler_params=_cparams(("arbitrary",)),
        name="dispatch",
    )(segments, dest.reshape(nt, 1, 2 * tt), h2)


def _expert_kernel(te_ref, na_ref, xs_ref, wg_ref, wu_ref, wd_ref, ys_ref, xb_scr):
    active = pl.program_id(0) < na_ref[0]

    @pl.when(active)
    def _():
        for c in range(ROW_CHUNKS):
            xb_scr[:, c * LANES:(c + 1) * LANES] = xs_ref[pl.ds(c, MOE_TILE, stride=ROW_CHUNKS), :].astype(BF16)
        xb = xb_scr[...]
        a = _silu(_dot(xb, wg_ref[0].astype(BF16))) * _dot(xb, wu_ref[0].astype(BF16))
        y = _dot(a.astype(BF16), wd_ref[0].astype(BF16))
        for c in range(ROW_CHUNKS):
            ys_ref[pl.ds(c, MOE_TILE, stride=ROW_CHUNKS), :] = y[:, c * LANES:(c + 1) * LANES]

    @pl.when(jnp.logical_not(active))
    def _():
        ys_ref[...] = jnp.zeros_like(ys_ref)


def _experts(xs, tile_expert, n_active, wp):
    d = D_MODEL
    n_rows = xs.shape[0] // ROW_CHUNKS
    nt = n_rows // MOE_TILE
    rows = MOE_TILE * ROW_CHUNKS

    def tile(i, te, na):
        return (jnp.minimum(i, na[0] - 1), 0)

    def wmap(i, te, na):
        return (te[jnp.minimum(i, na[0] - 1)], 0, 0)

    return pl.pallas_call(
        _expert_kernel,
        out_shape=jax.ShapeDtypeStruct((n_rows * ROW_CHUNKS, LANES), F32),
        grid_spec=pltpu.PrefetchScalarGridSpec(
            num_scalar_prefetch=2,
            grid=(nt,),
            in_specs=[pl.BlockSpec((rows, LANES), tile),
                      pl.BlockSpec((1, d, D_EXPERT), wmap),
                      pl.BlockSpec((1, d, D_EXPERT), wmap),
                      pl.BlockSpec((1, D_EXPERT, d), wmap)],
            out_specs=pl.BlockSpec((rows, LANES), lambda i, te, na: (i, 0)),
            scratch_shapes=[pltpu.VMEM((MOE_TILE, d), BF16)]),
        compiler_params=_cparams(("arbitrary",)),
        name="experts",
    )(tile_expert, n_active, xs, wp["w_e_gate"], wp["w_e_up"], wp["w_e_down"])


def _combine_kernel(dest_ref, x1_ref, rw_ref, ga_ref, ys_hbm, o_ref, buf, sem):
    tt = x1_ref.shape[0]
    i = pl.program_id(0)
    nt = pl.num_programs(0) - 1
    par = lax.rem(i, 2)

    def copy(p, r, slot, d0):
        return pltpu.make_async_copy(ys_hbm.at[pl.ds(d0, ROW_CHUNKS)],
                                     buf.at[p, slot, pl.ds(pl.multiple_of(r * ROW_CHUNKS, ROW_CHUNKS), ROW_CHUNKS)],
                                     sem.at[p])

    @pl.when(i < nt)
    def _():
        def start(r, c):
            for slot in range(TOP_K):
                d0 = pl.multiple_of(dest_ref[0, 0, 2 * r + slot] * ROW_CHUNKS, ROW_CHUNKS)
                copy(par, r, slot, d0).start(priority=slot)
            return c

        lax.fori_loop(0, tt, start, 0, unroll=8)

    @pl.when(i > 0)
    def _():
        def wait(r, c):
            for slot in range(TOP_K):
                copy(1 - par, r, slot, 0).wait()
            return c

        lax.fori_loop(0, tt, wait, 0, unroll=8)
        rw = rw_ref[...]
        for c in range(ROW_CHUNKS):
            cs = slice(c * LANES, (c + 1) * LANES)
            rows = pl.ds(c, tt, stride=ROW_CHUNKS)
            moe = buf[1 - par, 0, rows, :] * rw[:, 4:5] + buf[1 - par, 1, rows, :] * rw[:, 5:6]
            o_ref[:, cs] = x1_ref[:, cs] + ga_ref[0, 5:6, cs] * moe


def _combine(x1, route, mod, ys, dest, s):
    t, d = x1.shape
    tt = min(ROW_TILE, s)
    nt = t // tt
    per_b = s // tt
    lag = lambda i: jnp.maximum(i - 1, 0)
    return pl.pallas_call(
        _combine_kernel,
        out_shape=jax.ShapeDtypeStruct((t, d), F32),
        grid=(nt + 1,),
        in_specs=[pl.BlockSpec((1, 1, 2 * tt), lambda i: (jnp.minimum(i, nt - 1), 0, 0), memory_space=pltpu.SMEM),
                  pl.BlockSpec((tt, d), lambda i: (lag(i), 0)),
                  pl.BlockSpec((tt, LANES), lambda i: (lag(i), 0)),
                  pl.BlockSpec((1, 8, d), lambda i: (lag(i) // per_b, 0, 0)),
                  pl.BlockSpec(memory_space=pl.ANY)],
        out_specs=pl.BlockSpec((tt, d), lambda i: (lag(i), 0)),
        scratch_shapes=[pltpu.VMEM((2, TOP_K, tt * ROW_CHUNKS, LANES), F32), pltpu.SemaphoreType.DMA((2,))],
        compiler_params=_cparams(("arbitrary",)),
        name="combine",
    )(dest.reshape(nt, 1, 2 * tt), x1, route, mod, ys)


def _moe(x1, h2, route, counts, mod, wp):
    nb, s, d = x1.shape
    t = nb * s
    route2 = route.reshape(t, LANES)
    e_flat = route2[:, 0:2].astype(jnp.int32)
    rank = route2[:, 2:4].astype(jnp.int32)
    cnt = counts[0, :N_EXPERTS].astype(jnp.int32)
    padded = (cnt + MOE_TILE - 1) // MOE_TILE * MOE_TILE
    pad_end = jnp.cumsum(padded)
    pad_start = pad_end - padded
    is_e = e_flat[..., None] == jnp.arange(N_EXPERTS, dtype=jnp.int32)
    dest = (jnp.sum(jnp.where(is_e, pad_start, 0), axis=-1) + rank).reshape(t * TOP_K)
    n_rows = t * TOP_K + N_EXPERTS * MOE_TILE
    nt = n_rows // MOE_TILE
    tile_row = jnp.arange(nt, dtype=jnp.int32) * MOE_TILE
    tile_expert = jnp.minimum(jnp.sum((pad_end[None, :] <= tile_row[:, None]).astype(jnp.int32), axis=1),
                              N_EXPERTS - 1)
    n_active = (pad_end[-1:] // MOE_TILE).astype(jnp.int32)
    xs = _dispatch(h2, dest, jnp.stack([pad_end, padded]).astype(jnp.int32), n_rows)
    ys = _experts(xs, tile_expert, n_active, wp)
    out = _combine(x1.reshape(t, d), route2, mod, ys, dest, s)
    return out.reshape(nb, s, d)


def _swap_halves(a, axis=-1):
    n = a.shape[axis] // 2
    lo, hi = jnp.split(a, [n], axis=axis)
    return jnp.concatenate([hi, lo], axis=axis)


def _slot(parts, n_lead):
    out = jnp.zeros((n_lead, SLOT), F32)
    for off, a in parts:
        out = out.at[:, off:off + a.shape[-1]].set(a)
    return out


def _prep_weights(s_max, g_norm1, w_in, w_gk_fwd, b_gk_fwd, w_gk_bwd, b_gk_bwd, g_gla_out, w_gla_o, g_q_a,
                  w_q_b, g_kv_a, w_kv_b, g_q_nope, g_k_nope, g_q_rope, g_k_rope, w_mla_o, w_out, g_norm2,
                  w_group, b_group, w_router, b_router, w_e_gate, w_e_up, w_e_down):
    d = D_MODEL
    hk, hv = GLA_HEADS * GLA_DK, GLA_HEADS * GLA_DV
    offs = np.cumsum([0, hk, hk, hv, hv, GLA_RANK, GLA_RANK, MLA_Q_RANK, MLA_KV_RANK, MLA_ROPE, d, d])
    cols = [w_in[:, offs[i]:offs[i + 1]] for i in range(11)]
    wq_, wk_, wv_, wg_, wgf, wgb, wqa, wkva, wkr, wga, wgm = cols
    kr_slot = _slot([(MLA_NOPE, wkr)], d)
    krs_slot = _slot([(MLA_NOPE, _swap_halves(wkr))], d)
    gate_slot = _slot([(0, wgf), (GLA_RANK, wgb)], d)
    w_in_ext = jnp.concatenate([wq_, wk_, wv_, wg_, wga, wgm, wqa, wkva, kr_slot, krs_slot, gate_slot],
                               axis=1).astype(BF16)
    w_gk = jnp.zeros((SLOT, 2 * hk), F32)
    w_gk = w_gk.at[0:GLA_RANK, 0:hk].set(w_gk_fwd).at[GLA_RANK:2 * GLA_RANK, hk:].set(w_gk_bwd).astype(BF16)
    b_gk = jnp.concatenate([b_gk_fwd, b_gk_bwd]).reshape(1, -1)

    wq3 = w_q_b.reshape(MLA_Q_RANK, MLA_HEADS, MLA_NOPE + MLA_ROPE)
    zq = jnp.zeros((MLA_Q_RANK, MLA_HEADS, SLOT - MLA_NOPE - MLA_ROPE), F32)
    w_q = jnp.concatenate([wq3, zq], axis=-1).reshape(MLA_Q_RANK, -1).T.astype(BF16)
    wkv3 = w_kv_b.reshape(MLA_KV_RANK, MLA_HEADS, MLA_NOPE + MLA_V)
    w_k = jnp.concatenate([wkv3[..., :MLA_NOPE], jnp.zeros((MLA_KV_RANK, MLA_HEADS, SLOT - MLA_NOPE), F32)],
                          axis=-1).reshape(MLA_KV_RANK, -1).astype(BF16)
    w_v = jnp.concatenate([wkv3[..., MLA_NOPE:], jnp.zeros((MLA_KV_RANK, MLA_HEADS, MLA_VROWS - MLA_V), F32)],
                          axis=-1).reshape(MLA_KV_RANK, -1).T.astype(BF16)
    v_ones = jnp.zeros((MLA_HEADS, MLA_VROWS), F32).at[:, MLA_V].set(1.0).reshape(-1, 1)

    lanevecs = jnp.concatenate([
        _slot([(0, g_q_nope[None]), (MLA_NOPE, g_q_rope[None])], 1),
        _slot([(MLA_NOPE, _swap_halves(g_q_rope)[None])], 1),
        _slot([(0, g_k_nope[None])], 1),
        _slot([(MLA_NOPE, g_k_rope[None])], 1),
        _slot([(MLA_NOPE, _swap_halves(g_k_rope)[None])], 1),
        jnp.zeros((3, SLOT), F32)], axis=0)
    seg = np.zeros((SLOT, SLOT), np.float32)
    seg[:MLA_NOPE, :MLA_NOPE] = 1.0 / MLA_NOPE
    seg[MLA_NOPE:MLA_NOPE + MLA_ROPE, MLA_NOPE:MLA_NOPE + MLA_ROPE] = 1.0 / MLA_ROPE
    bd = jnp.asarray(seg, BF16)

    half = MLA_ROPE // 2
    inv = ROPE_BASE ** (-jnp.arange(half, dtype=F32) / half)
    ang = jnp.arange(s_max, dtype=F32)[:, None] * inv[None, :]
    cos, sin = jnp.cos(ang), jnp.sin(ang)
    pad = jnp.zeros((s_max, SLOT - MLA_NOPE - MLA_ROPE), F32)
    cos_t = jnp.concatenate([jnp.ones((s_max, MLA_NOPE), F32), cos, cos, pad], axis=1)
    sin_t = jnp.concatenate([jnp.zeros((s_max, MLA_NOPE), F32), -sin, sin, pad], axis=1)

    w_route = jnp.zeros((d, LANES), F32).at[:, :N_EXPERTS].set(w_router)
    w_route = w_route.at[:, N_EXPERTS:N_EXPERTS + N_GROUPS].set(w_group)
    b_route = jnp.zeros((1, LANES), F32).at[0, :N_EXPERTS].set(b_router)
    b_route = b_route.at[0, N_EXPERTS:N_EXPERTS + N_GROUPS].set(b_group)

    return dict(
        g1=g_norm1.reshape(1, d), w_in=w_in_ext, w_gk=w_gk, b_gk=b_gk,
        g_qa=g_q_a.reshape(1, -1), w_q=w_q, g_kva=g_kv_a.reshape(1, -1), w_k=w_k, w_v=w_v,
        v_ones=v_ones, bd=bd, lanevecs=lanevecs, colvecs=lanevecs.T, cos=cos_t, sin=sin_t,
        g_gla=g_gla_out.reshape(1, -1), w_gla_o=w_gla_o.astype(BF16), w_mla_o=w_mla_o.astype(BF16),
        w_out=w_out.astype(BF16), g2=g_norm2.reshape(1, d), w_route=w_route, b_route=b_route,
        w_e_gate=w_e_gate, w_e_up=w_e_up, w_e_down=w_e_down)


def _layer(x, mod, wp):
    q, k, v, zg, zga, zgm, lgf, lgb, qm, km, vm = _inproj(x, mod, wp)
    o_f, o_b = _gla(q, k, v, lgf, lgb)
    o_mla = _mla(qm, km, vm)
    x1, h2, route, counts = _merge(o_f, o_b, zg, zga, zgm, o_mla, x, mod, wp)
    return _moe(x1, h2, route, counts, mod, wp)


def kernel(x_prompt, x_sample, c_prompt, c_sample, w_ada, b_ada, g_norm1, w_in, w_gk_fwd, b_gk_fwd, w_gk_bwd, b_gk_bwd, g_gla_out, w_gla_o, g_q_a, w_q_b, g_kv_a, w_kv_b, g_q_nope, g_k_nope, g_q_rope, g_k_rope, w_mla_o, w_out, g_norm2, w_group, b_group, w_router, b_router, w_e_gate, w_e_up, w_e_down):
    assert w_ada.shape[0] == 1, "single-layer trunk"
    s_max = max(x_prompt.shape[1], x_sample.shape[1])
    wp = _prep_weights(s_max, *[p[0] for p in (
        g_norm1, w_in, w_gk_fwd, b_gk_fwd, w_gk_bwd, b_gk_bwd, g_gla_out, w_gla_o, g_q_a, w_q_b, g_kv_a,
        w_kv_b, g_q_nope, g_k_nope, g_q_rope, g_k_rope, w_mla_o, w_out, g_norm2, w_group, b_group,
        w_router, b_router, w_e_gate, w_e_up, w_e_down)])
    n_prompt = x_prompt.shape[0]
    mod = _ada(jnp.concatenate([c_prompt, c_sample], axis=0), w_ada[0], b_ada[0])
    y_prompt = _layer(x_prompt, mod[:n_prompt], wp)
    y_sample = _layer(x_sample, mod[n_prompt:], wp)
    return (y_prompt, y_sample)
```

```python
import math

import numpy as np
import jax
import jax.numpy as jnp
from jax import lax
from jax.experimental import pallas as pl
from jax.experimental.pallas import tpu as pltpu

F32 = jnp.float32
BF16 = jnp.bfloat16

D_MODEL = 1024
GLA_HEADS, GLA_DK, GLA_DV, GLA_RANK = 4, 128, 256, 16
GLA_GATE_NORM = 16.0
MLA_HEADS, MLA_Q_RANK, MLA_KV_RANK = 8, 256, 128
MLA_NOPE, MLA_ROPE, MLA_V = 64, 32, 64
ROPE_BASE = 10000.0
N_GROUPS, EXPERTS_PER_GROUP, TOP_K, D_EXPERT = 4, 8, 2, 512
N_EXPERTS = N_GROUPS * EXPERTS_PER_GROUP
NORM_EPS = 1e-6

LANES = 128
ROW_CHUNKS = D_MODEL // LANES
SLOT = LANES
VMEM_LIMIT = 56 * 1024 * 1024

INPROJ_TILE = 512
ROW_TILE = 256
GLA_CHUNK = 64
GLA_BLOCK = 512
GLA_EXP_CLAMP = 80.0
MLA_TQ = 512
MLA_VROWS = MLA_V + 16
MLA_UNROLL = 2
MOVE_TILE = 512
MERGE_TILE = 512
MERGE_SPLIT = 2
MOE_TILE = 512
MISC_COLS = MLA_Q_RANK + MLA_KV_RANK + 3 * SLOT


def _cparams(sem):
    return pltpu.CompilerParams(dimension_semantics=sem, vmem_limit_bytes=VMEM_LIMIT)


def _dot(a, b):
    return jnp.dot(a, b, preferred_element_type=F32)


def _dot_nt(a, b):
    return lax.dot_general(a, b, (((1,), (1,)), ((), ())), preferred_element_type=F32)


def _split_bf16(x):
    hi = x.astype(BF16)
    lo = (x - hi.astype(F32)).astype(BF16)
    return hi, lo


def _sigmoid(x):
    return jax.nn.sigmoid(x)


def _silu(x):
    return x * _sigmoid(x)


def _ada_kernel(c_ref, w_ref, b_ref, o_ref):
    a = _silu(c_ref[...])
    ahi, alo = _split_bf16(a)
    w = w_ref[...]
    whi, wlo = _split_bf16(w)
    o_ref[...] = _dot(ahi, whi) + _dot(ahi, wlo) + _dot(alo, whi) + b_ref[...]


def _ada(c, w_ada, b_ada):
    nb = c.shape[0]
    cp = jnp.zeros((8, D_MODEL), F32).at[:nb].set(c)
    tn = 1536
    out = pl.pallas_call(
        _ada_kernel,
        out_shape=jax.ShapeDtypeStruct((8, 6 * D_MODEL), F32),
        grid=(6 * D_MODEL // tn,),
        in_specs=[pl.BlockSpec((8, D_MODEL), lambda j: (0, 0)),
                  pl.BlockSpec((D_MODEL, tn), lambda j: (0, j)),
                  pl.BlockSpec((1, tn), lambda j: (0, j))],
        out_specs=pl.BlockSpec((8, tn), lambda j: (0, j)),
        compiler_params=_cparams(("arbitrary",)),
        name="ada",
    )(cp, w_ada, b_ada.reshape(1, -1))
    mod = out[:nb].reshape(nb, 6, D_MODEL)
    return jnp.concatenate([mod, jnp.zeros((nb, 2, D_MODEL), F32)], axis=1)


_C_Q, _C_K, _C_V, _C_G, _C_GA, _C_GM = 0, 512, 1024, 2048, 3072, 4096
_C_MISC = 5120
IN_EXT = _C_MISC + MISC_COLS


def _segmean(xsq, bd):
    return _dot(xsq.astype(BF16), bd)


def _inproj_kernel(x_ref, mod_ref, g1_ref, win_ref, wgk_ref, bgk_ref, gqa_ref, wq_ref, gkva_ref,
                   wk_ref, wv_ref, vone_ref, bd_ref, lv_ref, cv_ref, cos_ref, sin_ref, cosc_ref, sinc_ref,
                   q_o, k_o, v_o, zg_o, zga_o, zgm_o, lgf_o, lgb_o, qm_o, km_o, vm_o, h_scr):
    x = x_ref[0]
    ms = jnp.mean(x * x, axis=-1, keepdims=True)
    h = (x * lax.rsqrt(ms + NORM_EPS) * g1_ref[...]) * (1.0 + mod_ref[0, 1:2, :]) + mod_ref[0, 0:1, :]
    h_scr[...] = h.astype(BF16)

    def proj(c0, n):
        return _dot(h_scr[...], win_ref[:, c0:c0 + n])

    q_o[0] = (proj(_C_Q, 512) * (GLA_DK ** -0.5)).astype(BF16)
    k_o[0] = proj(_C_K, 512).astype(BF16)
    v_o[0] = proj(_C_V, 1024).astype(BF16)
    zg_o[0] = _silu(proj(_C_G, 1024)).astype(BF16)
    zga_o[0] = _sigmoid(proj(_C_GA, 1024)).astype(BF16)
    zgm_o[0] = _sigmoid(proj(_C_GM, 1024)).astype(BF16)
    misc = proj(_C_MISC, MISC_COLS)

    bd = bd_ref[...]
    cos_t = cos_ref[...]
    sin_t = sin_ref[...]
    g_kn, g_kr, g_krs = lv_ref[2:3, :], lv_ref[3:4, :], lv_ref[4:5, :]
    scale = (MLA_NOPE + MLA_ROPE) ** -0.5 * math.log2(math.e)

    zqa = misc[:, 0:MLA_Q_RANK]
    qa = zqa * lax.rsqrt(jnp.mean(zqa * zqa, axis=-1, keepdims=True) + NORM_EPS) * gqa_ref[...]
    qq = _dot_nt(wq_ref[...], qa.astype(BF16))
    cos_c, sin_c = cosc_ref[...], sinc_ref[...]
    gc_q, gc_qs = cv_ref[:, 0:1], cv_ref[:, 1:2]
    half = MLA_ROPE // 2
    r0, r1, r2 = MLA_NOPE, MLA_NOPE + half, MLA_NOPE + MLA_ROPE
    for hh in range(MLA_HEADS):
        xq = qq[hh * SLOT:(hh + 1) * SLOT, :]
        xs = jnp.concatenate([xq[:r0], xq[r1:r2], xq[r0:r1], xq[r2:]], axis=0)
        r = lax.rsqrt(_dot(bd, (xq * xq).astype(BF16)) + NORM_EPS)
        qm = (xq * gc_q * cos_c + xs * gc_qs * sin_c) * (r * scale)
        qm_o[0, hh * SLOT:(hh + 1) * SLOT, :] = qm.astype(BF16)

    c0 = MLA_Q_RANK
    zkva = misc[:, c0:c0 + MLA_KV_RANK]
    kva = zkva * lax.rsqrt(jnp.mean(zkva * zkva, axis=-1, keepdims=True) + NORM_EPS) * gkva_ref[...]
    kvab = kva.astype(BF16)
    kk = _dot(kvab, wk_ref[...])
    for t in range(vm_o.shape[1]):
        vm_o[0, t] = (_dot_nt(wv_ref[...], kvab[t * ROW_TILE:(t + 1) * ROW_TILE]) + vone_ref[...]).astype(BF16)
    c0 += MLA_KV_RANK
    kr = misc[:, c0:c0 + SLOT]
    krs = misc[:, c0 + SLOT:c0 + 2 * SLOT]
    r_kr = lax.rsqrt(_segmean(kr * kr, bd) + NORM_EPS)
    kpe = (kr * g_kr * cos_t + krs * g_krs * sin_t) * r_kr
    for hh in range(MLA_HEADS):
        xk = kk[:, hh * SLOT:(hh + 1) * SLOT]
        r = lax.rsqrt(_segmean(xk * xk, bd) + NORM_EPS)
        km_o[0, :, hh * SLOT:(hh + 1) * SLOT] = (xk * r * g_kn + kpe).astype(BF16)

    c0 += 2 * SLOT
    zgate = misc[:, c0:c0 + SLOT].astype(BF16)
    pre = _dot(zgate, wgk_ref[...]) + bgk_ref[...]
    lg = (jnp.minimum(pre, 0.0) - jnp.log(1.0 + jnp.exp(-jnp.abs(pre)))) * (1.0 / GLA_GATE_NORM)
    lgf_o[0] = lg[:, :GLA_HEADS * GLA_DK]
    lgb_o[0] = lg[:, GLA_HEADS * GLA_DK:]


def _inproj(x, mod, wp):
    nb, s, d = x.shape
    tm = min(INPROJ_TILE, s)
    kt = min(ROW_TILE, s)
    const = lambda b, i: (0, 0)
    row = lambda b, i: (b, i, 0)

    def full(a):
        return pl.BlockSpec(a.shape, const)

    def out(n, dt):
        return jax.ShapeDtypeStruct((nb, s, n), dt), pl.BlockSpec((1, tm, n), row)

    outs = [out(512, BF16), out(512, BF16), out(1024, BF16), out(1024, BF16), out(1024, BF16),
            out(1024, BF16), out(512, F32), out(512, F32),
            (jax.ShapeDtypeStruct((nb, MLA_HEADS * SLOT, s), BF16),
             pl.BlockSpec((1, MLA_HEADS * SLOT, tm), lambda b, i: (b, 0, i))),
            out(MLA_HEADS * SLOT, BF16),
            (jax.ShapeDtypeStruct((nb, s // kt, MLA_HEADS * MLA_VROWS, kt), BF16),
             pl.BlockSpec((1, tm // kt, MLA_HEADS * MLA_VROWS, kt), lambda b, i: (b, i, 0, 0)))]
    return pl.pallas_call(
        _inproj_kernel,
        out_shape=[o[0] for o in outs],
        grid=(nb, s // tm),
        in_specs=[pl.BlockSpec((1, tm, d), row),
                  pl.BlockSpec((1, 8, d), lambda b, i: (b, 0, 0)),
                  full(wp["g1"]), full(wp["w_in"]), full(wp["w_gk"]), full(wp["b_gk"]),
                  full(wp["g_qa"]), full(wp["w_q"]), full(wp["g_kva"]), full(wp["w_k"]), full(wp["w_v"]),
                  full(wp["v_ones"]), full(wp["bd"]), full(wp["lanevecs"]), full(wp["colvecs"]),
                  pl.BlockSpec((tm, SLOT), lambda b, i: (i, 0)),
                  pl.BlockSpec((tm, SLOT), lambda b, i: (i, 0)),
                  pl.BlockSpec((SLOT, tm), lambda b, i: (0, i)),
                  pl.BlockSpec((SLOT, tm), lambda b, i: (0, i))],
        out_specs=[o[1] for o in outs],
        scratch_shapes=[pltpu.VMEM((tm, d), BF16)],
        compiler_params=_cparams(("arbitrary", "arbitrary")),
        name="inproj",
    )(x, mod, wp["g1"], wp["w_in"], wp["w_gk"], wp["b_gk"], wp["g_qa"], wp["w_q"], wp["g_kva"],
      wp["w_k"], wp["w_v"], wp["v_ones"], wp["bd"], wp["lanevecs"], wp["colvecs"],
      wp["cos"][:s], wp["sin"][:s], wp["cos"][:s].T, wp["sin"][:s].T)


def _dot_tn(a, b):
    return lax.dot_general(a, b, (((0,), (0,)), ((), ())), preferred_element_type=F32)


def _gla_direction(q_ref, k_ref, v_ref, g_ref, o_ref, st_ref, scr, cum, mask, reverse):
    qin_s, qmid_s, kmid_s, kout_s, tot_s, u_s = scr
    c = GLA_CHUNK
    nch = q_ref.shape[1] // c
    rows = [slice(j * c, (j + 1) * c) for j in range(nch)]
    ks = [slice(h * GLA_DK, (h + 1) * GLA_DK) for h in range(GLA_HEADS)]
    vs = [slice(h * GLA_DV, (h + 1) * GLA_DV) for h in range(GLA_HEADS)]

    for j in range(nch):
        hi, lo = _split_bf16(g_ref[0, rows[j], :])
        b = _dot(cum, hi) + _dot(cum, lo)
        mid = b[c // 2:c // 2 + 1, :]
        tot = b[0:1, :] if reverse else b[c - 1:c, :]
        q = q_ref[0, rows[j], :].astype(F32)
        k = k_ref[0, rows[j], :].astype(F32)
        qin_s[rows[j], :] = (q * jnp.exp(b)).astype(BF16)
        qmid_s[rows[j], :] = (q * jnp.exp(jnp.minimum(b - mid, GLA_EXP_CLAMP))).astype(BF16)
        kmid_s[rows[j], :] = (k * jnp.exp(jnp.minimum(mid - b, GLA_EXP_CLAMP))).astype(BF16)
        kout_s[rows[j], :] = (k * jnp.exp(tot - b)).astype(BF16)
        tot_s[j:j + 1, :] = tot

    attn = {}
    for j in range(nch):
        for h in range(GLA_HEADS):
            a = _dot_nt(qmid_s[rows[j], ks[h]], kmid_s[rows[j], ks[h]])
            attn[j, h] = jnp.where(mask, a, 0.0).astype(BF16)

    for j in range(nch):
        for h in range(GLA_HEADS):
            v = v_ref[0, rows[j], vs[h]]
            o_ref[0, rows[j], vs[h]] = _dot(attn[j, h], v)
            u_s[j * GLA_HEADS + h] = _dot_tn(kout_s[rows[j], ks[h]], v)

    for j in (range(nch - 1, -1, -1) if reverse else range(nch)):
        for h in range(GLA_HEADS):
            st = st_ref[h]
            o_ref[0, rows[j], vs[h]] += _dot(qin_s[rows[j], ks[h]], st.astype(BF16))
            tot = jnp.broadcast_to(tot_s[j:j + 1, ks[h]], (GLA_DK, GLA_DK))
            dec = jnp.exp(tot.T)
            st_ref[h] = jnp.concatenate([dec] * (GLA_DV // GLA_DK), axis=1) * st + u_s[j * GLA_HEADS + h]


def _gla_kernel(qf, kf, vf, gf, qb, kb, vb, gb, of_ref, ob_ref, sf_ref, sb_ref, *scr):
    @pl.when(pl.program_id(1) == 0)
    def _():
        sf_ref[...] = jnp.zeros_like(sf_ref)
        sb_ref[...] = jnp.zeros_like(sb_ref)

    c = GLA_CHUNK
    ri = lax.broadcasted_iota(jnp.int32, (c, c), 0)
    ci = lax.broadcasted_iota(jnp.int32, (c, c), 1)
    cum_f = jnp.where(ri >= ci, 1.0, 0.0).astype(BF16)
    cum_b = jnp.where(ci >= ri, 1.0, 0.0).astype(BF16)
    n = len(scr) // 2
    _gla_direction(qf, kf, vf, gf, of_ref, sf_ref, scr[:n], cum_f, ri >= ci, False)
    _gla_direction(qb, kb, vb, gb, ob_ref, sb_ref, scr[n:], cum_b, ci > ri, True)


def _gla(q, k, v, lgf, lgb):
    nb, s, _ = q.shape
    cb = min(GLA_BLOCK, s)
    ns = s // cb
    nch = cb // GLA_CHUNK
    fwd = lambda b, i: (b, i, 0)
    bwd = lambda b, i: (b, ns - 1 - i, 0)
    hk, hv = GLA_HEADS * GLA_DK, GLA_HEADS * GLA_DV

    def specs(im):
        return [pl.BlockSpec((1, cb, hk), im), pl.BlockSpec((1, cb, hk), im),
                pl.BlockSpec((1, cb, hv), im), pl.BlockSpec((1, cb, hk), im)]

    per_dir = [pltpu.VMEM((cb, hk), BF16)] * 4 + [pltpu.VMEM((8, hk), F32),
                                                  pltpu.VMEM((nch * GLA_HEADS, GLA_DK, GLA_DV), F32)]
    return pl.pallas_call(
        _gla_kernel,
        out_shape=[jax.ShapeDtypeStruct((nb, s, hv), F32)] * 2,
        grid=(nb, ns),
        in_specs=specs(fwd) + specs(bwd),
        out_specs=[pl.BlockSpec((1, cb, hv), fwd), pl.BlockSpec((1, cb, hv), bwd)],
        scratch_shapes=[pltpu.VMEM((GLA_HEADS, GLA_DK, GLA_DV), F32)] * 2 + per_dir * 2,
        compiler_params=_cparams(("arbitrary", "arbitrary")),
        name="gla",
    )(q, k, v, lgf, q, k, v, lgb)


def _mla_kernel(q_ref, k_ref, vt_ref, o_ref, sc0, sc1, p0, p1):
    tq = q_ref.shape[2]
    nk = vt_ref.shape[1]
    tk = vt_ref.shape[3]
    gu = MLA_UNROLL
    ng = nk // gu
    sc, pp = (sc0, sc1), (p0, p1)
    qs = [q_ref[0, hh * SLOT:(hh + 1) * SLOT, :] for hh in range(2)]
    tiles = [(u, hh) for u in range(gu) for hh in range(2)]

    def score_tile(g, slot, u, hh):
        r0 = pl.multiple_of((g * gu + u) * tk, tk)
        kt = k_ref[0, pl.ds(r0, tk), hh * SLOT:(hh + 1) * SLOT]
        st = _dot(kt, qs[hh])
        sc[slot][2 * u + hh] = st
        return jnp.max(st, axis=0, keepdims=True)

    def softmax_tile(slot, u, hh, m, tmax):
        m_new = jnp.maximum(m, tmax)
        pp[slot][2 * u + hh] = jnp.exp2(sc[slot][2 * u + hh] - m_new).astype(BF16)
        return m_new, jnp.exp2(m - m_new)

    def value_tile(g, slot, u, hh, alpha, acc):
        vt = vt_ref[0, g * gu + u, hh * MLA_VROWS:(hh + 1) * MLA_VROWS, :]
        return alpha * acc + _dot(vt, pp[slot][2 * u + hh])

    def issue_scores(g, slot):
        return tuple(score_tile(g, slot, u, hh) for u, hh in tiles)

    def softmax(slot, ms, tmax):
        ms, alphas = list(ms), []
        for u, hh in tiles:
            ms[hh], alpha = softmax_tile(slot, u, hh, ms[hh], tmax[2 * u + hh])
            alphas.append(alpha)
        return tuple(ms), tuple(alphas)

    def values(g, slot, alphas, accs):
        accs = list(accs)
        for u, hh in tiles:
            accs[hh] = value_tile(g, slot, u, hh, alphas[2 * u + hh], accs[hh])
        return tuple(accs)

    def step(g, slot, carry, more_scores):
        ms, alphas, accs, tmax = carry
        ms, accs, new_alphas, new_tmax = list(ms), list(accs), [], []
        for u, hh in tiles:
            idx = 2 * u + hh
            accs[hh] = value_tile(g - 1, 1 - slot, u, hh, alphas[idx], accs[hh])
            ms[hh], alpha = softmax_tile(slot, u, hh, ms[hh], tmax[idx])
            new_alphas.append(alpha)
            if more_scores:
                new_tmax.append(score_tile(g + 1, 1 - slot, u, hh))
        return tuple(ms), tuple(new_alphas), tuple(accs), tuple(new_tmax)

    def double_step(i, carry):
        g = 2 * i + 1
        carry = step(g, 1, carry, True)
        return step(g + 1, 0, carry, True)

    m0 = jnp.full((1, tq), -jnp.inf, F32)
    acc0 = jnp.zeros((MLA_VROWS, tq), F32)
    tmax = issue_scores(0, 0)
    ms, alphas = softmax(0, (m0, m0), tmax)
    tmax = issue_scores(1, 1)
    carry = (ms, alphas, (acc0, acc0), tmax)
    for i in range((ng - 2) // 2):
        carry = double_step(i, carry)
    ms, alphas, accs, _ = step(ng - 1, 1, carry, False)
    accs = values(ng - 1, 1, alphas, accs)
    outs = [(a[:MLA_V] / a[MLA_V:MLA_V + 1]).T for a in accs]
    o_ref[0] = jnp.concatenate(outs, axis=-1).astype(BF16)


def _mla(qm, km, vmt):
    nb, s, _ = km.shape
    nk, tk = vmt.shape[1], vmt.shape[3]
    tq = min(MLA_TQ, s)
    assert (nk // MLA_UNROLL) % 2 == 0 and nk % MLA_UNROLL == 0
    slot_tiles = 2 * MLA_UNROLL
    return pl.pallas_call(
        _mla_kernel,
        out_shape=jax.ShapeDtypeStruct((nb, s, MLA_HEADS * MLA_V), BF16),
        grid=(nb, MLA_HEADS // 2, s // tq),
        in_specs=[pl.BlockSpec((1, 2 * SLOT, tq), lambda b, p, i: (b, p, i)),
                  pl.BlockSpec((1, s, 2 * SLOT), lambda b, p, i: (b, 0, p)),
                  pl.BlockSpec((1, nk, 2 * MLA_VROWS, tk), lambda b, p, i: (b, 0, p, 0))],
        out_specs=pl.BlockSpec((1, tq, 2 * MLA_V), lambda b, p, i: (b, i, p)),
        scratch_shapes=[pltpu.VMEM((slot_tiles, tk, tq), F32)] * 2 + [pltpu.VMEM((slot_tiles, tk, tq), BF16)] * 2,
        compiler_params=_cparams(("arbitrary", "arbitrary", "arbitrary")),
        name="mla",
    )(qm, km, vmt)


def _merge_kernel(of_ref, ob_ref, zg_ref, zga_ref, zgm_ref, om_ref, x_ref, mod_ref, ggla_ref, wglo_ref,
                  wmo_ref, wout_ref, g2_ref, wr_ref, br_ref, ltri_ref,
                  x1_o, h2_o, route_o, cnt_o, cnt_scr):
    first = (pl.program_id(0) == 0) & (pl.program_id(1) == 0)

    @pl.when(first)
    def _():
        cnt_scr[...] = jnp.zeros_like(cnt_scr)

    tm = x_ref.shape[1]
    hm = tm // MERGE_SPLIT
    subs = [slice(i * hm, (i + 1) * hm) for i in range(MERGE_SPLIT)]

    def gla_gate(rs):
        o = of_ref[0, rs, :] + ob_ref[0, rs, :]
        zg = zg_ref[0, rs, :].astype(F32)
        parts = []
        for hh in range(GLA_HEADS):
            seg = o[:, hh * GLA_DV:(hh + 1) * GLA_DV]
            r = lax.rsqrt(jnp.mean(seg * seg, axis=-1, keepdims=True) + NORM_EPS)
            parts.append((seg * r * ggla_ref[...]) * zg[:, hh * GLA_DV:(hh + 1) * GLA_DV])
        return jnp.concatenate(parts, axis=-1).astype(BF16)

    ogs = [gla_gate(rs) for rs in subs]
    ys = [(_dot(og, wglo_ref[...]), _dot(om_ref[0, rs, :], wmo_ref[...])) for og, rs in zip(ogs, subs)]
    merged = [(zga_ref[0, rs, :].astype(F32) * y_gla
               + zgm_ref[0, rs, :].astype(F32) * y_mla).astype(BF16) for (y_gla, y_mla), rs in zip(ys, subs)]
    mixes = [_dot(m, wout_ref[...]) for m in merged]

    w = wr_ref[...]
    whi, wlo = _split_bf16(w)
    splits = []
    for mix, rs in zip(mixes, subs):
        x1 = x_ref[0, rs, :] + mod_ref[0, 2:3, :] * mix
        x1_o[0, rs, :] = x1
        ms = jnp.mean(x1 * x1, axis=-1, keepdims=True)
        h2 = (x1 * lax.rsqrt(ms + NORM_EPS) * g2_ref[...]) * (1.0 + mod_ref[0, 4:5, :]) + mod_ref[0, 3:4, :]
        for c in range(ROW_CHUNKS):
            h2_o[pl.ds(rs.start * ROW_CHUNKS + c, hm, stride=ROW_CHUNKS), :] = h2[:, c * LANES:(c + 1) * LANES]
        splits.append(_split_bf16(h2))

    logit_parts = [_dot(hhi, whi) + _dot(hhi, wlo) + _dot(hlo, whi) + br_ref[...] for hhi, hlo in splits]
    logits = jnp.concatenate(logit_parts, axis=0)
    lane = lax.broadcasted_iota(jnp.int32, (tm, LANES), 1)
    neg = -jnp.inf
    is_g = (lane >= N_EXPERTS) & (lane < N_EXPERTS + N_GROUPS)
    lgm = jnp.where(is_g, logits, neg)
    mg = jnp.max(lgm, axis=-1, keepdims=True)
    sg = jnp.sum(jnp.exp(lgm - mg), axis=-1, keepdims=True)
    p_top = 1.0 / sg
    g_idx = jnp.min(jnp.where(lgm == mg, lane, 2 * LANES), axis=-1, keepdims=True) - N_EXPERTS
    in_grp = (lane >= g_idx * EXPERTS_PER_GROUP) & (lane < (g_idx + 1) * EXPERTS_PER_GROUP)
    el = jnp.where(in_grp, logits, neg)
    m1 = jnp.max(el, axis=-1, keepdims=True)
    e1 = jnp.min(jnp.where(el == m1, lane, 2 * LANES), axis=-1, keepdims=True)
    el2 = jnp.where(lane == e1, neg, el)
    m2 = jnp.max(el2, axis=-1, keepdims=True)
    e2 = jnp.min(jnp.where(el2 == m2, lane, 2 * LANES), axis=-1, keepdims=True)
    se = jnp.sum(jnp.exp(el - m1), axis=-1, keepdims=True)
    pv1 = 1.0 / se
    pv2 = jnp.exp(m2 - m1) / se
    w1 = p_top * pv1 / (pv1 + pv2)
    w2 = p_top * pv2 / (pv1 + pv2)

    oh1 = lane == e1
    oh2 = lane == e2
    oh = jnp.where(oh1 | oh2, 1.0, 0.0)
    prefix = _dot(ltri_ref[...], oh.astype(BF16)) + cnt_scr[...]
    r1 = jnp.sum(jnp.where(oh1, prefix, 0.0), axis=-1, keepdims=True)
    r2 = jnp.sum(jnp.where(oh2, prefix, 0.0), axis=-1, keepdims=True)
    cnt_scr[...] = cnt_scr[...] + jnp.sum(oh, axis=0, keepdims=True)
    cnt_o[...] = jnp.broadcast_to(cnt_scr[...], cnt_o.shape)

    route = jnp.where(lane == 0, e1.astype(F32), 0.0)
    route = jnp.where(lane == 1, e2.astype(F32), route)
    route = jnp.where(lane == 2, r1, route)
    route = jnp.where(lane == 3, r2, route)
    route = jnp.where(lane == 4, w1, route)
    route = jnp.where(lane == 5, w2, route)
    route_o[0] = route


def _merge(o_f, o_b, zg, zga, zgm, o_mla, x, mod, wp):
    nb, s, d = x.shape
    tm = min(MERGE_TILE, s)
    const = lambda b, i: (0, 0)
    row = lambda b, i: (b, i, 0)

    def full(a):
        return pl.BlockSpec(a.shape, const)

    def rows(n):
        return pl.BlockSpec((1, tm, n), row)

    ltri = jnp.asarray(np.tril(np.ones((tm, tm), np.float32), -1), BF16)
    return pl.pallas_call(
        _merge_kernel,
        out_shape=[jax.ShapeDtypeStruct((nb, s, d), F32), jax.ShapeDtypeStruct((nb * s * ROW_CHUNKS, LANES), F32),
                   jax.ShapeDtypeStruct((nb, s, LANES), F32), jax.ShapeDtypeStruct((8, LANES), F32)],
        grid=(nb, s // tm),
        in_specs=[rows(1024), rows(1024), rows(1024), rows(1024), rows(1024), rows(MLA_HEADS * MLA_V),
                  rows(d), pl.BlockSpec((1, 8, d), lambda b, i: (b, 0, 0)),
                  full(wp["g_gla"]), full(wp["w_gla_o"]), full(wp["w_mla_o"]), full(wp["w_out"]),
                  full(wp["g2"]), full(wp["w_route"]), full(wp["b_route"]), full(ltri)],
        out_specs=[rows(d), pl.BlockSpec((tm * ROW_CHUNKS, LANES), lambda b, i: (b * (s // tm) + i, 0)),
                   rows(LANES), pl.BlockSpec((8, LANES), const)],
        scratch_shapes=[pltpu.VMEM((1, LANES), F32)],
        compiler_params=_cparams(("arbitrary", "arbitrary")),
        name="merge",
    )(o_f, o_b, zg, zga, zgm, o_mla, x, mod, wp["g_gla"], wp["w_gla_o"], wp["w_mla_o"], wp["w_out"],
      wp["g2"], wp["w_route"], wp["b_route"], ltri)


def _dispatch_kernel(seg_ref, dest_ref, h_ref, xs_out, zero_scr, sem):
    tt = h_ref.shape[0] // ROW_CHUNKS
    tile_rows = MOE_TILE * ROW_CHUNKS

    @pl.when(pl.program_id(0) == 0)
    def _():
        zero_scr[...] = jnp.zeros_like(zero_scr)

        def zero_copy(e):
            r0 = pl.multiple_of((seg_ref[0, e] - MOE_TILE) * ROW_CHUNKS, tile_rows)
            return pltpu.make_async_copy(zero_scr, xs_out.at[pl.ds(r0, tile_rows)], sem.at[1])

        for e in range(N_EXPERTS):
            @pl.when(seg_ref[1, e] > 0)
            def _():
                zero_copy(e).start()
        for e in range(N_EXPERTS):
            @pl.when(seg_ref[1, e] > 0)
            def _():
                zero_copy(e).wait()

        def tail_copy(ti):
            return pltpu.make_async_copy(zero_scr, xs_out.at[pl.ds(pl.multiple_of(ti * tile_rows, tile_rows), tile_rows)],
                                         sem.at[1])

        first_tail = seg_ref[0, N_EXPERTS - 1] // MOE_TILE
        n_tiles = xs_out.shape[0] // tile_rows
        lax.fori_loop(first_tail, n_tiles, lambda ti, c: (tail_copy(ti).start(), c)[1], 0)
        lax.fori_loop(first_tail, n_tiles, lambda ti, c: (tail_copy(ti).wait(), c)[1], 0)

    def copy(r, slot):
        d0 = pl.multiple_of(dest_ref[0, 0, 2 * r + slot] * ROW_CHUNKS, ROW_CHUNKS)
        return pltpu.make_async_copy(h_ref.at[pl.ds(pl.multiple_of(r * ROW_CHUNKS, ROW_CHUNKS), ROW_CHUNKS)],
                                     xs_out.at[pl.ds(d0, ROW_CHUNKS)], sem.at[0])

    def start(r, c):
        copy(r, 0).start(priority=0)
        copy(r, 1).start(priority=1)
        return c

    def wait(r, c):
        copy(r, 0).wait()
        copy(r, 1).wait()
        return c

    lax.fori_loop(0, tt, start, 0, unroll=8)
    lax.fori_loop(0, tt, wait, 0, unroll=8)


def _dispatch(h2, dest, segments, n_rows):
    t = h2.shape[0] // ROW_CHUNKS
    tt = min(MOVE_TILE, t)
    nt = t // tt
    return pl.pallas_call(
        _dispatch_kernel,
        out_shape=jax.ShapeDtypeStruct((n_rows * ROW_CHUNKS, LANES), F32),
        grid_spec=pltpu.PrefetchScalarGridSpec(
            num_scalar_prefetch=1,
            grid=(nt,),
            in_specs=[pl.BlockSpec((1, 1, 2 * tt), lambda i, seg: (i, 0, 0), memory_space=pltpu.SMEM),
                      pl.BlockSpec((tt * ROW_CHUNKS, LANES), lambda i, seg: (i, 0))],
            out_specs=pl.BlockSpec(memory_space=pl.ANY),
            scratch_shapes=[pltpu.VMEM((MOE_TILE * ROW_CHUNKS, LANES), F32), pltpu.SemaphoreType.DMA((2,))]),
        compiler_params=_cparams(("arbitrary",)),
        name="dispatch",
    )(segments, dest.reshape(nt, 1, 2 * tt), h2)


def _expert_kernel(te_ref, na_ref, xs_ref, wg_ref, wu_ref, wd_ref, ys_ref, xb_scr):
    active = pl.program_id(0) < na_ref[0]

    @pl.when(active)
    def _():
        for c in range(ROW_CHUNKS):
            xb_scr[:, c * LANES:(c + 1) * LANES] = xs_ref[pl.ds(c, MOE_TILE, stride=ROW_CHUNKS), :].astype(BF16)
        xb = xb_scr[...]
        a = _silu(_dot(xb, wg_ref[0].astype(BF16))) * _dot(xb, wu_ref[0].astype(BF16))
        y = _dot(a.astype(BF16), wd_ref[0].astype(BF16))
        for c in range(ROW_CHUNKS):
            ys_ref[pl.ds(c, MOE_TILE, stride=ROW_CHUNKS), :] = y[:, c * LANES:(c + 1) * LANES]

    @pl.when(jnp.logical_not(active))
    def _():
        ys_ref[...] = jnp.zeros_like(ys_ref)


def _experts(xs, tile_expert, n_active, wp):
    d = D_MODEL
    n_rows = xs.shape[0] // ROW_CHUNKS
    nt = n_rows // MOE_TILE
    rows = MOE_TILE * ROW_CHUNKS

    def tile(i, te, na):
        return (jnp.minimum(i, na[0] - 1), 0)

    def wmap(i, te, na):
        return (te[jnp.minimum(i, na[0] - 1)], 0, 0)

    return pl.pallas_call(
        _expert_kernel,
        out_shape=jax.ShapeDtypeStruct((n_rows * ROW_CHUNKS, LANES), F32),
        grid_spec=pltpu.PrefetchScalarGridSpec(
            num_scalar_prefetch=2,
            grid=(nt,),
            in_specs=[pl.BlockSpec((rows, LANES), tile),
                      pl.BlockSpec((1, d, D_EXPERT), wmap),
                      pl.BlockSpec((1, d, D_EXPERT), wmap),
                      pl.BlockSpec((1, D_EXPERT, d), wmap)],
            out_specs=pl.BlockSpec((rows, LANES), lambda i, te, na: (i, 0)),
            scratch_shapes=[pltpu.VMEM((MOE_TILE, d), BF16)]),
        compiler_params=_cparams(("arbitrary",)),
        name="experts",
    )(tile_expert, n_active, xs, wp["w_e_gate"], wp["w_e_up"], wp["w_e_down"])


def _combine_kernel(dest_ref, x1_ref, rw_ref, ga_ref, ys_hbm, o_ref, buf, sem):
    tt = x1_ref.shape[0]
    i = pl.program_id(0)
    nt = pl.num_programs(0) - 1
    par = lax.rem(i, 2)

    def copy(p, r, slot, d0):
        return pltpu.make_async_copy(ys_hbm.at[pl.ds(d0, ROW_CHUNKS)],
                                     buf.at[p, slot, pl.ds(pl.multiple_of(r * ROW_CHUNKS, ROW_CHUNKS), ROW_CHUNKS)],
                                     sem.at[p])

    @pl.when(i < nt)
    def _():
        def start(r, c):
            for slot in range(TOP_K):
                d0 = pl.multiple_of(dest_ref[0, 0, 2 * r + slot] * ROW_CHUNKS, ROW_CHUNKS)
                copy(par, r, slot, d0).start(priority=slot)
            return c

        lax.fori_loop(0, tt, start, 0, unroll=8)

    @pl.when(i > 0)
    def _():
        def wait(r, c):
            for slot in range(TOP_K):
                copy(1 - par, r, slot, 0).wait()
            return c

        lax.fori_loop(0, tt, wait, 0, unroll=8)
        rw = rw_ref[...]
        for c in range(ROW_CHUNKS):
            cs = slice(c * LANES, (c + 1) * LANES)
            rows = pl.ds(c, tt, stride=ROW_CHUNKS)
            moe = buf[1 - par, 0, rows, :] * rw[:, 4:5] + buf[1 - par, 1, rows, :] * rw[:, 5:6]
            o_ref[:, cs] = x1_ref[:, cs] + ga_ref[0, 5:6, cs] * moe


def _combine(x1, route, mod, ys, dest, s):
    t, d = x1.shape
    tt = min(MOVE_TILE, s)
    nt = t // tt
    per_b = s // tt
    lag = lambda i: jnp.maximum(i - 1, 0)
    return pl.pallas_call(
        _combine_kernel,
        out_shape=jax.ShapeDtypeStruct((t, d), F32),
        grid=(nt + 1,),
        in_specs=[pl.BlockSpec((1, 1, 2 * tt), lambda i: (jnp.minimum(i, nt - 1), 0, 0), memory_space=pltpu.SMEM),
                  pl.BlockSpec((tt, d), lambda i: (lag(i), 0)),
                  pl.BlockSpec((tt, LANES), lambda i: (lag(i), 0)),
                  pl.BlockSpec((1, 8, d), lambda i: (lag(i) // per_b, 0, 0)),
                  pl.BlockSpec(memory_space=pl.ANY)],
        out_specs=pl.BlockSpec((tt, d), lambda i: (lag(i), 0)),
        scratch_shapes=[pltpu.VMEM((2, TOP_K, tt * ROW_CHUNKS, LANES), F32), pltpu.SemaphoreType.DMA((2,))],
        compiler_params=_cparams(("arbitrary",)),
        name="combine",
    )(dest.reshape(nt, 1, 2 * tt), x1, route, mod, ys)


def _moe(x1, h2, route, counts, mod, wp):
    nb, s, d = x1.shape
    t = nb * s
    route2 = route.reshape(t, LANES)
    e_flat = route2[:, 0:2].astype(jnp.int32)
    rank = route2[:, 2:4].astype(jnp.int32)
    cnt = counts[0, :N_EXPERTS].astype(jnp.int32)
    padded = (cnt + MOE_TILE - 1) // MOE_TILE * MOE_TILE
    pad_end = jnp.cumsum(padded)
    pad_start = pad_end - padded
    is_e = e_flat[..., None] == jnp.arange(N_EXPERTS, dtype=jnp.int32)
    dest = (jnp.sum(jnp.where(is_e, pad_start, 0), axis=-1) + rank).reshape(t * TOP_K)
    n_rows = t * TOP_K + N_EXPERTS * MOE_TILE
    nt = n_rows // MOE_TILE
    tile_row = jnp.arange(nt, dtype=jnp.int32) * MOE_TILE
    tile_expert = jnp.minimum(jnp.sum((pad_end[None, :] <= tile_row[:, None]).astype(jnp.int32), axis=1),
                              N_EXPERTS - 1)
    n_active = (pad_end[-1:] // MOE_TILE).astype(jnp.int32)
    xs = _dispatch(h2, dest, jnp.stack([pad_end, padded]).astype(jnp.int32), n_rows)
    ys = _experts(xs, tile_expert, n_active, wp)
    out = _combine(x1.reshape(t, d), route2, mod, ys, dest, s)
    return out.reshape(nb, s, d)


def _swap_halves(a, axis=-1):
    n = a.shape[axis] // 2
    lo, hi = jnp.split(a, [n], axis=axis)
    return jnp.concatenate([hi, lo], axis=axis)


def _slot(parts, n_lead):
    out = jnp.zeros((n_lead, SLOT), F32)
    for off, a in parts:
        out = out.at[:, off:off + a.shape[-1]].set(a)
    return out


def _prep_weights(s_max, g_norm1, w_in, w_gk_fwd, b_gk_fwd, w_gk_bwd, b_gk_bwd, g_gla_out, w_gla_o, g_q_a,
                  w_q_b, g_kv_a, w_kv_b, g_q_nope, g_k_nope, g_q_rope, g_k_rope, w_mla_o, w_out, g_norm2,
                  w_group, b_group, w_router, b_router, w_e_gate, w_e_up, w_e_down):
    d = D_MODEL
    hk, hv = GLA_HEADS * GLA_DK, GLA_HEADS * GLA_DV
    offs = np.cumsum([0, hk, hk, hv, hv, GLA_RANK, GLA_RANK, MLA_Q_RANK, MLA_KV_RANK, MLA_ROPE, d, d])
    cols = [w_in[:, offs[i]:offs[i + 1]] for i in range(11)]
    wq_, wk_, wv_, wg_, wgf, wgb, wqa, wkva, wkr, wga, wgm = cols
    kr_slot = _slot([(MLA_NOPE, wkr)], d)
    krs_slot = _slot([(MLA_NOPE, _swap_halves(wkr))], d)
    gate_slot = _slot([(0, wgf), (GLA_RANK, wgb)], d)
    w_in_ext = jnp.concatenate([wq_, wk_, wv_, wg_, wga, wgm, wqa, wkva, kr_slot, krs_slot, gate_slot],
                               axis=1).astype(BF16)
    w_gk = jnp.zeros((SLOT, 2 * hk), F32)
    w_gk = w_gk.at[0:GLA_RANK, 0:hk].set(w_gk_fwd).at[GLA_RANK:2 * GLA_RANK, hk:].set(w_gk_bwd).astype(BF16)
    b_gk = jnp.concatenate([b_gk_fwd, b_gk_bwd]).reshape(1, -1)

    wq3 = w_q_b.reshape(MLA_Q_RANK, MLA_HEADS, MLA_NOPE + MLA_ROPE)
    zq = jnp.zeros((MLA_Q_RANK, MLA_HEADS, SLOT - MLA_NOPE - MLA_ROPE), F32)
    w_q = jnp.concatenate([wq3, zq], axis=-1).reshape(MLA_Q_RANK, -1).T.astype(BF16)
    wkv3 = w_kv_b.reshape(MLA_KV_RANK, MLA_HEADS, MLA_NOPE + MLA_V)
    w_k = jnp.concatenate([wkv3[..., :MLA_NOPE], jnp.zeros((MLA_KV_RANK, MLA_HEADS, SLOT - MLA_NOPE), F32)],
                          axis=-1).reshape(MLA_KV_RANK, -1).astype(BF16)
    w_v = jnp.concatenate([wkv3[..., MLA_NOPE:], jnp.zeros((MLA_KV_RANK, MLA_HEADS, MLA_VROWS - MLA_V), F32)],
                          axis=-1).reshape(MLA_KV_RANK, -1).T.astype(BF16)
    v_ones = jnp.zeros((MLA_HEADS, MLA_VROWS), F32).at[:, MLA_V].set(1.0).reshape(-1, 1)

    lanevecs = jnp.concatenate([
        _slot([(0, g_q_nope[None]), (MLA_NOPE, g_q_rope[None])], 1),
        _slot([(MLA_NOPE, _swap_halves(g_q_rope)[None])], 1),
        _slot([(0, g_k_nope[None])], 1),
        _slot([(MLA_NOPE, g_k_rope[None])], 1),
        _slot([(MLA_NOPE, _swap_halves(g_k_rope)[None])], 1),
        jnp.zeros((3, SLOT), F32)], axis=0)
    seg = np.zeros((SLOT, SLOT), np.float32)
    seg[:MLA_NOPE, :MLA_NOPE] = 1.0 / MLA_NOPE
    seg[MLA_NOPE:MLA_NOPE + MLA_ROPE, MLA_NOPE:MLA_NOPE + MLA_ROPE] = 1.0 / MLA_ROPE
    bd = jnp.asarray(seg, BF16)

    half = MLA_ROPE // 2
    inv = ROPE_BASE ** (-jnp.arange(half, dtype=F32) / half)
    ang = jnp.arange(s_max, dtype=F32)[:, None] * inv[None, :]
    cos, sin = jnp.cos(ang), jnp.sin(ang)
    pad = jnp.zeros((s_max, SLOT - MLA_NOPE - MLA_ROPE), F32)
    cos_t = jnp.concatenate([jnp.ones((s_max, MLA_NOPE), F32), cos, cos, pad], axis=1)
    sin_t = jnp.concatenate([jnp.zeros((s_max, MLA_NOPE), F32), -sin, sin, pad], axis=1)

    w_route = jnp.zeros((d, LANES), F32).at[:, :N_EXPERTS].set(w_router)
    w_route = w_route.at[:, N_EXPERTS:N_EXPERTS + N_GROUPS].set(w_group)
    b_route = jnp.zeros((1, LANES), F32).at[0, :N_EXPERTS].set(b_router)
    b_route = b_route.at[0, N_EXPERTS:N_EXPERTS + N_GROUPS].set(b_group)

    return dict(
        g1=g_norm1.reshape(1, d), w_in=w_in_ext, w_gk=w_gk, b_gk=b_gk,
        g_qa=g_q_a.reshape(1, -1), w_q=w_q, g_kva=g_kv_a.reshape(1, -1), w_k=w_k, w_v=w_v,
        v_ones=v_ones, bd=bd, lanevecs=lanevecs, colvecs=lanevecs.T, cos=cos_t, sin=sin_t,
        g_gla=g_gla_out.reshape(1, -1), w_gla_o=w_gla_o.astype(BF16), w_mla_o=w_mla_o.astype(BF16),
        w_out=w_out.astype(BF16), g2=g_norm2.reshape(1, d), w_route=w_route, b_route=b_route,
        w_e_gate=w_e_gate, w_e_up=w_e_up, w_e_down=w_e_down)


def _layer(x, mod, wp):
    q, k, v, zg, zga, zgm, lgf, lgb, qm, km, vm = _inproj(x, mod, wp)
    o_f, o_b = _gla(q, k, v, lgf, lgb)
    o_mla = _mla(qm, km, vm)
    x1, h2, route, counts = _merge(o_f, o_b, zg, zga, zgm, o_mla, x, mod, wp)
    return _moe(x1, h2, route, counts, mod, wp)


def kernel(x_prompt, x_sample, c_prompt, c_sample, w_ada, b_ada, g_norm1, w_in, w_gk_fwd, b_gk_fwd, w_gk_bwd, b_gk_bwd, g_gla_out, w_gla_o, g_q_a, w_q_b, g_kv_a, w_kv_b, g_q_nope, g_k_nope, g_q_rope, g_k_rope, w_mla_o, w_out, g_norm2, w_group, b_group, w_router, b_router, w_e_gate, w_e_up, w_e_down):
    assert w_ada.shape[0] == 1, "single-layer trunk"
    s_max = max(x_prompt.shape[1], x_sample.shape[1])
    wp = _prep_weights(s_max, *[p[0] for p in (
        g_norm1, w_in, w_gk_fwd, b_gk_fwd, w_gk_bwd, b_gk_bwd, g_gla_out, w_gla_o, g_q_a, w_q_b, g_kv_a,
        w_kv_b, g_q_nope, g_k_nope, g_q_rope, g_k_rope, w_mla_o, w_out, g_norm2, w_group, b_group,
        w_router, b_router, w_e_gate, w_e_up, w_e_down)])
    n_prompt = x_prompt.shape[0]
    mod = _ada(jnp.concatenate([c_prompt, c_sample], axis=0), w_ada[0], b_ada[0])
    y_prompt = _layer(x_prompt, mod[:n_prompt], wp)
    y_sample = _layer(x_sample, mod[n_prompt:], wp)
    return (y_prompt, y_sample)
```

```python
import math

import numpy as np
import jax
import jax.numpy as jnp
from jax import lax
from jax.experimental import pallas as pl
from jax.experimental.pallas import tpu as pltpu

F32 = jnp.float32
BF16 = jnp.bfloat16

D_MODEL = 1024
GLA_HEADS, GLA_DK, GLA_DV, GLA_RANK = 4, 128, 256, 16
GLA_GATE_NORM = 16.0
MLA_HEADS, MLA_Q_RANK, MLA_KV_RANK = 8, 256, 128
MLA_NOPE, MLA_ROPE, MLA_V = 64, 32, 64
ROPE_BASE = 10000.0
N_GROUPS, EXPERTS_PER_GROUP, TOP_K, D_EXPERT = 4, 8, 2, 512
N_EXPERTS = N_GROUPS * EXPERTS_PER_GROUP
NORM_EPS = 1e-6

LANES = 128
ROW_CHUNKS = D_MODEL // LANES
SLOT = LANES
VMEM_LIMIT = 56 * 1024 * 1024

INPROJ_TILE = 512
ROW_TILE = 256
GLA_CHUNK = 64
GLA_BLOCK = 512
GLA_EXP_CLAMP = 80.0
MLA_TQ = 512
MLA_VROWS = MLA_V + 16
MLA_UNROLL = 2
MOVE_TILE = 512
MERGE_TILE = 512
MERGE_SPLIT = 2
MOE_TILE = 512
MISC_COLS = MLA_Q_RANK + MLA_KV_RANK + 3 * SLOT


def _cparams(sem):
    return pltpu.CompilerParams(dimension_semantics=sem, vmem_limit_bytes=VMEM_LIMIT)


def _dot(a, b):
    return jnp.dot(a, b, preferred_element_type=F32)


def _dot_nt(a, b):
    return lax.dot_general(a, b, (((1,), (1,)), ((), ())), preferred_element_type=F32)


def _split_bf16(x):
    hi = x.astype(BF16)
    lo = (x - hi.astype(F32)).astype(BF16)
    return hi, lo


def _sigmoid(x):
    return jax.nn.sigmoid(x)


def _silu(x):
    return x * _sigmoid(x)


def _ada_kernel(c_ref, w_ref, b_ref, o_ref):
    a = _silu(c_ref[...])
    ahi, alo = _split_bf16(a)
    w = w_ref[...]
    whi, wlo = _split_bf16(w)
    o_ref[...] = _dot(ahi, whi) + _dot(ahi, wlo) + _dot(alo, whi) + b_ref[...]


def _ada(c, w_ada, b_ada):
    nb = c.shape[0]
    cp = jnp.zeros((8, D_MODEL), F32).at[:nb].set(c)
    tn = 1536
    out = pl.pallas_call(
        _ada_kernel,
        out_shape=jax.ShapeDtypeStruct((8, 6 * D_MODEL), F32),
        grid=(6 * D_MODEL // tn,),
        in_specs=[pl.BlockSpec((8, D_MODEL), lambda j: (0, 0)),
                  pl.BlockSpec((D_MODEL, tn), lambda j: (0, j)),
                  pl.BlockSpec((1, tn), lambda j: (0, j))],
        out_specs=pl.BlockSpec((8, tn), lambda j: (0, j)),
        compiler_params=_cparams(("arbitrary",)),
        name="ada",
    )(cp, w_ada, b_ada.reshape(1, -1))
    mod = out[:nb].reshape(nb, 6, D_MODEL)
    return jnp.concatenate([mod, jnp.zeros((nb, 2, D_MODEL), F32)], axis=1)


_C_Q, _C_K, _C_V, _C_G, _C_GA, _C_GM = 0, 512, 1024, 2048, 3072, 4096
_C_MISC = 5120
IN_EXT = _C_MISC + MISC_COLS


def _segmean(xsq, bd):
    return _dot(xsq.astype(BF16), bd)


def _inproj_kernel(x_ref, mod_ref, g1_ref, win_ref, wgk_ref, bgk_ref, gqa_ref, wq_ref, gkva_ref,
                   wk_ref, wv_ref, vone_ref, bd_ref, lv_ref, cv_ref, cos_ref, sin_ref, cosc_ref, sinc_ref,
                   q_o, k_o, v_o, zg_o, zga_o, zgm_o, lgf_o, lgb_o, qm_o, km_o, vm_o, h_scr):
    x = x_ref[0]
    ms = jnp.mean(x * x, axis=-1, keepdims=True)
    h = (x * lax.rsqrt(ms + NORM_EPS) * g1_ref[...]) * (1.0 + mod_ref[0, 1:2, :]) + mod_ref[0, 0:1, :]
    h_scr[...] = h.astype(BF16)

    def proj(c0, n):
        return _dot(h_scr[...], win_ref[:, c0:c0 + n])

    q_o[0] = (proj(_C_Q, 512) * (GLA_DK ** -0.5)).astype(BF16)
    k_o[0] = proj(_C_K, 512).astype(BF16)
    v_o[0] = proj(_C_V, 1024).astype(BF16)
    zg_o[0] = _silu(proj(_C_G, 1024)).astype(BF16)
    zga_o[0] = _sigmoid(proj(_C_GA, 1024)).astype(BF16)
    zgm_o[0] = _sigmoid(proj(_C_GM, 1024)).astype(BF16)
    misc = proj(_C_MISC, MISC_COLS)

    bd = bd_ref[...]
    cos_t = cos_ref[...]
    sin_t = sin_ref[...]
    g_kn, g_kr, g_krs = lv_ref[2:3, :], lv_ref[3:4, :], lv_ref[4:5, :]
    scale = (MLA_NOPE + MLA_ROPE) ** -0.5 * math.log2(math.e)

    zqa = misc[:, 0:MLA_Q_RANK]
    qa = zqa * lax.rsqrt(jnp.mean(zqa * zqa, axis=-1, keepdims=True) + NORM_EPS) * gqa_ref[...]
    qq = _dot_nt(wq_ref[...], qa.astype(BF16))
    cos_c, sin_c = cosc_ref[...], sinc_ref[...]
    gc_q, gc_qs = cv_ref[:, 0:1], cv_ref[:, 1:2]
    half = MLA_ROPE // 2
    r0, r1, r2 = MLA_NOPE, MLA_NOPE + half, MLA_NOPE + MLA_ROPE
    for hh in range(MLA_HEADS):
        xq = qq[hh * SLOT:(hh + 1) * SLOT, :]
        xs = jnp.concatenate([xq[:r0], xq[r1:r2], xq[r0:r1], xq[r2:]], axis=0)
        r = lax.rsqrt(_dot(bd, (xq * xq).astype(BF16)) + NORM_EPS)
        qm = (xq * gc_q * cos_c + xs * gc_qs * sin_c) * (r * scale)
        qm_o[0, hh * SLOT:(hh + 1) * SLOT, :] = qm.astype(BF16)

    c0 = MLA_Q_RANK
    zkva = misc[:, c0:c0 + MLA_KV_RANK]
    kva = zkva * lax.rsqrt(jnp.mean(zkva * zkva, axis=-1, keepdims=True) + NORM_EPS) * gkva_ref[...]
    kvab = kva.astype(BF16)
    kk = _dot(kvab, wk_ref[...])
    for t in range(vm_o.shape[1]):
        vm_o[0, t] = (_dot_nt(wv_ref[...], kvab[t * ROW_TILE:(t + 1) * ROW_TILE]) + vone_ref[...]).astype(BF16)
    c0 += MLA_KV_RANK
    kr = misc[:, c0:c0 + SLOT]
    krs = misc[:, c0 + SLOT:c0 + 2 * SLOT]
    r_kr = lax.rsqrt(_segmean(kr * kr, bd) + NORM_EPS)
    kpe = (kr * g_kr * cos_t + krs * g_krs * sin_t) * r_kr
    for hh in range(MLA_HEADS):
        xk = kk[:, hh * SLOT:(hh + 1) * SLOT]
        r = lax.rsqrt(_segmean(xk * xk, bd) + NORM_EPS)
        km_o[0, :, hh * SLOT:(hh + 1) * SLOT] = (xk * r * g_kn + kpe).astype(BF16)

    c0 += 2 * SLOT
    zgate = misc[:, c0:c0 + SLOT].astype(BF16)
    pre = _dot(zgate, wgk_ref[...]) + bgk_ref[...]
    lg = (jnp.minimum(pre, 0.0) - jnp.log(1.0 + jnp.exp(-jnp.abs(pre)))) * (1.0 / GLA_GATE_NORM)
    lgf_o[0] = lg[:, :GLA_HEADS * GLA_DK]
    lgb_o[0] = lg[:, GLA_HEADS * GLA_DK:]


def _inproj(x, mod, wp):
    nb, s, d = x.shape
    tm = min(INPROJ_TILE, s)
    kt = min(ROW_TILE, s)
    const = lambda b, i: (0, 0)
    row = lambda b, i: (b, i, 0)

    def full(a):
        return pl.BlockSpec(a.shape, const)

    def out(n, dt):
        return jax.ShapeDtypeStruct((nb, s, n), dt), pl.BlockSpec((1, tm, n), row)

    outs = [out(512, BF16), out(512, BF16), out(1024, BF16), out(1024, BF16), out(1024, BF16),
            out(1024, BF16), out(512, F32), out(512, F32),
            (jax.ShapeDtypeStruct((nb, MLA_HEADS * SLOT, s), BF16),
             pl.BlockSpec((1, MLA_HEADS * SLOT, tm), lambda b, i: (b, 0, i))),
            out(MLA_HEADS * SLOT, BF16),
            (jax.ShapeDtypeStruct((nb, s // kt, MLA_HEADS * MLA_VROWS, kt), BF16),
             pl.BlockSpec((1, tm // kt, MLA_HEADS * MLA_VROWS, kt), lambda b, i: (b, i, 0, 0)))]
    return pl.pallas_call(
        _inproj_kernel,
        out_shape=[o[0] for o in outs],
        grid=(nb, s // tm),
        in_specs=[pl.BlockSpec((1, tm, d), row),
                  pl.BlockSpec((1, 8, d), lambda b, i: (b, 0, 0)),
                  full(wp["g1"]), full(wp["w_in"]), full(wp["w_gk"]), full(wp["b_gk"]),
                  full(wp["g_qa"]), full(wp["w_q"]), full(wp["g_kva"]), full(wp["w_k"]), full(wp["w_v"]),
                  full(wp["v_ones"]), full(wp["bd"]), full(wp["lanevecs"]), full(wp["colvecs"]),
                  pl.BlockSpec((tm, SLOT), lambda b, i: (i, 0)),
                  pl.BlockSpec((tm, SLOT), lambda b, i: (i, 0)),
                  pl.BlockSpec((SLOT, tm), lambda b, i: (0, i)),
                  pl.BlockSpec((SLOT, tm), lambda b, i: (0, i))],
        out_specs=[o[1] for o in outs],
        scratch_shapes=[pltpu.VMEM((tm, d), BF16)],
        compiler_params=_cparams(("arbitrary", "arbitrary")),
        name="inproj",
    )(x, mod, wp["g1"], wp["w_in"], wp["w_gk"], wp["b_gk"], wp["g_qa"], wp["w_q"], wp["g_kva"],
      wp["w_k"], wp["w_v"], wp["v_ones"], wp["bd"], wp["lanevecs"], wp["colvecs"],
      wp["cos"][:s], wp["sin"][:s], wp["cos"][:s].T, wp["sin"][:s].T)


def _dot_tn(a, b):
    return lax.dot_general(a, b, (((0,), (0,)), ((), ())), preferred_element_type=F32)


def _gla_direction(q_ref, k_ref, v_ref, g_ref, o_ref, st_ref, scr, cum, mask, reverse):
    qin_s, qmid_s, kmid_s, kout_s, tot_s, u_s = scr
    c = GLA_CHUNK
    nch = q_ref.shape[1] // c
    rows = [slice(j * c, (j + 1) * c) for j in range(nch)]
    ks = [slice(h * GLA_DK, (h + 1) * GLA_DK) for h in range(GLA_HEADS)]
    vs = [slice(h * GLA_DV, (h + 1) * GLA_DV) for h in range(GLA_HEADS)]

    for j in range(nch):
        hi, lo = _split_bf16(g_ref[0, rows[j], :])
        b = _dot(cum, hi) + _dot(cum, lo)
        mid = b[c // 2:c // 2 + 1, :]
        tot = b[0:1, :] if reverse else b[c - 1:c, :]
        q = q_ref[0, rows[j], :].astype(F32)
        k = k_ref[0, rows[j], :].astype(F32)
        qin_s[rows[j], :] = (q * jnp.exp(b)).astype(BF16)
        qmid_s[rows[j], :] = (q * jnp.exp(jnp.minimum(b - mid, GLA_EXP_CLAMP))).astype(BF16)
        kmid_s[rows[j], :] = (k * jnp.exp(jnp.minimum(mid - b, GLA_EXP_CLAMP))).astype(BF16)
        kout_s[rows[j], :] = (k * jnp.exp(tot - b)).astype(BF16)
        tot_s[j:j + 1, :] = tot

    attn = {}
    for j in range(nch):
        for h in range(GLA_HEADS):
            a = _dot_nt(qmid_s[rows[j], ks[h]], kmid_s[rows[j], ks[h]])
            attn[j, h] = jnp.where(mask, a, 0.0).astype(BF16)

    for j in range(nch):
        for h in range(GLA_HEADS):
            v = v_ref[0, rows[j], vs[h]]
            o_ref[0, rows[j], vs[h]] = _dot(attn[j, h], v)
            u_s[j * GLA_HEADS + h] = _dot_tn(kout_s[rows[j], ks[h]], v)

    for j in (range(nch - 1, -1, -1) if reverse else range(nch)):
        for h in range(GLA_HEADS):
            st = st_ref[h]
            o_ref[0, rows[j], vs[h]] += _dot(qin_s[rows[j], ks[h]], st.astype(BF16))
            tot = jnp.broadcast_to(tot_s[j:j + 1, ks[h]], (GLA_DK, GLA_DK))
            dec = jnp.exp(tot.T)
            st_ref[h] = jnp.concatenate([dec] * (GLA_DV // GLA_DK), axis=1) * st + u_s[j * GLA_HEADS + h]


def _gla_kernel(qf, kf, vf, gf, qb, kb, vb, gb, of_ref, ob_ref, sf_ref, sb_ref, *scr):
    @pl.when(pl.program_id(1) == 0)
    def _():
        sf_ref[...] = jnp.zeros_like(sf_ref)
        sb_ref[...] = jnp.zeros_like(sb_ref)

    c = GLA_CHUNK
    ri = lax.broadcasted_iota(jnp.int32, (c, c), 0)
    ci = lax.broadcasted_iota(jnp.int32, (c, c), 1)
    cum_f = jnp.where(ri >= ci, 1.0, 0.0).astype(BF16)
    cum_b = jnp.where(ci >= ri, 1.0, 0.0).astype(BF16)
    n = len(scr) // 2
    _gla_direction(qf, kf, vf, gf, of_ref, sf_ref, scr[:n], cum_f, ri >= ci, False)
    _gla_direction(qb, kb, vb, gb, ob_ref, sb_ref, scr[n:], cum_b, ci > ri, True)


def _gla(q, k, v, lgf, lgb):
    nb, s, _ = q.shape
    cb = min(GLA_BLOCK, s)
    ns = s // cb
    nch = cb // GLA_CHUNK
    fwd = lambda b, i: (b, i, 0)
    bwd = lambda b, i: (b, ns - 1 - i, 0)
    hk, hv = GLA_HEADS * GLA_DK, GLA_HEADS * GLA_DV

    def specs(im):
        return [pl.BlockSpec((1, cb, hk), im), pl.BlockSpec((1, cb, hk), im),
                pl.BlockSpec((1, cb, hv), im), pl.BlockSpec((1, cb, hk), im)]

    per_dir = [pltpu.VMEM((cb, hk), BF16)] * 4 + [pltpu.VMEM((8, hk), F32),
                                                  pltpu.VMEM((nch * GLA_HEADS, GLA_DK, GLA_DV), F32)]
    return pl.pallas_call(
        _gla_kernel,
        out_shape=[jax.ShapeDtypeStruct((nb, s, hv), F32)] * 2,
        grid=(nb, ns),
        in_specs=specs(fwd) + specs(bwd),
        out_specs=[pl.BlockSpec((1, cb, hv), fwd), pl.BlockSpec((1, cb, hv), bwd)],
        scratch_shapes=[pltpu.VMEM((GLA_HEADS, GLA_DK, GLA_DV), F32)] * 2 + per_dir * 2,
        compiler_params=_cparams(("arbitrary", "arbitrary")),
        name="gla",
    )(q, k, v, lgf, q, k, v, lgb)


def _mla_kernel(q_ref, k_ref, vt_ref, o_ref, sc0, sc1, p0, p1):
    tq = q_ref.shape[2]
    nk = vt_ref.shape[1]
    tk = vt_ref.shape[3]
    gu = MLA_UNROLL
    ng = nk // gu
    sc, pp = (sc0, sc1), (p0, p1)
    qs = [q_ref[0, hh * SLOT:(hh + 1) * SLOT, :] for hh in range(2)]
    tiles = [(u, hh) for hh in range(2) for u in range(gu)]

    def score_tile(g, slot, u, hh):
        r0 = pl.multiple_of((g * gu + u) * tk, tk)
        kt = k_ref[0, pl.ds(r0, tk), hh * SLOT:(hh + 1) * SLOT]
        st = _dot(kt, qs[hh])
        sc[slot][2 * u + hh] = st
        return jnp.max(st, axis=0, keepdims=True)

    def softmax_tile(slot, u, hh, m, tmax):
        m_new = jnp.maximum(m, tmax)
        pp[slot][2 * u + hh] = jnp.exp2(sc[slot][2 * u + hh] - m_new).astype(BF16)
        return m_new, jnp.exp2(m - m_new)

    def value_tile(g, slot, u, hh, alpha, acc):
        vt = vt_ref[0, g * gu + u, hh * MLA_VROWS:(hh + 1) * MLA_VROWS, :]
        return alpha * acc + _dot(vt, pp[slot][2 * u + hh])

    def issue_scores(g, slot):
        return tuple(score_tile(g, slot, u, hh) for u, hh in tiles)

    def softmax(slot, ms, tmax):
        ms, alphas = list(ms), []
        for u, hh in tiles:
            ms[hh], alpha = softmax_tile(slot, u, hh, ms[hh], tmax[2 * u + hh])
            alphas.append(alpha)
        return tuple(ms), tuple(alphas)

    def values(g, slot, alphas, accs):
        accs = list(accs)
        for u, hh in tiles:
            accs[hh] = value_tile(g, slot, u, hh, alphas[2 * u + hh], accs[hh])
        return tuple(accs)

    def step(g, slot, carry, more_scores):
        ms, alphas, accs, tmax = carry
        ms, accs, new_alphas, new_tmax = list(ms), list(accs), [], []
        for u, hh in tiles:
            idx = 2 * u + hh
            accs[hh] = value_tile(g - 1, 1 - slot, u, hh, alphas[idx], accs[hh])
            ms[hh], alpha = softmax_tile(slot, u, hh, ms[hh], tmax[idx])
            new_alphas.append(alpha)
            if more_scores:
                new_tmax.append(score_tile(g + 1, 1 - slot, u, hh))
        return tuple(ms), tuple(new_alphas), tuple(accs), tuple(new_tmax)

    def double_step(i, carry):
        g = 2 * i + 1
        carry = step(g, 1, carry, True)
        return step(g + 1, 0, carry, True)

    m0 = jnp.full((1, tq), -jnp.inf, F32)
    acc0 = jnp.zeros((MLA_VROWS, tq), F32)
    tmax = issue_scores(0, 0)
    ms, alphas = softmax(0, (m0, m0), tmax)
    tmax = issue_scores(1, 1)
    carry = (ms, alphas, (acc0, acc0), tmax)
    for i in range((ng - 2) // 2):
        carry = double_step(i, carry)
    ms, alphas, accs, _ = step(ng - 1, 1, carry, False)
    accs = values(ng - 1, 1, alphas, accs)
    outs = [(a[:MLA_V] / a[MLA_V:MLA_V + 1]).T for a in accs]
    o_ref[0] = jnp.concatenate(outs, axis=-1).astype(BF16)


def _mla(qm, km, vmt):
    nb, s, _ = km.shape
    nk, tk = vmt.shape[1], vmt.shape[3]
    tq = min(MLA_TQ, s)
    assert (nk // MLA_UNROLL) % 2 == 0 and nk % MLA_UNROLL == 0
    slot_tiles = 2 * MLA_UNROLL
    return pl.pallas_call(
        _mla_kernel,
        out_shape=jax.ShapeDtypeStruct((nb, s, MLA_HEADS * MLA_V), BF16),
        grid=(nb, MLA_HEADS // 2, s // tq),
        in_specs=[pl.BlockSpec((1, 2 * SLOT, tq), lambda b, p, i: (b, p, i)),
                  pl.BlockSpec((1, s, 2 * SLOT), lambda b, p, i: (b, 0, p)),
                  pl.BlockSpec((1, nk, 2 * MLA_VROWS, tk), lambda b, p, i: (b, 0, p, 0))],
        out_specs=pl.BlockSpec((1, tq, 2 * MLA_V), lambda b, p, i: (b, i, p)),
        scratch_shapes=[pltpu.VMEM((slot_tiles, tk, tq), F32)] * 2 + [pltpu.VMEM((slot_tiles, tk, tq), BF16)] * 2,
        compiler_params=_cparams(("arbitrary", "arbitrary", "arbitrary")),
        name="mla",
    )(qm, km, vmt)


def _merge_kernel(of_ref, ob_ref, zg_ref, zga_ref, zgm_ref, om_ref, x_ref, mod_ref, ggla_ref, wglo_ref,
                  wmo_ref, wout_ref, g2_ref, wr_ref, br_ref, ltri_ref,
                  x1_o, h2_o, route_o, cnt_o, cnt_scr):
    first = (pl.program_id(0) == 0) & (pl.program_id(1) == 0)

    @pl.when(first)
    def _():
        cnt_scr[...] = jnp.zeros_like(cnt_scr)

    tm = x_ref.shape[1]
    hm = tm // MERGE_SPLIT
    subs = [slice(i * hm, (i + 1) * hm) for i in range(MERGE_SPLIT)]

    def gla_gate(rs):
        o = of_ref[0, rs, :] + ob_ref[0, rs, :]
        zg = zg_ref[0, rs, :].astype(F32)
        parts = []
        for hh in range(GLA_HEADS):
            seg = o[:, hh * GLA_DV:(hh + 1) * GLA_DV]
            r = lax.rsqrt(jnp.mean(seg * seg, axis=-1, keepdims=True) + NORM_EPS)
            parts.append((seg * r * ggla_ref[...]) * zg[:, hh * GLA_DV:(hh + 1) * GLA_DV])
        return jnp.concatenate(parts, axis=-1).astype(BF16)

    ogs = [gla_gate(rs) for rs in subs]
    ys = [(_dot(og, wglo_ref[...]), _dot(om_ref[0, rs, :], wmo_ref[...])) for og, rs in zip(ogs, subs)]
    merged = [(zga_ref[0, rs, :].astype(F32) * y_gla
               + zgm_ref[0, rs, :].astype(F32) * y_mla).astype(BF16) for (y_gla, y_mla), rs in zip(ys, subs)]
    mixes = [_dot(m, wout_ref[...]) for m in merged]

    w = wr_ref[...]
    whi, wlo = _split_bf16(w)
    splits = []
    for mix, rs in zip(mixes, subs):
        x1 = x_ref[0, rs, :] + mod_ref[0, 2:3, :] * mix
        x1_o[0, rs, :] = x1
        ms = jnp.mean(x1 * x1, axis=-1, keepdims=True)
        h2 = (x1 * lax.rsqrt(ms + NORM_EPS) * g2_ref[...]) * (1.0 + mod_ref[0, 4:5, :]) + mod_ref[0, 3:4, :]
        for c in range(ROW_CHUNKS):
            h2_o[pl.ds(rs.start * ROW_CHUNKS + c, hm, stride=ROW_CHUNKS), :] = h2[:, c * LANES:(c + 1) * LANES]
        splits.append(_split_bf16(h2))

    logit_parts = [_dot(hhi, whi) + _dot(hhi, wlo) + _dot(hlo, whi) + br_ref[...] for hhi, hlo in splits]
    logits = jnp.concatenate(logit_parts, axis=0)
    lane = lax.broadcasted_iota(jnp.int32, (tm, LANES), 1)
    neg = -jnp.inf
    is_g = (lane >= N_EXPERTS) & (lane < N_EXPERTS + N_GROUPS)
    lgm = jnp.where(is_g, logits, neg)
    mg = jnp.max(lgm, axis=-1, keepdims=True)
    sg = jnp.sum(jnp.exp(lgm - mg), axis=-1, keepdims=True)
    p_top = 1.0 / sg
    g_idx = jnp.min(jnp.where(lgm == mg, lane, 2 * LANES), axis=-1, keepdims=True) - N_EXPERTS
    in_grp = (lane >= g_idx * EXPERTS_PER_GROUP) & (lane < (g_idx + 1) * EXPERTS_PER_GROUP)
    el = jnp.where(in_grp, logits, neg)
    m1 = jnp.max(el, axis=-1, keepdims=True)
    e1 = jnp.min(jnp.where(el == m1, lane, 2 * LANES), axis=-1, keepdims=True)
    el2 = jnp.where(lane == e1, neg, el)
    m2 = jnp.max(el2, axis=-1, keepdims=True)
    e2 = jnp.min(jnp.where(el2 == m2, lane, 2 * LANES), axis=-1, keepdims=True)
    se = jnp.sum(jnp.exp(el - m1), axis=-1, keepdims=True)
    pv1 = 1.0 / se
    pv2 = jnp.exp(m2 - m1) / se
    w1 = p_top * pv1 / (pv1 + pv2)
    w2 = p_top * pv2 / (pv1 + pv2)

    oh1 = lane == e1
    oh2 = lane == e2
    oh = jnp.where(oh1 | oh2, 1.0, 0.0)
    prefix = _dot(ltri_ref[...], oh.astype(BF16)) + cnt_scr[...]
    r1 = jnp.sum(jnp.where(oh1, prefix, 0.0), axis=-1, keepdims=True)
    r2 = jnp.sum(jnp.where(oh2, prefix, 0.0), axis=-1, keepdims=True)
    cnt_scr[...] = cnt_scr[...] + jnp.sum(oh, axis=0, keepdims=True)
    cnt_o[...] = jnp.broadcast_to(cnt_scr[...], cnt_o.shape)

    route = jnp.where(lane == 0, e1.astype(F32), 0.0)
    route = jnp.where(lane == 1, e2.astype(F32), route)
    route = jnp.where(lane == 2, r1, route)
    route = jnp.where(lane == 3, r2, route)
    route = jnp.where(lane == 4, w1, route)
    route = jnp.where(lane == 5, w2, route)
    route_o[0] = route


def _merge(o_f, o_b, zg, zga, zgm, o_mla, x, mod, wp):
    nb, s, d = x.shape
    tm = min(MERGE_TILE, s)
    const = lambda b, i: (0, 0)
    row = lambda b, i: (b, i, 0)

    def full(a):
        return pl.BlockSpec(a.shape, const)

    def rows(n):
        return pl.BlockSpec((1, tm, n), row)

    ltri = jnp.asarray(np.tril(np.ones((tm, tm), np.float32), -1), BF16)
    return pl.pallas_call(
        _merge_kernel,
        out_shape=[jax.ShapeDtypeStruct((nb, s, d), F32), jax.ShapeDtypeStruct((nb * s * ROW_CHUNKS, LANES), F32),
                   jax.ShapeDtypeStruct((nb, s, LANES), F32), jax.ShapeDtypeStruct((8, LANES), F32)],
        grid=(nb, s // tm),
        in_specs=[rows(1024), rows(1024), rows(1024), rows(1024), rows(1024), rows(MLA_HEADS * MLA_V),
                  rows(d), pl.BlockSpec((1, 8, d), lambda b, i: (b, 0, 0)),
                  full(wp["g_gla"]), full(wp["w_gla_o"]), full(wp["w_mla_o"]), full(wp["w_out"]),
                  full(wp["g2"]), full(wp["w_route"]), full(wp["b_route"]), full(ltri)],
        out_specs=[rows(d), pl.BlockSpec((tm * ROW_CHUNKS, LANES), lambda b, i: (b * (s // tm) + i, 0)),
                   rows(LANES), pl.BlockSpec((8, LANES), const)],
        scratch_shapes=[pltpu.VMEM((1, LANES), F32)],
        compiler_params=_cparams(("arbitrary", "arbitrary")),
        name="merge",
    )(o_f, o_b, zg, zga, zgm, o_mla, x, mod, wp["g_gla"], wp["w_gla_o"], wp["w_mla_o"], wp["w_out"],
      wp["g2"], wp["w_route"], wp["b_route"], ltri)


def _dispatch_kernel(seg_ref, dest_ref, h_ref, xs_out, zero_scr, sem):
    tt = h_ref.shape[0] // ROW_CHUNKS
    tile_rows = MOE_TILE * ROW_CHUNKS

    @pl.when(pl.program_id(0) == 0)
    def _():
        zero_scr[...] = jnp.zeros_like(zero_scr)

        def zero_copy(e):
            r0 = pl.multiple_of((seg_ref[0, e] - MOE_TILE) * ROW_CHUNKS, tile_rows)
            return pltpu.make_async_copy(zero_scr, xs_out.at[pl.ds(r0, tile_rows)], sem.at[1])

        for e in range(N_EXPERTS):
            @pl.when(seg_ref[1, e] > 0)
            def _():
                zero_copy(e).start()
        for e in range(N_EXPERTS):
            @pl.when(seg_ref[1, e] > 0)
            def _():
                zero_copy(e).wait()

        def tail_copy(ti):
            return pltpu.make_async_copy(zero_scr, xs_out.at[pl.ds(pl.multiple_of(ti * tile_rows, tile_rows), tile_rows)],
                                         sem.at[1])

        first_tail = seg_ref[0, N_EXPERTS - 1] // MOE_TILE
        n_tiles = xs_out.shape[0] // tile_rows
        lax.fori_loop(first_tail, n_tiles, lambda ti, c: (tail_copy(ti).start(), c)[1], 0)
        lax.fori_loop(first_tail, n_tiles, lambda ti, c: (tail_copy(ti).wait(), c)[1], 0)

    def copy(r, slot):
        d0 = pl.multiple_of(dest_ref[0, 0, 2 * r + slot] * ROW_CHUNKS, ROW_CHUNKS)
        return pltpu.make_async_copy(h_ref.at[pl.ds(pl.multiple_of(r * ROW_CHUNKS, ROW_CHUNKS), ROW_CHUNKS)],
                                     xs_out.at[pl.ds(d0, ROW_CHUNKS)], sem.at[0])

    def start(r, c):
        copy(r, 0).start(priority=0)
        copy(r, 1).start(priority=1)
        return c

    def wait(r, c):
        copy(r, 0).wait()
        copy(r, 1).wait()
        return c

    lax.fori_loop(0, tt, start, 0, unroll=8)
    lax.fori_loop(0, tt, wait, 0, unroll=8)


def _dispatch(h2, dest, segments, n_rows):
    t = h2.shape[0] // ROW_CHUNKS
    tt = min(MOVE_TILE, t)
    nt = t // tt
    return pl.pallas_call(
        _dispatch_kernel,
        out_shape=jax.ShapeDtypeStruct((n_rows * ROW_CHUNKS, LANES), F32),
        grid_spec=pltpu.PrefetchScalarGridSpec(
            num_scalar_prefetch=1,
            grid=(nt,),
            in_specs=[pl.BlockSpec((1, 1, 2 * tt), lambda i, seg: (i, 0, 0), memory_space=pltpu.SMEM),
                      pl.BlockSpec((tt * ROW_CHUNKS, LANES), lambda i, seg: (i, 0))],
            out_specs=pl.BlockSpec(memory_space=pl.ANY),
            scratch_shapes=[pltpu.VMEM((MOE_TILE * ROW_CHUNKS, LANES), F32), pltpu.SemaphoreType.DMA((2,))]),
        compiler_params=_cparams(("arbitrary",)),
        name="dispatch",
    )(segments, dest.reshape(nt, 1, 2 * tt), h2)


def _expert_kernel(te_ref, na_ref, xs_ref, wg_ref, wu_ref, wd_ref, ys_ref, xb_scr):
    active = pl.program_id(0) < na_ref[0]

    @pl.when(active)
    def _():
        for c in range(ROW_CHUNKS):
            xb_scr[:, c * LANES:(c + 1) * LANES] = xs_ref[pl.ds(c, MOE_TILE, stride=ROW_CHUNKS), :].astype(BF16)
        xb = xb_scr[...]
        a = _silu(_dot(xb, wg_ref[0].astype(BF16))) * _dot(xb, wu_ref[0].astype(BF16))
        y = _dot(a.astype(BF16), wd_ref[0].astype(BF16))
        for c in range(ROW_CHUNKS):
            ys_ref[pl.ds(c, MOE_TILE, stride=ROW_CHUNKS), :] = y[:, c * LANES:(c + 1) * LANES]

    @pl.when(jnp.logical_not(active))
    def _():
        ys_ref[...] = jnp.zeros_like(ys_ref)


def _experts(xs, tile_expert, n_active, wp):
    d = D_MODEL
    n_rows = xs.shape[0] // ROW_CHUNKS
    nt = n_rows // MOE_TILE
    rows = MOE_TILE * ROW_CHUNKS

    def tile(i, te, na):
        return (jnp.minimum(i, na[0] - 1), 0)

    def wmap(i, te, na):
        return (te[jnp.minimum(i, na[0] - 1)], 0, 0)

    return pl.pallas_call(
        _expert_kernel,
        out_shape=jax.ShapeDtypeStruct((n_rows * ROW_CHUNKS, LANES), F32),
        grid_spec=pltpu.PrefetchScalarGridSpec(
            num_scalar_prefetch=2,
            grid=(nt,),
            in_specs=[pl.BlockSpec((rows, LANES), tile),
                      pl.BlockSpec((1, d, D_EXPERT), wmap),
                      pl.BlockSpec((1, d, D_EXPERT), wmap),
                      pl.BlockSpec((1, D_EXPERT, d), wmap)],
            out_specs=pl.BlockSpec((rows, LANES), lambda i, te, na: (i, 0)),
            scratch_shapes=[pltpu.VMEM((MOE_TILE, d), BF16)]),
        compiler_params=_cparams(("arbitrary",)),
        name="experts",
    )(tile_expert, n_active, xs, wp["w_e_gate"], wp["w_e_up"], wp["w_e_down"])


def _combine_kernel(dest_ref, x1_ref, rw_ref, ga_ref, ys_hbm, o_ref, buf, sem):
    tt = x1_ref.shape[0]
    i = pl.program_id(0)
    nt = pl.num_programs(0) - 1
    par = lax.rem(i, 2)

    def copy(p, r, slot, d0):
        return pltpu.make_async_copy(ys_hbm.at[pl.ds(d0, ROW_CHUNKS)],
                                     buf.at[p, slot, pl.ds(pl.multiple_of(r * ROW_CHUNKS, ROW_CHUNKS), ROW_CHUNKS)],
                                     sem.at[p])

    @pl.when(i < nt)
    def _():
        def start(r, c):
            for slot in range(TOP_K):
                d0 = pl.multiple_of(dest_ref[0, 0, 2 * r + slot] * ROW_CHUNKS, ROW_CHUNKS)
                copy(par, r, slot, d0).start(priority=slot)
            return c

        lax.fori_loop(0, tt, start, 0, unroll=8)

    @pl.when(i > 0)
    def _():
        def wait(r, c):
            for slot in range(TOP_K):
                copy(1 - par, r, slot, 0).wait()
            return c

        lax.fori_loop(0, tt, wait, 0, unroll=8)
        rw = rw_ref[...]
        for c in range(ROW_CHUNKS):
            cs = slice(c * LANES, (c + 1) * LANES)
            rows = pl.ds(c, tt, stride=ROW_CHUNKS)
            moe = buf[1 - par, 0, rows, :] * rw[:, 4:5] + buf[1 - par, 1, rows, :] * rw[:, 5:6]
            o_ref[:, cs] = x1_ref[:, cs] + ga_ref[0, 5:6, cs] * moe


def _combine(x1, route, mod, ys, dest, s):
    t, d = x1.shape
    tt = min(MOVE_TILE, s)
    nt = t // tt
    per_b = s // tt
    lag = lambda i: jnp.maximum(i - 1, 0)
    return pl.pallas_call(
        _combine_kernel,
        out_shape=jax.ShapeDtypeStruct((t, d), F32),
        grid=(nt + 1,),
        in_specs=[pl.BlockSpec((1, 1, 2 * tt), lambda i: (jnp.minimum(i, nt - 1), 0, 0), memory_space=pltpu.SMEM),
                  pl.BlockSpec((tt, d), lambda i: (lag(i), 0)),
                  pl.BlockSpec((tt, LANES), lambda i: (lag(i), 0)),
                  pl.BlockSpec((1, 8, d), lambda i: (lag(i) // per_b, 0, 0)),
                  pl.BlockSpec(memory_space=pl.ANY)],
        out_specs=pl.BlockSpec((tt, d), lambda i: (lag(i), 0)),
        scratch_shapes=[pltpu.VMEM((2, TOP_K, tt * ROW_CHUNKS, LANES), F32), pltpu.SemaphoreType.DMA((2,))],
        compiler_params=_cparams(("arbitrary",)),
        name="combine",
    )(dest.reshape(nt, 1, 2 * tt), x1, route, mod, ys)


def _moe(x1, h2, route, counts, mod, wp):
    nb, s, d = x1.shape
    t = nb * s
    route2 = route.reshape(t, LANES)
    e_flat = route2[:, 0:2].astype(jnp.int32)
    rank = route2[:, 2:4].astype(jnp.int32)
    cnt = counts[0, :N_EXPERTS].astype(jnp.int32)
    padded = (cnt + MOE_TILE - 1) // MOE_TILE * MOE_TILE
    pad_end = jnp.cumsum(padded)
    pad_start = pad_end - padded
    is_e = e_flat[..., None] == jnp.arange(N_EXPERTS, dtype=jnp.int32)
    dest = (jnp.sum(jnp.where(is_e, pad_start, 0), axis=-1) + rank).reshape(t * TOP_K)
    n_rows = t * TOP_K + N_EXPERTS * MOE_TILE
    nt = n_rows // MOE_TILE
    tile_row = jnp.arange(nt, dtype=jnp.int32) * MOE_TILE
    tile_expert = jnp.minimum(jnp.sum((pad_end[None, :] <= tile_row[:, None]).astype(jnp.int32), axis=1),
                              N_EXPERTS - 1)
    n_active = (pad_end[-1:] // MOE_TILE).astype(jnp.int32)
    xs = _dispatch(h2, dest, jnp.stack([pad_end, padded]).astype(jnp.int32), n_rows)
    ys = _experts(xs, tile_expert, n_active, wp)
    out = _combine(x1.reshape(t, d), route2, mod, ys, dest, s)
    return out.reshape(nb, s, d)


def _swap_halves(a, axis=-1):
    n = a.shape[axis] // 2
    lo, hi = jnp.split(a, [n], axis=axis)
    return jnp.concatenate([hi, lo], axis=axis)


def _slot(parts, n_lead):
    out = jnp.zeros((n_lead, SLOT), F32)
    for off, a in parts:
        out = out.at[:, off:off + a.shape[-1]].set(a)
    return out


def _prep_weights(s_max, g_norm1, w_in, w_gk_fwd, b_gk_fwd, w_gk_bwd, b_gk_bwd, g_gla_out, w_gla_o, g_q_a,
                  w_q_b, g_kv_a, w_kv_b, g_q_nope, g_k_nope, g_q_rope, g_k_rope, w_mla_o, w_out, g_norm2,
                  w_group, b_group, w_router, b_router, w_e_gate, w_e_up, w_e_down):
    d = D_MODEL
    hk, hv = GLA_HEADS * GLA_DK, GLA_HEADS * GLA_DV
    offs = np.cumsum([0, hk, hk, hv, hv, GLA_RANK, GLA_RANK, MLA_Q_RANK, MLA_KV_RANK, MLA_ROPE, d, d])
    cols = [w_in[:, offs[i]:offs[i + 1]] for i in range(11)]
    wq_, wk_, wv_, wg_, wgf, wgb, wqa, wkva, wkr, wga, wgm = cols
    kr_slot = _slot([(MLA_NOPE, wkr)], d)
    krs_slot = _slot([(MLA_NOPE, _swap_halves(wkr))], d)
    gate_slot = _slot([(0, wgf), (GLA_RANK, wgb)], d)
    w_in_ext = jnp.concatenate([wq_, wk_, wv_, wg_, wga, wgm, wqa, wkva, kr_slot, krs_slot, gate_slot],
                               axis=1).astype(BF16)
    w_gk = jnp.zeros((SLOT, 2 * hk), F32)
    w_gk = w_gk.at[0:GLA_RANK, 0:hk].set(w_gk_fwd).at[GLA_RANK:2 * GLA_RANK, hk:].set(w_gk_bwd).astype(BF16)
    b_gk = jnp.concatenate([b_gk_fwd, b_gk_bwd]).reshape(1, -1)

    wq3 = w_q_b.reshape(MLA_Q_RANK, MLA_HEADS, MLA_NOPE + MLA_ROPE)
    zq = jnp.zeros((MLA_Q_RANK, MLA_HEADS, SLOT - MLA_NOPE - MLA_ROPE), F32)
    w_q = jnp.concatenate([wq3, zq], axis=-1).reshape(MLA_Q_RANK, -1).T.astype(BF16)
    wkv3 = w_kv_b.reshape(MLA_KV_RANK, MLA_HEADS, MLA_NOPE + MLA_V)
    w_k = jnp.concatenate([wkv3[..., :MLA_NOPE], jnp.zeros((MLA_KV_RANK, MLA_HEADS, SLOT - MLA_NOPE), F32)],
                          axis=-1).reshape(MLA_KV_RANK, -1).astype(BF16)
    w_v = jnp.concatenate([wkv3[..., MLA_NOPE:], jnp.zeros((MLA_KV_RANK, MLA_HEADS, MLA_VROWS - MLA_V), F32)],
                          axis=-1).reshape(MLA_KV_RANK, -1).T.astype(BF16)
    v_ones = jnp.zeros((MLA_HEADS, MLA_VROWS), F32).at[:, MLA_V].set(1.0).reshape(-1, 1)

    lanevecs = jnp.concatenate([
        _slot([(0, g_q_nope[None]), (MLA_NOPE, g_q_rope[None])], 1),
        _slot([(MLA_NOPE, _swap_halves(g_q_rope)[None])], 1),
        _slot([(0, g_k_nope[None])], 1),
        _slot([(MLA_NOPE, g_k_rope[None])], 1),
        _slot([(MLA_NOPE, _swap_halves(g_k_rope)[None])], 1),
        jnp.zeros((3, SLOT), F32)], axis=0)
    seg = np.zeros((SLOT, SLOT), np.float32)
    seg[:MLA_NOPE, :MLA_NOPE] = 1.0 / MLA_NOPE
    seg[MLA_NOPE:MLA_NOPE + MLA_ROPE, MLA_NOPE:MLA_NOPE + MLA_ROPE] = 1.0 / MLA_ROPE
    bd = jnp.asarray(seg, BF16)

    half = MLA_ROPE // 2
    inv = ROPE_BASE ** (-jnp.arange(half, dtype=F32) / half)
    ang = jnp.arange(s_max, dtype=F32)[:, None] * inv[None, :]
    cos, sin = jnp.cos(ang), jnp.sin(ang)
    pad = jnp.zeros((s_max, SLOT - MLA_NOPE - MLA_ROPE), F32)
    cos_t = jnp.concatenate([jnp.ones((s_max, MLA_NOPE), F32), cos, cos, pad], axis=1)
    sin_t = jnp.concatenate([jnp.zeros((s_max, MLA_NOPE), F32), -sin, sin, pad], axis=1)

    w_route = jnp.zeros((d, LANES), F32).at[:, :N_EXPERTS].set(w_router)
    w_route = w_route.at[:, N_EXPERTS:N_EXPERTS + N_GROUPS].set(w_group)
    b_route = jnp.zeros((1, LANES), F32).at[0, :N_EXPERTS].set(b_router)
    b_route = b_route.at[0, N_EXPERTS:N_EXPERTS + N_GROUPS].set(b_group)

    return dict(
        g1=g_norm1.reshape(1, d), w_in=w_in_ext, w_gk=w_gk, b_gk=b_gk,
        g_qa=g_q_a.reshape(1, -1), w_q=w_q, g_kva=g_kv_a.reshape(1, -1), w_k=w_k, w_v=w_v,
        v_ones=v_ones, bd=bd, lanevecs=lanevecs, colvecs=lanevecs.T, cos=cos_t, sin=sin_t,
        g_gla=g_gla_out.reshape(1, -1), w_gla_o=w_gla_o.astype(BF16), w_mla_o=w_mla_o.astype(BF16),
        w_out=w_out.astype(BF16), g2=g_norm2.reshape(1, d), w_route=w_route, b_route=b_route,
        w_e_gate=w_e_gate, w_e_up=w_e_up, w_e_down=w_e_down)


def _layer(x, mod, wp):
    q, k, v, zg, zga, zgm, lgf, lgb, qm, km, vm = _inproj(x, mod, wp)
    o_f, o_b = _gla(q, k, v, lgf, lgb)
    o_mla = _mla(qm, km, vm)
    x1, h2, route, counts = _merge(o_f, o_b, zg, zga, zgm, o_mla, x, mod, wp)
    return _moe(x1, h2, route, counts, mod, wp)


def kernel(x_prompt, x_sample, c_prompt, c_sample, w_ada, b_ada, g_norm1, w_in, w_gk_fwd, b_gk_fwd, w_gk_bwd, b_gk_bwd, g_gla_out, w_gla_o, g_q_a, w_q_b, g_kv_a, w_kv_b, g_q_nope, g_k_nope, g_q_rope, g_k_rope, w_mla_o, w_out, g_norm2, w_group, b_group, w_router, b_router, w_e_gate, w_e_up, w_e_down):
    assert w_ada.shape[0] == 1, "single-layer trunk"
    s_max = max(x_prompt.shape[1], x_sample.shape[1])
    wp = _prep_weights(s_max, *[p[0] for p in (
        g_norm1, w_in, w_gk_fwd, b_gk_fwd, w_gk_bwd, b_gk_bwd, g_gla_out, w_gla_o, g_q_a, w_q_b, g_kv_a,
        w_kv_b, g_q_nope, g_k_nope, g_q_rope, g_k_rope, w_mla_o, w_out, g_norm2, w_group, b_group,
        w_router, b_router, w_e_gate, w_e_up, w_e_down)])
    n_prompt = x_prompt.shape[0]
    mod = _ada(jnp.concatenate([c_prompt, c_sample], axis=0), w_ada[0], b_ada[0])
    y_prompt = _layer(x_prompt, mod[:n_prompt], wp)
    y_sample = _layer(x_sample, mod[n_prompt:], wp)
    return (y_prompt, y_sample)
```
